```python
import math
import jax, jax.numpy as jnp
from jax import lax
import numpy as np

D_MODEL = 1024
BATCH = 2
SEQ = 8192
DEPTH = 2

HG_HEADS = 4
HG_W = D_MODEL // 4
HG_DK = HG_W // HG_HEADS
ML_HEADS = 4
ML_W = D_MODEL // 4
ML_DH = ML_W // ML_HEADS
FX_HEADS = 8
FX_W = D_MODEL // 2
FX_DH = FX_W // FX_HEADS
D_MIX = HG_W + ML_W + FX_W
SPLIT_SIZES = (HG_W, HG_W, HG_W, HG_W, ML_W, ML_W, ML_W, ML_HEADS, ML_HEADS, ML_W, FX_W, FX_W, FX_W, FX_HEADS)
D_IN = 4 * HG_W + 4 * ML_W + 2 * ML_HEADS + 3 * FX_W + FX_HEADS
CONV_K = 4
CHUNK = 64
Q_BLOCK = 128
D_FF = ((8 * D_MODEL // 3 + 127) // 128) * 128
N_EXPERTS = 8
TOP_K = 2
D_FF_EXPERT = 7 * D_MODEL // 2
EXPERT_BLOCK = 256
N_DENSE = (DEPTH + 1) // 2
N_MOE = DEPTH // 2
EPS = 1e-6

kernel_name = "hymba_style_hgrn2_mlstm_fox_moe"


def rmsnorm(x, g):
    xf = x.astype(jnp.float32)
    y = xf * lax.rsqrt(jnp.mean(xf * xf, axis=-1, keepdims=True) + EPS)
    return (y * g.astype(jnp.float32)).astype(x.dtype)


def head_rmsnorm(o, gain, heads):
    B, S, W = o.shape
    of = o.reshape(B, S, heads, W // heads)
    of = of * lax.rsqrt(jnp.mean(of * of, axis=-1, keepdims=True) + EPS)
    return of.reshape(B, S, W) * gain.astype(jnp.float32)


def to_chunks(t, heads):
    B, S, W = t.shape
    return t.reshape(B, S // CHUNK, CHUNK, heads, W // heads).transpose(1, 0, 3, 2, 4)


def gate_chunks(t):
    B, S, H = t.shape
    return t.reshape(B, S // CHUNK, CHUNK, H).transpose(1, 0, 3, 2)


def from_chunks(t):
    NC, B, H, C, d = t.shape
    return t.transpose(1, 0, 3, 2, 4).reshape(B, NC * C, H * d)


def causal_conv(x, w):
    C = x.shape[-1]
    return lax.conv_general_dilated(x, w[:, None, :].astype(x.dtype), window_strides=(1,),
                                    padding=[(CONV_K - 1, 0)],
                                    dimension_numbers=('NWC', 'WIO', 'NWC'),
                                    feature_group_count=C)


def hgrn2_mixer(q, fz, i, g, lb, gain):
    f32 = jnp.float32
    B = q.shape[0]
    lb = lb.astype(f32)
    fz = fz.astype(f32)
    log_f = jnp.logaddexp(jnp.log(lb), jnp.log1p(-lb) + jax.nn.log_sigmoid(fz))
    k = (1.0 - lb) * jax.nn.sigmoid(-fz)
    qc = to_chunks(q.astype(f32), HG_HEADS)
    kc = to_chunks(k, HG_HEADS)
    vc = to_chunks(i.astype(f32), HG_HEADS)
    gc = to_chunks(log_f, HG_HEADS)
    mask = jnp.tril(jnp.ones((CHUNK, CHUNK), bool))[:, :, None]

    def step(S, xs):
        qb, kb, vb, gb = xs
        b = jnp.cumsum(gb, axis=2)
        diff = b[:, :, :, None, :] - b[:, :, None, :, :]
        decay = jnp.exp(jnp.where(mask, diff, -jnp.inf))
        scores = jnp.einsum('bhtd,bhsd,bhtsd->bhts', qb, kb, decay)
        o = (jnp.einsum('bhts,bhsv->bhtv', scores, vb)
             + jnp.einsum('bhtd,bhdv->bhtv', qb * jnp.exp(b), S))
        b_end = b[:, :, -1:, :]
        S = (jnp.exp(b_end[:, :, 0, :])[..., None] * S
             + jnp.einsum('bhsd,bhsv->bhdv', kb * jnp.exp(b_end - b), vb))
        return S, o

    S0 = jnp.zeros((B, HG_HEADS, HG_DK, HG_DK), f32)
    _, o = lax.scan(step, S0, (qc, kc, vc, gc))
    o = from_chunks(o)
    return head_rmsnorm(o, gain, HG_HEADS) * jax.nn.silu(g.astype(f32))


def mlstm_mixer(q, k, v, ig, fg, og, conv_w, b_i, b_f, gain):
    f32 = jnp.float32
    B = q.shape[0]
    qk = jax.nn.silu(causal_conv(jnp.concatenate([q, k], axis=-1), conv_w))
    q, k = jnp.split(qk.astype(f32), 2, axis=-1)
    k = k * (ML_DH ** -0.5)
    log_i = ig.astype(f32) + b_i.astype(f32)
    log_f = jax.nn.log_sigmoid(fg.astype(f32) + b_f.astype(f32))
    qc = to_chunks(q, ML_HEADS)
    kc = to_chunks(k, ML_HEADS)
    vc = to_chunks(v.astype(f32), ML_HEADS)
    ic = gate_chunks(log_i)
    fc = gate_chunks(log_f)
    mask = jnp.tril(jnp.ones((CHUNK, CHUNK), bool))

    def step(carry, xs):
        Cm, n, m = carry
        qb, kb, vb, lib, lfb = xs
        b = jnp.cumsum(lfb, axis=-1)
        d_log = jnp.where(mask, b[..., :, None] - b[..., None, :] + lib[..., None, :], -jnp.inf)
        inter = b + m[..., None]
        m_t = jnp.maximum(jnp.max(d_log, axis=-1), inter)
        w = jnp.einsum('bhtd,bhsd->bhts', qb, kb) * jnp.exp(d_log - m_t[..., None])
        s_inter = jnp.exp(inter - m_t)
        num = (jnp.einsum('bhts,bhsv->bhtv', w, vb)
               + s_inter[..., None] * jnp.einsum('bhtd,bhdv->bhtv', qb, Cm))
        den = w.sum(-1) + s_inter * jnp.einsum('bhtd,bhd->bht', qb, n)
        h = num / jnp.maximum(jnp.abs(den), jnp.exp(-m_t))[..., None]
        b_end = b[..., -1]
        log_w = b_end[..., None] - b + lib
        m_new = jnp.maximum(b_end + m, jnp.max(log_w, axis=-1))
        wk = jnp.exp(log_w - m_new[..., None])
        decay = jnp.exp(b_end + m - m_new)
        Cm = decay[..., None, None] * Cm + jnp.einsum('bhs,bhsd,bhsv->bhdv', wk, kb, vb)
        n = decay[..., None] * n + jnp.einsum('bhs,bhsd->bhd', wk, kb)
        return (Cm, n, m_new), h

    init = (jnp.zeros((B, ML_HEADS, ML_DH, ML_DH), f32),
            jnp.zeros((B, ML_HEADS, ML_DH), f32),
            jnp.zeros((B, ML_HEADS), f32))
    _, h = lax.scan(step, init, (qc, kc, vc, ic, fc))
    h = from_chunks(h)
    return head_rmsnorm(h, gain, ML_HEADS) * jax.nn.sigmoid(og.astype(f32))


def fox_mixer(q, k, v, fz, b_f):
    f32 = jnp.float32
    B, S, _ = q.shape

    def heads(t):
        return t.reshape(B, S, FX_HEADS, FX_DH).transpose(0, 2, 1, 3)

    q, k, v = heads(q), heads(k), heads(v)
    log_f = jax.nn.log_sigmoid(fz.astype(f32) + b_f.astype(f32)).transpose(0, 2, 1)
    c = jnp.cumsum(log_f, axis=-1)
    nb = S // Q_BLOCK
    q_blocks = q.reshape(B, FX_HEADS, nb, Q_BLOCK, FX_DH).transpose(2, 0, 1, 3, 4)
    c_blocks = c.reshape(B, FX_HEADS, nb, Q_BLOCK).transpose(2, 0, 1, 3)
    kpos = jnp.arange(S)
    scale = FX_DH ** -0.5

    def block(args):
        qb, cb, bi = args
        qpos = bi * Q_BLOCK + jnp.arange(Q_BLOCK)
        s = (jnp.einsum('bhqd,bhkd->bhqk', qb, k).astype(f32) * scale
             + cb[..., None] - c[:, :, None, :])
        s = jnp.where(kpos[None, :] <= qpos[:, None], s, -jnp.inf)
        p = jax.nn.softmax(s, axis=-1)
        return jnp.einsum('bhqk,bhkd->bhqd', p.astype(v.dtype), v)

    o = lax.map(block, (q_blocks, c_blocks, jnp.arange(nb)))
    o = o.transpose(1, 2, 0, 3, 4).reshape(B, FX_HEADS, S, FX_DH)
    return o.transpose(0, 2, 1, 3).reshape(B, S, FX_W)


def swiglu(h, w_gate, w_up, w_down):
    return (jax.nn.silu(h @ w_gate) * (h @ w_up)) @ w_down


def moe_swiglu(h, router, w_gate, w_up, w_down):
    B, S, D = h.shape
    N = B * S
    xt = h.reshape(N, D)
    logits = (xt @ router).astype(jnp.float32)
    top_val, top_idx = lax.top_k(logits, TOP_K)
    gates = jax.nn.softmax(top_val, axis=-1)
    A = N * TOP_K
    e_flat = top_idx.reshape(A)
    t_flat = jnp.repeat(jnp.arange(N, dtype=jnp.int32), TOP_K)
    g_flat = gates.reshape(A).astype(h.dtype)
    order = jnp.argsort(e_flat)
    e_s, t_s, g_s = e_flat[order], t_flat[order], g_flat[order]
    counts = jax.ops.segment_sum(jnp.ones((A,), jnp.int32), e_flat, num_segments=N_EXPERTS)
    starts = jnp.cumsum(counts) - counts
    padded = (counts + EXPERT_BLOCK - 1) // EXPERT_BLOCK * EXPERT_BLOCK
    pad_end = jnp.cumsum(padded)
    pad_start = pad_end - padded
    dest = pad_start[e_s] + (jnp.arange(A, dtype=jnp.int32) - starts[e_s])
    P = A + N_EXPERTS * EXPERT_BLOCK
    n_blocks = P // EXPERT_BLOCK
    tok_pad = jnp.full((P,), N, jnp.int32).at[dest].set(t_s)
    gate_pad = jnp.zeros((P,), h.dtype).at[dest].set(g_s)
    blk_expert = jnp.minimum(
        jnp.searchsorted(pad_end, jnp.arange(n_blocks) * EXPERT_BLOCK, side='right'),
        N_EXPERTS - 1)
    x_pad = jnp.concatenate([xt, jnp.zeros((1, D), h.dtype)], axis=0)

    def expert_block(args):
        tok, gt, e = args
        xb = x_pad[tok]
        hid = jax.nn.silu(xb @ w_gate[e]) * (xb @ w_up[e])
        return (hid @ w_down[e]) * gt[:, None]

    ys = lax.map(expert_block, (tok_pad.reshape(n_blocks, EXPERT_BLOCK),
                                gate_pad.reshape(n_blocks, EXPERT_BLOCK), blk_expert))
    y = jnp.zeros((N + 1, D), h.dtype).at[tok_pad].add(ys.reshape(P, D))
    return y[:N].reshape(B, S, D)


def setup_inputs(seed: int = 0) -> dict:
    key = jax.random.key(seed)
    ks = jax.random.split(key, 24)
    f32 = jnp.float32

    def nrm(k, shape, scale):
        return jax.random.normal(k, shape, f32) * scale

    return {
        "x": nrm(ks[0], (BATCH, SEQ, D_MODEL), 1.0),
        "attn_norm": 1.0 + nrm(ks[1], (DEPTH, D_MODEL), 0.02),
        "w_in": nrm(ks[2], (DEPTH, D_MODEL, D_IN), D_MODEL ** -0.5),
        "hgrn_lb": nrm(ks[3], (DEPTH, HG_W), 0.5),
        "hgrn_norm": 1.0 + nrm(ks[4], (DEPTH, HG_W), 0.02),
        "mlstm_conv": nrm(ks[5], (DEPTH, CONV_K, 2 * ML_W), CONV_K ** -0.5),
        "mlstm_b_i": nrm(ks[6], (DEPTH, ML_HEADS), 0.1),
        "mlstm_b_f": jnp.linspace(3.0, 6.0, ML_HEADS, dtype=f32)[None, :] + nrm(ks[7], (DEPTH, ML_HEADS), 0.1),
        "mlstm_norm": 1.0 + nrm(ks[8], (DEPTH, ML_W), 0.02),
        "fox_b_f": jnp.linspace(1.0, 6.0, FX_HEADS, dtype=f32)[None, :] + nrm(ks[9], (DEPTH, FX_HEADS), 0.1),
        "w_out": nrm(ks[10], (DEPTH, D_MIX, D_MODEL), D_MIX ** -0.5),
        "ffn_norm": 1.0 + nrm(ks[11], (DEPTH, D_MODEL), 0.02),
        "dense_w_gate": nrm(ks[12], (N_DENSE, D_MODEL, D_FF), D_MODEL ** -0.5),
        "dense_w_up": nrm(ks[13], (N_DENSE, D_MODEL, D_FF), D_MODEL ** -0.5),
        "dense_w_down": nrm(ks[14], (N_DENSE, D_FF, D_MODEL), D_FF ** -0.5),
        "router": nrm(ks[15], (N_MOE, D_MODEL, N_EXPERTS), D_MODEL ** -0.5),
        "moe_w_gate": nrm(ks[16], (N_MOE, N_EXPERTS, D_MODEL, D_FF_EXPERT), D_MODEL ** -0.5),
        "moe_w_up": nrm(ks[17], (N_MOE, N_EXPERTS, D_MODEL, D_FF_EXPERT), D_MODEL ** -0.5),
        "moe_w_down": nrm(ks[18], (N_MOE, N_EXPERTS, D_FF_EXPERT, D_MODEL), D_FF_EXPERT ** -0.5),
        "final_norm": 1.0 + nrm(ks[19], (D_MODEL,), 0.02),
    }


def reference(x, attn_norm, w_in, hgrn_lb, hgrn_norm, mlstm_conv, mlstm_b_i, mlstm_b_f,
              mlstm_norm, fox_b_f, w_out, ffn_norm, dense_w_gate, dense_w_up, dense_w_down,
              router, moe_w_gate, moe_w_up, moe_w_down, final_norm):
    lb_all = jnp.cumsum(jax.nn.softmax(hgrn_lb.astype(jnp.float32), axis=0), axis=0)
    lb_all = lb_all - lb_all[0:1]
    split_idx = np.cumsum(SPLIT_SIZES)[:-1].tolist()
    for l in range(DEPTH):
        h = rmsnorm(x, attn_norm[l])
        (hq, hf, hi, hg, mq, mk, mv, mi, mf, mo, fq, fk, fv, ff) = jnp.split(h @ w_in[l], split_idx, axis=-1)
        o_hg = hgrn2_mixer(hq, hf, hi, hg, lb_all[l], hgrn_norm[l])
        o_ml = mlstm_mixer(mq, mk, mv, mi, mf, mo, mlstm_conv[l], mlstm_b_i[l], mlstm_b_f[l], mlstm_norm[l])
        o_fx = fox_mixer(fq, fk, fv, ff, fox_b_f[l])
        mix = jnp.concatenate([o_hg.astype(x.dtype), o_ml.astype(x.dtype), o_fx.astype(x.dtype)], axis=-1)
        x = x + mix @ w_out[l]
        h = rmsnorm(x, ffn_norm[l])
        if l % 2 == 0:
            j = l // 2
            x = x + swiglu(h, dense_w_gate[j], dense_w_up[j], dense_w_down[j])
        else:
            j = l // 2
            x = x + moe_swiglu(h, router[j], moe_w_gate[j], moe_w_up[j], moe_w_down[j])
    return rmsnorm(x, final_norm)
```

```python
import functools

import jax
import jax.numpy as jnp
from jax import lax
from jax.experimental import pallas as pl
from jax.experimental.pallas import tpu as pltpu

F32 = jnp.float32
BF16 = jnp.bfloat16
EPS = 1e-6
NEG_INF = float("-inf")

HEAD_DIM = 64
HG_W = 256
ML_W = 256
FX_W = 512
ML_HEADS = 4
FX_HEADS = 8
CONV_K = 4
N_EXPERTS = 8
TOP_K = 2
N_GATE_ROWS = 16

LANES = 128
VMEM_LIMIT = 48 * 1024 * 1024

TM_PROJ = 512
T_GATE = 256
T_HG = 64
T_ML = 128
TQ_FX = 512
TK_FX = 512
TM_FFN = 512
BM_MOE = 512
TF_MOE = 512
T_DISPATCH = 256


def _cparams(sem, vmem=VMEM_LIMIT):
    return pltpu.CompilerParams(dimension_semantics=sem, vmem_limit_bytes=vmem)


def _split3(x):
    hi = x.astype(BF16)
    r = x - hi.astype(F32)
    mid = r.astype(BF16)
    lo = (r - mid.astype(F32)).astype(BF16)
    return hi, mid, lo


def _dot(a, b):
    return jnp.dot(a, b, preferred_element_type=F32)


def _dot_nt(a, b):
    return lax.dot_general(a, b, (((1,), (1,)), ((), ())), preferred_element_type=F32)


def _dot_tn(a, b):
    return lax.dot_general(a, b, (((0,), (0,)), ((), ())), preferred_element_type=F32)


def _dot3(parts, b):
    return _dot(parts[0], b) + _dot(parts[1], b) + _dot(parts[2], b)


def _log_sigmoid(z):
    return -(jnp.maximum(-z, 0.0) + jnp.log1p(jnp.exp(-jnp.abs(z))))


def _sigmoid(z):
    return 1.0 / (1.0 + jnp.exp(-z))


def _head_mean_sq(o, m_bf):
    o2 = o * o
    hi = o2.astype(BF16)
    lo = (o2 - hi.astype(F32)).astype(BF16)
    return (_dot(hi, m_bf) + _dot(lo, m_bf)) * (1.0 / HEAD_DIM)


def _norm_inproj_kernel(x_ref, g_ref, w_ref, wgt_ref, main_ref, grow_ref, *, tn):
    x = x_ref[...]
    ms = jnp.mean(x * x, axis=-1, keepdims=True)
    h = ((x * lax.rsqrt(ms + EPS)) * g_ref[...]).astype(BF16)
    for j in range(w_ref.shape[1] // tn):
        main_ref[:, j * tn:(j + 1) * tn] = _dot(h, w_ref[:, j * tn:(j + 1) * tn]).astype(BF16)
    grow_ref[...] = _dot_nt(wgt_ref[...], h)


def norm_inproj(x, g, w_main, w_gate_t):
    n, d = x.shape
    wm = w_main.shape[1]
    tm = min(TM_PROJ, n)
    return pl.pallas_call(
        functools.partial(_norm_inproj_kernel, tn=512),
        out_shape=(jax.ShapeDtypeStruct((n, wm), BF16), jax.ShapeDtypeStruct((N_GATE_ROWS, n), F32)),
        grid=(n // tm,),
        in_specs=[pl.BlockSpec((tm, d), lambda i: (i, 0)),
                  pl.BlockSpec((1, d), lambda i: (0, 0)),
                  pl.BlockSpec((d, wm), lambda i: (0, 0)),
                  pl.BlockSpec((N_GATE_ROWS, d), lambda i: (0, 0))],
        out_specs=(pl.BlockSpec((tm, wm), lambda i: (i, 0)),
                   pl.BlockSpec((N_GATE_ROWS, tm), lambda i: (0, i))),
        compiler_params=_cparams(("arbitrary",)),
        name="norm_inproj",
    )(x, g, w_main, w_gate_t)


def _gates_kernel(g_ref, bias_ref, grow_ref, gcol_ref, carry_ref):
    t = g_ref.shape[1]

    @pl.when(pl.program_id(1) == 0)
    def _():
        carry_ref[...] = jnp.zeros_like(carry_ref)

    z = g_ref[...] + bias_ref[...]
    row = lax.broadcasted_iota(jnp.int32, z.shape, 0)
    is_input_gate = row < ML_HEADS
    val = jnp.where(is_input_gate, 0.0, _log_sigmoid(z))
    r_i = lax.broadcasted_iota(jnp.int32, (t, t), 0)
    c_i = lax.broadcasted_iota(jnp.int32, (t, t), 1)
    upper = jnp.where(r_i <= c_i, 1.0, 0.0).astype(BF16)
    tot = _dot3(_split3(val), upper) + carry_ref[:, 0:1]
    out = jnp.where(is_input_gate, z, tot)
    grow_ref[...] = out
    carry_ref[...] = jnp.broadcast_to(tot[:, t - 1:t], carry_ref.shape)
    eye = jnp.where(r_i == c_i, 1.0, 0.0).astype(BF16)
    p0, p1, p2 = _split3(out)
    gcol_ref[...] = _dot_nt(eye, p0) + _dot_nt(eye, p1) + _dot_nt(eye, p2)


def gates(g_row, bias, batch):
    r, n = g_row.shape
    s = n // batch
    t = min(T_GATE, s)
    nb = s // t
    return pl.pallas_call(
        _gates_kernel,
        out_shape=(jax.ShapeDtypeStruct((r, n), F32), jax.ShapeDtypeStruct((n, r), F32)),
        grid=(batch, nb),
        in_specs=[pl.BlockSpec((r, t), lambda b, j: (0, b * nb + j)),
                  pl.BlockSpec((r, 1), lambda b, j: (0, 0))],
        out_specs=(pl.BlockSpec((r, t), lambda b, j: (0, b * nb + j)),
                   pl.BlockSpec((t, r), lambda b, j: (b * nb + j, 0))),
        scratch_shapes=[pltpu.VMEM((r, LANES), F32)],
        compiler_params=_cparams(("arbitrary", "arbitrary")),
        name="gates",
    )(g_row, bias)


def _hgrn_kernel(q_ref, f_ref, i_ref, g_ref, lb_ref, gain_ref, m_ref, o_ref, st_ref, x_ref, y_ref, *, layer):
    t = q_ref.shape[0]
    grp = 16
    ngrp = t // grp

    @pl.when(pl.program_id(1) == 0)
    def _():
        st_ref[...] = jnp.zeros_like(st_ref)

    lbp = lb_ref[...]
    rows = [lbp[r:r + 1, :] for r in range(lbp.shape[0])]
    mx = functools.reduce(jnp.maximum, rows)
    es = [jnp.exp(r - mx) for r in rows]
    tot = functools.reduce(lambda a, b: a + b, es)
    cs, run = [], None
    for e in es:
        run = e / tot if run is None else run + e / tot
        cs.append(run)
    lb = cs[layer] - cs[0]

    z = f_ref[...].astype(F32)
    a = jnp.log(lb)
    bb = jnp.log1p(-lb) + _log_sigmoid(z)
    log_f = jnp.maximum(a, bb) + jnp.log1p(jnp.exp(-jnp.abs(a - bb)))
    k = (1.0 - lb) * _sigmoid(-z)
    q = q_ref[...].astype(F32)
    v = i_ref[...].astype(F32)
    m_bf = m_ref[...]

    r_i = lax.broadcasted_iota(jnp.int32, (t, t), 0)
    c_i = lax.broadcasted_iota(jnp.int32, (t, t), 1)
    lower = jnp.where(c_i <= r_i, 1.0, 0.0).astype(BF16)
    f0, f1, f2 = _split3(log_f)
    b = _dot(lower, f0) + _dot(lower, f1) + _dot(lower, f2)

    st = st_ref[...]
    o_inter = _dot_nt((q * jnp.exp(b)).astype(BF16), st.astype(BF16))

    qg = [q[g * grp:(g + 1) * grp, :] for g in range(ngrp)]
    bg = [b[g * grp:(g + 1) * grp, :] for g in range(ngrp)]
    t_in_grp = lax.broadcasted_iota(jnp.int32, (grp, q.shape[1]), 0)
    off = 0
    for s in range(t):
        gs = s // grp
        bs = b[s:s + 1, :]
        ks = k[s:s + 1, :]
        for g in range(gs, ngrp):
            diff = bg[g] - bs
            if g == gs:
                diff = jnp.where(t_in_grp >= (s - gs * grp), diff, NEG_INF)
            x_ref[off:off + grp, :] = (qg[g] * (ks * jnp.exp(diff))).astype(BF16)
            off += grp
    y_ref[...] = _dot(x_ref[...], m_bf)
    og = [o_inter[g * grp:(g + 1) * grp, :] for g in range(ngrp)]
    off = 0
    for s in range(t):
        gs = s // grp
        vs = v[s:s + 1, :]
        for g in range(gs, ngrp):
            og[g] = og[g] + y_ref[off:off + grp, :] * vs
            off += grp
    o = jnp.concatenate(og, axis=0)

    b_end = b[t - 1:t, :]
    kd = k * jnp.exp(b_end - b)
    upd = _dot_tn(v.astype(BF16), kd.astype(BF16))
    st_ref[...] = st * jnp.exp(b_end) + upd * m_bf.astype(F32)

    gt = g_ref[...].astype(F32)
    y = o * lax.rsqrt(_head_mean_sq(o, m_bf) + EPS) * gain_ref[...] * (gt * _sigmoid(gt))
    o_ref[...] = y.astype(o_ref.dtype)


def hgrn2(main, lb_all, gain, m_bf, batch, layer):
    n = main.shape[0]
    s = n // batch
    t = T_HG
    nc = s // t
    w = HG_W
    n_pairs = sum(t // 16 - si // 16 for si in range(t)) * 16
    col = lambda cidx: pl.BlockSpec((t, w), lambda b, c: (b * nc + c, cidx))
    full = lambda shape: pl.BlockSpec(shape, lambda b, c: (0, 0))
    return pl.pallas_call(
        functools.partial(_hgrn_kernel, layer=layer),
        out_shape=jax.ShapeDtypeStruct((n, w), BF16),
        grid=(batch, nc),
        in_specs=[col(0), col(1), col(2), col(3), full(lb_all.shape), full((1, w)), full((w, w))],
        out_specs=pl.BlockSpec((t, w), lambda b, c: (b * nc + c, 0)),
        scratch_shapes=[pltpu.VMEM((w, w), F32), pltpu.VMEM((n_pairs, w), BF16), pltpu.VMEM((n_pairs, w), F32)],
        compiler_params=_cparams(("arbitrary", "arbitrary")),
        name="hgrn2",
    )(main, main, main, main, lb_all, gain, m_bf)


def _mlstm_kernel(q_ref, k_ref, v_ref, og_ref, grow_ref, gcol_ref, cw_ref, gain_ref, m_ref, o_ref,
                  ext_ref, ct_ref, n_ref, mm_ref):
    t = q_ref.shape[0]
    w = q_ref.shape[1]
    halo = 8

    @pl.when(pl.program_id(1) == 0)
    def _():
        ext_ref[0:halo, :] = jnp.zeros((halo, 2 * w), F32)
        ct_ref[...] = jnp.zeros_like(ct_ref)
        n_ref[...] = jnp.zeros_like(n_ref)
        mm_ref[...] = jnp.zeros_like(mm_ref)

    ext_ref[halo:halo + t, 0:w] = q_ref[...].astype(F32)
    ext_ref[halo:halo + t, w:2 * w] = k_ref[...].astype(F32)
    cw = cw_ref[...]
    y = None
    for j in range(CONV_K):
        term = ext_ref[halo - (CONV_K - 1) + j:halo - (CONV_K - 1) + j + t, :] * cw[j:j + 1, :]
        y = term if y is None else y + term
    tail = ext_ref[t:t + halo, :]
    ext_ref[0:halo, :] = tail
    qk = y * _sigmoid(y)
    q = qk[:, 0:w]
    k = qk[:, w:2 * w] * (HEAD_DIM ** -0.5)
    kb = k.astype(BF16)
    vb = v_ref[...]
    m_bf = m_ref[...]

    grow = grow_ref[...]
    gcol = gcol_ref[...]
    lane_head = lax.broadcasted_iota(jnp.int32, (1, w), 1) // HEAD_DIM
    r_i = lax.broadcasted_iota(jnp.int32, (t, t), 0)
    c_i = lax.broadcasted_iota(jnp.int32, (t, t), 1)
    causal = c_i <= r_i

    num_intra = jnp.zeros((t, w), F32)
    sint_l = jnp.zeros((t, w), F32)
    wsum_l = jnp.zeros((t, w), F32)
    mt_l = jnp.zeros((t, w), F32)
    wk_l = jnp.zeros((t, w), F32)
    decay_l = jnp.zeros((1, w), F32)
    for h in range(ML_HEADS):
        sel = lane_head == h
        qh = jnp.where(sel, q, 0.0).astype(BF16)
        s = _dot_nt(qh, kb)
        bc = gcol[:, ML_HEADS + h:ML_HEADS + h + 1]
        br = grow[ML_HEADS + h:ML_HEADS + h + 1, :]
        lir = grow[h:h + 1, :]
        lic = gcol[:, h:h + 1]
        dlog = jnp.where(causal, bc - br + lir, NEG_INF)
        mmh = mm_ref[h:h + 1, 0:1]
        inter = bc + mmh
        m_t = jnp.maximum(jnp.max(dlog, axis=1, keepdims=True), inter)
        wgt = s * jnp.exp(dlog - m_t)
        s_int = jnp.exp(inter - m_t)
        pv = _dot(wgt.astype(BF16), vb)
        num_intra = jnp.where(sel, pv, num_intra)
        sint_l = jnp.where(sel, s_int, sint_l)
        wsum_l = jnp.where(sel, jnp.sum(wgt, axis=1, keepdims=True), wsum_l)
        mt_l = jnp.where(sel, m_t, mt_l)
        b_end = br[:, t - 1:t]
        m_new = jnp.maximum(b_end + mmh, jnp.max(b_end - br + lir, axis=1, keepdims=True))
        wk_l = jnp.where(sel, jnp.exp(b_end - bc + lic - m_new), wk_l)
        decay_l = jnp.where(sel, jnp.exp(b_end + mmh - m_new), decay_l)
        mm_ref[h:h + 1, :] = jnp.broadcast_to(m_new - b_end, (1, mm_ref.shape[1]))

    ct = ct_ref[...]
    nrow = n_ref[0:1, :]
    q_c = _dot_nt(q.astype(BF16), ct.astype(BF16))
    qn = q * nrow
    qn_hi = qn.astype(BF16)
    qn_lo = (qn - qn_hi.astype(F32)).astype(BF16)
    qn_l = _dot(qn_hi, m_bf) + _dot(qn_lo, m_bf)
    num = num_intra + sint_l * q_c
    den = wsum_l + sint_l * qn_l
    hval = num / jnp.maximum(jnp.abs(den), jnp.exp(-mt_l))

    kw = k * wk_l
    upd = _dot_tn(vb, kw.astype(BF16))
    ct_ref[...] = decay_l * ct + upd * m_bf.astype(F32)
    n_ref[...] = jnp.broadcast_to(decay_l * nrow + jnp.sum(kw, axis=0, keepdims=True), n_ref.shape)

    og = og_ref[...].astype(F32)
    yv = hval * lax.rsqrt(_head_mean_sq(hval, m_bf) + EPS) * gain_ref[...] * _sigmoid(og)
    o_ref[...] = yv.astype(o_ref.dtype)


def mlstm(main, g_row, g_col, conv_w, gain, m_bf, batch):
    n = main.shape[0]
    s = n // batch
    t = min(T_ML, s)
    nc = s // t
    w = ML_W
    col = lambda cidx: pl.BlockSpec((t, w), lambda b, c: (b * nc + c, cidx))
    full = lambda shape: pl.BlockSpec(shape, lambda b, c: (0, 0))
    return pl.pallas_call(
        _mlstm_kernel,
        out_shape=jax.ShapeDtypeStruct((n, w), BF16),
        grid=(batch, nc),
        in_specs=[col(4), col(5), col(6), col(7),
                  pl.BlockSpec((N_GATE_ROWS, t), lambda b, c: (0, b * nc + c)),
                  pl.BlockSpec((t, N_GATE_ROWS), lambda b, c: (b * nc + c, 0)),
                  full((CONV_K, 2 * w)), full((1, w)), full((w, w))],
        out_specs=pl.BlockSpec((t, w), lambda b, c: (b * nc + c, 0)),
        scratch_shapes=[pltpu.VMEM((t + 8, 2 * w), F32), pltpu.VMEM((w, w), F32),
                        pltpu.VMEM((8, w), F32), pltpu.VMEM((8, LANES), F32)],
        compiler_params=_cparams(("arbitrary", "arbitrary")),
        name="mlstm",
    )(main, main, main, main, g_row, g_col, conv_w, gain, m_bf)


def _fox_kernel(q_ref, k_ref, v_ref, ck_ref, cq_ref, o_ref, m_ref, l_ref, acc_ref):
    tq = q_ref.shape[0]
    tk = k_ref.shape[0]
    qi = pl.program_id(2)
    kj = pl.program_id(3)
    lane = lax.broadcasted_iota(jnp.int32, (1, LANES), 1)
    first = lane < HEAD_DIM

    @pl.when(kj == 0)
    def _():
        m_ref[...] = jnp.full(m_ref.shape, NEG_INF, F32)
        l_ref[...] = jnp.zeros_like(l_ref)
        acc_ref[...] = jnp.zeros_like(acc_ref)

    def step(diag):
        q2 = q_ref[...]
        k2 = k_ref[...]
        v2 = v_ref[...]
        alphas, pvs = [], []
        for a in range(2):
            sel = first if a == 0 else jnp.logical_not(first)
            qa = jnp.where(sel, q2, jnp.zeros_like(q2)) * jnp.asarray(HEAD_DIM ** -0.5, q2.dtype)
            s = _dot_nt(qa, k2)
            s = s + (cq_ref[0, a:a + 1, 0:1] - ck_ref[0, a:a + 1, :])
            if diag:
                r_i = lax.broadcasted_iota(jnp.int32, (tq, tk), 0)
                c_i = lax.broadcasted_iota(jnp.int32, (tq, tk), 1)
                s = jnp.where(c_i <= r_i, s, NEG_INF)
            m_prev = m_ref[a]
            m_new = jnp.maximum(m_prev, jnp.max(s, axis=1, keepdims=True))
            p = jnp.exp(s - m_new)
            alpha = jnp.exp(m_prev - m_new)
            l_ref[a] = alpha * l_ref[a] + jnp.sum(p, axis=1, keepdims=True)
            m_ref[a] = m_new
            alphas.append(alpha)
            pvs.append(_dot(p.astype(v2.dtype), v2))
        acc_ref[...] = jnp.where(first, alphas[0], alphas[1]) * acc_ref[...] + jnp.where(first, pvs[0], pvs[1])

    @pl.when(kj < qi)
    def _():
        step(False)

    @pl.when(kj == qi)
    def _():
        step(True)
        o_ref[...] = (acc_ref[...] / jnp.where(first, l_ref[0], l_ref[1])).astype(o_ref.dtype)


def fox(main, c_fox, batch):
    n = main.shape[0]
    s = n // batch
    tq = min(TQ_FX, s)
    tk = tq
    nq = s // tq
    pairs = FX_HEADS // 2
    qcol, kcol, vcol = 2048 // LANES, 2560 // LANES, 3072 // LANES
    kv_blk = lambda b, p, i, j: b * nq + jnp.minimum(i, j)
    return pl.pallas_call(
        _fox_kernel,
        out_shape=jax.ShapeDtypeStruct((n, FX_W), BF16),
        grid=(batch, pairs, nq, nq),
        in_specs=[pl.BlockSpec((tq, LANES), lambda b, p, i, j: (b * nq + i, qcol + p)),
                  pl.BlockSpec((tk, LANES), lambda b, p, i, j: (kv_blk(b, p, i, j), kcol + p)),
                  pl.BlockSpec((tk, LANES), lambda b, p, i, j: (kv_blk(b, p, i, j), vcol + p)),
                  pl.BlockSpec((1, 2, tk), lambda b, p, i, j: (p, 0, kv_blk(b, p, i, j))),
                  pl.BlockSpec((1, 2, tq), lambda b, p, i, j: (p, 0, b * nq + i))],
        out_specs=pl.BlockSpec((tq, LANES), lambda b, p, i, j: (b * nq + i, p)),
        scratch_shapes=[pltpu.VMEM((2, tq, 1), F32), pltpu.VMEM((2, tq, 1), F32), pltpu.VMEM((tq, LANES), F32)],
        compiler_params=_cparams(("arbitrary", "arbitrary", "arbitrary", "arbitrary")),
        name="fox",
    )(main, main, main, c_fox, c_fox)


def _outproj_body(x_ref, ohg_ref, oml_ref, ofx_ref, w_ref, g_ref):
    acc = x_ref[...]
    acc = acc + _dot(ohg_ref[...], w_ref[0:HG_W, :])
    acc = acc + _dot(oml_ref[...], w_ref[HG_W:HG_W + ML_W, :])
    acc = acc + _dot(ofx_ref[...], w_ref[HG_W + ML_W:, :])
    ms = jnp.mean(acc * acc, axis=-1, keepdims=True)
    h = (acc * lax.rsqrt(ms + EPS)) * g_ref[...]
    return acc, h


def _outproj_dense_kernel(x_ref, ohg_ref, oml_ref, ofx_ref, w_ref, g_ref, xo_ref, h_ref):
    acc, h = _outproj_body(x_ref, ohg_ref, oml_ref, ofx_ref, w_ref, g_ref)
    xo_ref[...] = acc
    h_ref[...] = h.astype(h_ref.dtype)


def _outproj_moe_kernel(x_ref, ohg_ref, oml_ref, ofx_ref, w_ref, g_ref, rhi_ref, rlo_ref,
                        xo_ref, h_ref, idx_ref, gate_ref):
    acc, h = _outproj_body(x_ref, ohg_ref, oml_ref, ofx_ref, w_ref, g_ref)
    xo_ref[...] = acc
    h_ref[...] = h
    h_hi = h.astype(BF16)
    h_lo = (h - h_hi.astype(F32)).astype(BF16)
    logits = _dot(h_hi, rhi_ref[...]) + _dot(h_hi, rlo_ref[...]) + _dot(h_lo, rhi_ref[...])
    lane_i = lax.broadcasted_iota(jnp.int32, logits.shape, 1)
    lane = lane_i.astype(F32)
    lg = jnp.where(lane_i < N_EXPERTS, logits, NEG_INF)
    m1 = jnp.max(lg, axis=1, keepdims=True)
    i1 = jnp.min(jnp.where(lg == m1, lane, float(LANES)), axis=1, keepdims=True)
    lg2 = jnp.where(lane == i1, NEG_INF, lg)
    m2 = jnp.max(lg2, axis=1, keepdims=True)
    i2 = jnp.min(jnp.where(lg2 == m2, lane, float(LANES)), axis=1, keepdims=True)
    e = jnp.exp(m2 - m1)
    g1 = 1.0 / (1.0 + e)
    g2 = e / (1.0 + e)
    idx_ref[...] = jnp.where(lane_i == 0, i1, jnp.where(lane_i == 1, i2, 0.0)).astype(jnp.int32)
    gate_ref[...] = jnp.where(lane_i == 0, g1, jnp.where(lane_i == 1, g2, 0.0))


def outproj(x, o_hg, o_ml, o_fx, w_out, g, router_parts=None):
    n, d = x.shape
    tm = min(TM_PROJ, n)
    row = lambda width: pl.BlockSpec((tm, width), lambda i: (i, 0))
    full = lambda shape: pl.BlockSpec(shape, lambda i: (0, 0))
    in_specs = [row(d), row(HG_W), row(ML_W), row(FX_W), full(w_out.shape), full((1, d))]
    args = [x, o_hg, o_ml, o_fx, w_out, g]
    if router_parts is None:
        kern = _outproj_dense_kernel
        out_shape = (jax.ShapeDtypeStruct((n, d), F32), jax.ShapeDtypeStruct((n, d), BF16))
        out_specs = (row(d), row(d))
    else:
        kern = _outproj_moe_kernel
        in_specs += [full(router_parts[0].shape), full(router_parts[1].shape)]
        args += list(router_parts)
        out_shape = (jax.ShapeDtypeStruct((n, d), F32), jax.ShapeDtypeStruct((n, d), F32),
                     jax.ShapeDtypeStruct((n, LANES), jnp.int32), jax.ShapeDtypeStruct((n, LANES), F32))
        out_specs = (row(d), row(d), row(LANES), row(LANES))
    return pl.pallas_call(
        kern, out_shape=out_shape, grid=(n // tm,), in_specs=in_specs, out_specs=out_specs,
        compiler_params=_cparams(("arbitrary",)), name="outproj",
    )(*args)


def _ffn_kernel(h_ref, x_ref, wg_ref, wu_ref, wd_ref, o_ref):
    @pl.when(pl.program_id(1) == 0)
    def _():
        o_ref[...] = x_ref[...]

    h = h_ref[...]
    gt = _dot(h, wg_ref[...])
    up = _dot(h, wu_ref[...])
    hid = (gt * _sigmoid(gt) * up).astype(BF16)
    o_ref[...] += _dot(hid, wd_ref[...])


def dense_ffn(h, x, wg, wu, wd):
    n, d = x.shape
    ff = wg.shape[1]
    tm = min(TM_FFN, n)
    tf = ff // 2 if (ff // 2) % LANES == 0 else ff
    return pl.pallas_call(
        _ffn_kernel,
        out_shape=jax.ShapeDtypeStruct((n, d), F32),
        grid=(n // tm, ff // tf),
        in_specs=[pl.BlockSpec((tm, d), lambda i, f: (i, 0)),
                  pl.BlockSpec((tm, d), lambda i, f: (i, 0)),
                  pl.BlockSpec((d, tf), lambda i, f: (0, f)),
                  pl.BlockSpec((d, tf), lambda i, f: (0, f)),
                  pl.BlockSpec((tf, d), lambda i, f: (f, 0))],
        out_specs=pl.BlockSpec((tm, d), lambda i, f: (i, 0)),
        compiler_params=_cparams(("arbitrary", "arbitrary")),
        name="dense_ffn",
    )(h, x, wg, wu, wd)


def _row_copy(src_ref, src_row, dst_ref, dst_row, sem):
    return pltpu.make_async_copy(src_ref.at[pl.ds(src_row, 1)], dst_ref.at[pl.ds(dst_row, 1)], sem)


def _dispatch_kernel(dest_ref, h_ref, xs_in_ref, xs_ref, sem, *, tb):
    del xs_in_ref
    base = pl.program_id(0) * tb

    def issue(i, carry):
        for kk in range(TOP_K):
            _row_copy(h_ref, base + i, xs_ref, dest_ref[(base + i) * TOP_K + kk], sem).start()
        return carry

    lax.fori_loop(0, tb, issue, 0)

    def drain(i, carry):
        for kk in range(TOP_K):
            _row_copy(h_ref, base + i, xs_ref, dest_ref[(base + i) * TOP_K + kk], sem).wait()
        return carry

    lax.fori_loop(0, tb, drain, 0)


def moe_dispatch(dest, h, p_rows):
    n, d = h.shape
    tb = min(T_DISPATCH, n)
    xs0 = jnp.zeros((p_rows, d), h.dtype)
    return pl.pallas_call(
        functools.partial(_dispatch_kernel, tb=tb),
        out_shape=jax.ShapeDtypeStruct((p_rows, d), h.dtype),
        grid_spec=pltpu.PrefetchScalarGridSpec(
            num_scalar_prefetch=1, grid=(n // tb,),
            in_specs=[pl.BlockSpec(memory_space=pl.ANY), pl.BlockSpec(memory_space=pl.ANY)],
            out_specs=pl.BlockSpec(memory_space=pl.ANY),
            scratch_shapes=[pltpu.SemaphoreType.DMA(())]),
        input_output_aliases={2: 0},
        compiler_params=_cparams(("arbitrary",)),
        name="moe_dispatch",
    )(dest, h, xs0)


def _experts_kernel(blk_e_ref, nused_ref, x_ref, wg_ref, wu_ref, wd_ref, o_ref):
    m = pl.program_id(0)
    f = pl.program_id(1)

    @pl.when(f == 0)
    def _():
        o_ref[...] = jnp.zeros_like(o_ref)

    @pl.when(m < nused_ref[0])
    def _():
        xb = x_ref[...].astype(BF16)
        gt = _dot(xb, wg_ref[0])
        up = _dot(xb, wu_ref[0])
        hid = (gt * _sigmoid(gt) * up).astype(BF16)
        o_ref[...] += _dot(hid, wd_ref[0])


def moe_experts(blk_expert, n_used, xs, wg, wu, wd):
    p_rows, d = xs.shape
    ff = wg.shape[2]
    bm = BM_MOE
    tf = TF_MOE
    nf = ff // tf

    def w_idx(m, f, be, nu):
        live = m < nu[0]
        return be[m], jnp.where(live, f, nf - 1)

    return pl.pallas_call(
        _experts_kernel,
        out_shape=jax.ShapeDtypeStruct((p_rows, d), F32),
        grid_spec=pltpu.PrefetchScalarGridSpec(
            num_scalar_prefetch=2, grid=(p_rows // bm, nf),
            in_specs=[pl.BlockSpec((bm, d), lambda m, f, be, nu: (m, 0)),
                      pl.BlockSpec((1, d, tf), lambda m, f, be, nu: (w_idx(m, f, be, nu)[0], 0, w_idx(m, f, be, nu)[1])),
                      pl.BlockSpec((1, d, tf), lambda m, f, be, nu: (w_idx(m, f, be, nu)[0], 0, w_idx(m, f, be, nu)[1])),
                      pl.BlockSpec((1, tf, d), lambda m, f, be, nu: (w_idx(m, f, be, nu)[0], w_idx(m, f, be, nu)[1], 0))],
            out_specs=pl.BlockSpec((bm, d), lambda m, f, be, nu: (m, 0))),
        compiler_params=_cparams(("arbitrary", "arbitrary")),
        name="moe_experts",
    )(blk_expert, n_used, xs, wg, wu, wd)


def _combine_kernel(dest_ref, x_ref, gate_ref, fg_ref, ys_ref, o_ref, buf_ref, sem, *, tb, final):
    base = pl.program_id(0) * tb

    def issue(i, carry):
        for kk in range(TOP_K):
            _row_copy(ys_ref, dest_ref[(base + i) * TOP_K + kk], buf_ref.at[kk], i, sem).start()
        return carry

    lax.fori_loop(0, tb, issue, 0)

    def drain(i, carry):
        for kk in range(TOP_K):
            _row_copy(ys_ref, dest_ref[(base + i) * TOP_K + kk], buf_ref.at[kk], i, sem).wait()
        return carry

    lax.fori_loop(0, tb, drain, 0)
    gate = gate_ref[...]
    y = x_ref[...] + gate[:, 0:1] * buf_ref[0] + gate[:, 1:2] * buf_ref[1]
    if final:
        ms = jnp.mean(y * y, axis=-1, keepdims=True)
        y = (y * lax.rsqrt(ms + EPS)) * fg_ref[...]
    o_ref[...] = y


def moe_combine(dest, x, gate, final_g, ys, final):
    n, d = x.shape
    tb = min(T_DISPATCH, n)
    return pl.pallas_call(
        functools.partial(_combine_kernel, tb=tb, final=final),
        out_shape=jax.ShapeDtypeStruct((n, d), F32),
        grid_spec=pltpu.PrefetchScalarGridSpec(
            num_scalar_prefetch=1, grid=(n // tb,),
            in_specs=[pl.BlockSpec((tb, d), lambda i, dr: (i, 0)),
                      pl.BlockSpec((tb, LANES), lambda i, dr: (i, 0)),
                      pl.BlockSpec((1, d), lambda i, dr: (0, 0)),
                      pl.BlockSpec(memory_space=pl.ANY)],
            out_specs=pl.BlockSpec((tb, d), lambda i, dr: (i, 0)),
            scratch_shapes=[pltpu.VMEM((TOP_K, tb, d), F32), pltpu.SemaphoreType.DMA(())]),
        compiler_params=_cparams(("arbitrary",)),
        name="moe_combine",
    )(dest, x, gate, final_g, ys)


def _final_norm_kernel(x_ref, g_ref, o_ref):
    x = x_ref[...]
    ms = jnp.mean(x * x, axis=-1, keepdims=True)
    o_ref[...] = (x * lax.rsqrt(ms + EPS)) * g_ref[...]


def final_norm(x, g):
    n, d = x.shape
    tm = min(TM_PROJ, n)
    return pl.pallas_call(
        _final_norm_kernel,
        out_shape=jax.ShapeDtypeStruct((n, d), F32),
        grid=(n // tm,),
        in_specs=[pl.BlockSpec((tm, d), lambda i: (i, 0)), pl.BlockSpec((1, d), lambda i: (0, 0))],
        out_specs=pl.BlockSpec((tm, d), lambda i: (i, 0)),
        compiler_params=_cparams(("arbitrary",)),
        name="final_norm",
    )(x, g)


def _moe_plan(top_idx, bm):
    n = top_idx.shape[0]
    a = n * TOP_K
    e_flat = top_idx.reshape(a)
    onehot = (e_flat[:, None] == jnp.arange(N_EXPERTS, dtype=jnp.int32)[None, :]).astype(jnp.int32)
    incl = jnp.cumsum(onehot, axis=0)
    rank = jnp.sum((incl - onehot) * onehot, axis=1)
    counts = incl[-1]
    padded = (counts + bm - 1) // bm * bm
    pad_end = jnp.cumsum(padded)
    pad_start = pad_end - padded
    dest = (jnp.sum(pad_start[None, :] * onehot, axis=1) + rank).astype(jnp.int32)
    p_rows = a + N_EXPERTS * bm
    n_blocks = p_rows // bm
    n_used = (pad_end[-1] // bm).astype(jnp.int32)
    blk_start = jnp.arange(n_blocks, dtype=jnp.int32) * bm
    blk_expert = jnp.sum((blk_start[:, None] >= pad_end[None, :]).astype(jnp.int32), axis=1)
    blk_expert = jnp.minimum(blk_expert, N_EXPERTS - 1)
    last_e = blk_expert[jnp.maximum(n_used - 1, 0)]
    blk_expert = jnp.where(jnp.arange(n_blocks) < n_used, blk_expert, last_e).astype(jnp.int32)
    return dest, blk_expert, n_used.reshape(1), p_rows


def _head_block_mask(width):
    r = jnp.arange(width) // HEAD_DIM
    return (r[:, None] == r[None, :]).astype(BF16)


def kernel(x, attn_norm, w_in, hgrn_lb, hgrn_norm, mlstm_conv, mlstm_b_i, mlstm_b_f, mlstm_norm, fox_b_f, w_out, ffn_norm, dense_w_gate, dense_w_up, dense_w_down, router, moe_w_gate, moe_w_up, moe_w_down, final_norm_g):
    batch, seq, d = x.shape
    depth = w_in.shape[0]
    n = batch * seq
    xf = x.reshape(n, d)
    m_bf = _head_block_mask(HG_W)
    n_main_a = 4 * HG_W + 3 * ML_W
    gate_a = n_main_a
    mo_a = gate_a + 2 * ML_HEADS
    fx_a = mo_a + ML_W
    ff_a = fx_a + 3 * FX_W
    done = False
    for l in range(depth):
        wl = w_in[l]
        w_main = jnp.concatenate([wl[:, :n_main_a], wl[:, mo_a:fx_a], wl[:, fx_a:ff_a]], axis=1).astype(BF16)
        w_gate_t = jnp.concatenate([wl[:, gate_a:mo_a], wl[:, ff_a:]], axis=1).T.astype(BF16)
        main, g_row_raw = norm_inproj(xf, attn_norm[l].reshape(1, d), w_main, w_gate_t)
        bias = jnp.concatenate([mlstm_b_i[l], mlstm_b_f[l], fox_b_f[l]]).reshape(N_GATE_ROWS, 1).astype(F32)
        g_row, g_col = gates(g_row_raw, bias, batch)
        c_fox = g_row[2 * ML_HEADS:].reshape(FX_HEADS // 2, 2, n)
        o_hg = hgrn2(main, hgrn_lb.astype(F32), hgrn_norm[l].reshape(1, HG_W), m_bf, batch, l)
        o_ml = mlstm(main, g_row, g_col, mlstm_conv[l], mlstm_norm[l].reshape(1, ML_W), m_bf, batch)
        o_fx = fox(main, c_fox, batch)
        wo = w_out[l].astype(BF16)
        fg = ffn_norm[l].reshape(1, d)
        j = l // 2
        if l % 2 == 0:
            x_res, h = outproj(xf, o_hg, o_ml, o_fx, wo, fg)
            xf = dense_ffn(h, x_res, dense_w_gate[j].astype(BF16), dense_w_up[j].astype(BF16),
                           dense_w_down[j].astype(BF16))
        else:
            r = jnp.pad(router[j], ((0, 0), (0, LANES - N_EXPERTS)))
            r_hi = r.astype(BF16)
            r_lo = (r - r_hi.astype(F32)).astype(BF16)
            x_res, h, idx, gate = outproj(xf, o_hg, o_ml, o_fx, wo, fg, (r_hi, r_lo))
            dest, blk_expert, n_used, p_rows = _moe_plan(idx[:, :TOP_K], BM_MOE)
            xs = moe_dispatch(dest, h, p_rows)
            ys = moe_experts(blk_expert, n_used, xs, moe_w_gate[j].astype(BF16), moe_w_up[j].astype(BF16),
                             moe_w_down[j].astype(BF16))
            done = l == depth - 1
            xf = moe_combine(dest, x_res, gate, final_norm_g.reshape(1, d), ys, done)
    if not done:
        xf = final_norm(xf, final_norm_g.reshape(1, d))
    return xf.reshape(batch, seq, d)
```

```python
import functools

import jax
import jax.numpy as jnp
from jax import lax
from jax.experimental import pallas as pl
from jax.experimental.pallas import tpu as pltpu

F32 = jnp.float32
BF16 = jnp.bfloat16
EPS = 1e-6
NEG_INF = float("-inf")
LOG2E = 1.4426950408889634

HEAD_DIM = 64
HG_W = 256
ML_W = 256
FX_W = 512
ML_HEADS = 4
FX_HEADS = 8
CONV_K = 4
N_EXPERTS = 8
TOP_K = 2
N_GATE_ROWS = 16

LANES = 128
VMEM_LIMIT = 48 * 1024 * 1024
VMEM_LIMIT_BIG = 56 * 1024 * 1024

TM_PROJ = 512
T_GATE = 256
T_HG = 64
T_ML = 128
TQ_FX = 512
TK_FX = 512
TM_FFN = 512
BM_MOE = 512
TF_MOE = 1792
T_DISPATCH = 256


def _cparams(sem, vmem=VMEM_LIMIT):
    return pltpu.CompilerParams(dimension_semantics=sem, vmem_limit_bytes=vmem)


def _split3(x):
    hi = x.astype(BF16)
    r = x - hi.astype(F32)
    mid = r.astype(BF16)
    lo = (r - mid.astype(F32)).astype(BF16)
    return hi, mid, lo


def _dot(a, b):
    return jnp.dot(a, b, preferred_element_type=F32)


def _dot_nt(a, b):
    return lax.dot_general(a, b, (((1,), (1,)), ((), ())), preferred_element_type=F32)


def _dot_tn(a, b):
    return lax.dot_general(a, b, (((0,), (0,)), ((), ())), preferred_element_type=F32)


def _dot3(parts, b):
    return _dot(parts[0], b) + _dot(parts[1], b) + _dot(parts[2], b)


def _log_sigmoid(z):
    return -(jnp.maximum(-z, 0.0) + jnp.log1p(jnp.exp(-jnp.abs(z))))


def _sigmoid(z):
    return 1.0 / (1.0 + jnp.exp(-z))


def _head_mean_sq(o, m_bf):
    o2 = o * o
    hi = o2.astype(BF16)
    lo = (o2 - hi.astype(F32)).astype(BF16)
    return (_dot(hi, m_bf) + _dot(lo, m_bf)) * (1.0 / HEAD_DIM)


def _norm_inproj_kernel(x_ref, g_ref, w_ref, wgt_ref, main_ref, grow_ref, *, tn):
    x = x_ref[...]
    ms = jnp.mean(x * x, axis=-1, keepdims=True)
    h = ((x * lax.rsqrt(ms + EPS)) * g_ref[...]).astype(BF16)
    for j in range(w_ref.shape[1] // tn):
        main_ref[:, j * tn:(j + 1) * tn] = _dot(h, w_ref[:, j * tn:(j + 1) * tn]).astype(BF16)
    grow_ref[...] = _dot_nt(wgt_ref[...], h)


def norm_inproj(x, g, w_main, w_gate_t):
    n, d = x.shape
    wm = w_main.shape[1]
    tm = min(TM_PROJ, n)
    return pl.pallas_call(
        functools.partial(_norm_inproj_kernel, tn=512),
        out_shape=(jax.ShapeDtypeStruct((n, wm), BF16), jax.ShapeDtypeStruct((N_GATE_ROWS, n), F32)),
        grid=(n // tm,),
        in_specs=[pl.BlockSpec((tm, d), lambda i: (i, 0)),
                  pl.BlockSpec((1, d), lambda i: (0, 0)),
                  pl.BlockSpec((d, wm), lambda i: (0, 0)),
                  pl.BlockSpec((N_GATE_ROWS, d), lambda i: (0, 0))],
        out_specs=(pl.BlockSpec((tm, wm), lambda i: (i, 0)),
                   pl.BlockSpec((N_GATE_ROWS, tm), lambda i: (0, i))),
        compiler_params=_cparams(("arbitrary",)),
        name="norm_inproj",
    )(x, g, w_main, w_gate_t)


def _gates_kernel(g_ref, bias_ref, sel_ref, grow_ref, gcol_ref, caug_ref, carry_ref):
    t = g_ref.shape[1]

    @pl.when(pl.program_id(1) == 0)
    def _():
        carry_ref[...] = jnp.zeros_like(carry_ref)

    z = g_ref[...] + bias_ref[...]
    row = lax.broadcasted_iota(jnp.int32, z.shape, 0)
    is_input_gate = row < ML_HEADS
    val = jnp.where(is_input_gate, 0.0, _log_sigmoid(z))
    r_i = lax.broadcasted_iota(jnp.int32, (t, t), 0)
    c_i = lax.broadcasted_iota(jnp.int32, (t, t), 1)
    upper = jnp.where(r_i <= c_i, 1.0, 0.0).astype(BF16)
    tot = _dot3(_split3(val), upper) + carry_ref[:, 0:1]
    out = jnp.where(is_input_gate, z, tot)
    grow_ref[...] = out
    carry_ref[...] = jnp.broadcast_to(tot[:, t - 1:t], carry_ref.shape)
    eye = jnp.where(r_i == c_i, 1.0, 0.0).astype(BF16)
    p0, p1, p2 = _split3(out)
    gcol_ref[...] = _dot_nt(eye, p0) + _dot_nt(eye, p1) + _dot_nt(eye, p2)
    n0, n1, n2 = _split3(out * (-LOG2E))
    zrows = _dot(sel_ref[0], n0) + _dot(sel_ref[1], n1) + _dot(sel_ref[2], n2)
    caug_ref[...] = _dot_nt(eye, zrows.astype(BF16)).astype(BF16)


def _bias_lane_selectors():
    sel = [[[0.0] * N_GATE_ROWS for _ in range(FX_HEADS // 2 * LANES)] for _ in range(3)]
    for j in range(3):
        for p in range(FX_HEADS // 2):
            for a in range(2):
                sel[j][LANES * p + 3 * a + j][2 * ML_HEADS + 2 * p + a] = 1.0
    return jnp.asarray(sel, BF16)


def gates(g_row, bias, batch):
    r, n = g_row.shape
    s = n // batch
    t = min(T_GATE, s)
    nb = s // t
    sel = _bias_lane_selectors()
    wc = sel.shape[1]
    return pl.pallas_call(
        _gates_kernel,
        out_shape=(jax.ShapeDtypeStruct((r, n), F32), jax.ShapeDtypeStruct((n, r), F32),
                   jax.ShapeDtypeStruct((n, wc), BF16)),
        grid=(batch, nb),
        in_specs=[pl.BlockSpec((r, t), lambda b, j: (0, b * nb + j)),
                  pl.BlockSpec((r, 1), lambda b, j: (0, 0)),
                  pl.BlockSpec(sel.shape, lambda b, j: (0, 0, 0))],
        out_specs=(pl.BlockSpec((r, t), lambda b, j: (0, b * nb + j)),
                   pl.BlockSpec((t, r), lambda b, j: (b * nb + j, 0)),
                   pl.BlockSpec((t, wc), lambda b, j: (b * nb + j, 0))),
        scratch_shapes=[pltpu.VMEM((r, LANES), F32)],
        compiler_params=_cparams(("arbitrary", "arbitrary")),
        name="gates",
    )(g_row, bias, sel)


def _hgrn_kernel(q_ref, f_ref, i_ref, g_ref, lb_ref, gain_ref, m_ref, o_ref, st_ref, x_ref, y_ref, *, layer):
    t = q_ref.shape[0]
    grp = 16
    ngrp = t // grp

    @pl.when(pl.program_id(1) == 0)
    def _():
        st_ref[...] = jnp.zeros_like(st_ref)

    lbp = lb_ref[...]
    rows = [lbp[r:r + 1, :] for r in range(lbp.shape[0])]
    mx = functools.reduce(jnp.maximum, rows)
    es = [jnp.exp(r - mx) for r in rows]
    tot = functools.reduce(lambda a, b: a + b, es)
    cs, run = [], None
    for e in es:
        run = e / tot if run is None else run + e / tot
        cs.append(run)
    lb = cs[layer] - cs[0]

    z = f_ref[...].astype(F32)
    a = jnp.log(lb)
    bb = jnp.log1p(-lb) + _log_sigmoid(z)
    log_f = jnp.maximum(a, bb) + jnp.log1p(jnp.exp(-jnp.abs(a - bb)))
    k = (1.0 - lb) * _sigmoid(-z)
    q = q_ref[...].astype(F32)
    v = i_ref[...].astype(F32)
    m_bf = m_ref[...]

    r_i = lax.broadcasted_iota(jnp.int32, (t, t), 0)
    c_i = lax.broadcasted_iota(jnp.int32, (t, t), 1)
    lower = jnp.where(c_i <= r_i, 1.0, 0.0).astype(BF16)
    f0, f1, f2 = _split3(log_f)
    b = _dot(lower, f0) + _dot(lower, f1) + _dot(lower, f2)

    st = st_ref[...]
    o_inter = _dot_nt((q * jnp.exp(b)).astype(BF16), st.astype(BF16))

    qg = [q[g * grp:(g + 1) * grp, :] for g in range(ngrp)]
    b2 = b * LOG2E
    bg = [b2[g * grp:(g + 1) * grp, :] for g in range(ngrp)]
    t_in_grp = lax.broadcasted_iota(jnp.int32, (grp, q.shape[1]), 0)
    off = 0
    for s in range(t):
        gs = s // grp
        bs = b2[s:s + 1, :]
        ks = k[s:s + 1, :]
        for g in range(gs, ngrp):
            diff = bg[g] - bs
            if g == gs:
                diff = jnp.where(t_in_grp >= (s - gs * grp), diff, NEG_INF)
            x_ref[off:off + grp, :] = (qg[g] * (ks * jnp.exp2(diff))).astype(BF16)
            off += grp
    y_ref[...] = _dot(x_ref[...], m_bf)
    og = [o_inter[g * grp:(g + 1) * grp, :] for g in range(ngrp)]
    off = 0
    for s in range(t):
        gs = s // grp
        vs = v[s:s + 1, :]
        for g in range(gs, ngrp):
            og[g] = og[g] + y_ref[off:off + grp, :] * vs
            off += grp
    o = jnp.concatenate(og, axis=0)

    b_end = b[t - 1:t, :]
    kd = k * jnp.exp(b_end - b)
    upd = _dot_tn(v.astype(BF16), kd.astype(BF16))
    st_ref[...] = st * jnp.exp(b_end) + upd * m_bf.astype(F32)

    gt = g_ref[...].astype(F32)
    y = o * lax.rsqrt(_head_mean_sq(o, m_bf) + EPS) * gain_ref[...] * (gt * _sigmoid(gt))
    o_ref[...] = y.astype(o_ref.dtype)


def hgrn2(main, lb_all, gain, m_bf, batch, layer):
    n = main.shape[0]
    s = n // batch
    t = T_HG
    nc = s // t
    w = HG_W
    n_pairs = sum(t // 16 - si // 16 for si in range(t)) * 16
    col = lambda cidx: pl.BlockSpec((t, w), lambda b, c: (b * nc + c, cidx))
    full = lambda shape: pl.BlockSpec(shape, lambda b, c: (0, 0))
    return pl.pallas_call(
        functools.partial(_hgrn_kernel, layer=layer),
        out_shape=jax.ShapeDtypeStruct((n, w), BF16),
        grid=(batch, nc),
        in_specs=[col(0), col(1), col(2), col(3), full(lb_all.shape), full((1, w)), full((w, w))],
        out_specs=pl.BlockSpec((t, w), lambda b, c: (b * nc + c, 0)),
        scratch_shapes=[pltpu.VMEM((w, w), F32), pltpu.VMEM((n_pairs, w), BF16), pltpu.VMEM((n_pairs, w), F32)],
        compiler_params=_cparams(("arbitrary", "arbitrary")),
        name="hgrn2",
    )(main, main, main, main, lb_all, gain, m_bf)


def _mlstm_kernel(q_ref, k_ref, v_ref, og_ref, grow_ref, gcol_ref, cw_ref, gain_ref, m_ref, o_ref,
                  ext_ref, ct_ref, n_ref, mm_ref):
    t = q_ref.shape[0]
    w = q_ref.shape[1]
    halo = 8

    @pl.when(pl.program_id(1) == 0)
    def _():
        ext_ref[0:halo, :] = jnp.zeros((halo, 2 * w), F32)
        ct_ref[...] = jnp.zeros_like(ct_ref)
        n_ref[...] = jnp.zeros_like(n_ref)
        mm_ref[...] = jnp.zeros_like(mm_ref)

    ext_ref[halo:halo + t, 0:w] = q_ref[...].astype(F32)
    ext_ref[halo:halo + t, w:2 * w] = k_ref[...].astype(F32)
    cw = cw_ref[...]
    y = None
    for j in range(CONV_K):
        term = ext_ref[halo - (CONV_K - 1) + j:halo - (CONV_K - 1) + j + t, :] * cw[j:j + 1, :]
        y = term if y is None else y + term
    tail = ext_ref[t:t + halo, :]
    ext_ref[0:halo, :] = tail
    qk = y * _sigmoid(y)
    q = qk[:, 0:w]
    k = qk[:, w:2 * w] * (HEAD_DIM ** -0.5)
    kb = k.astype(BF16)
    vb = v_ref[...]
    m_bf = m_ref[...]

    grow = grow_ref[...]
    gcol = gcol_ref[...]
    lane_head = lax.broadcasted_iota(jnp.int32, (1, w), 1) // HEAD_DIM
    r_i = lax.broadcasted_iota(jnp.int32, (t, t), 0)
    c_i = lax.broadcasted_iota(jnp.int32, (t, t), 1)
    causal = c_i <= r_i

    num_intra = jnp.zeros((t, w), F32)
    sint_l = jnp.zeros((t, w), F32)
    wsum_l = jnp.zeros((t, w), F32)
    mt_l = jnp.zeros((t, w), F32)
    wk_l = jnp.zeros((t, w), F32)
    decay_l = jnp.zeros((1, w), F32)
    for h in range(ML_HEADS):
        sel = lane_head == h
        qh = jnp.where(sel, q, 0.0).astype(BF16)
        s = _dot_nt(qh, kb)
        bc = gcol[:, ML_HEADS + h:ML_HEADS + h + 1]
        br = grow[ML_HEADS + h:ML_HEADS + h + 1, :]
        lir = grow[h:h + 1, :]
        lic = gcol[:, h:h + 1]
        dlog = jnp.where(causal, bc - br + lir, NEG_INF)
        mmh = mm_ref[h:h + 1, 0:1]
        inter = bc + mmh
        m_t = jnp.maximum(jnp.max(dlog, axis=1, keepdims=True), inter)
        wgt = s * jnp.exp(dlog - m_t)
        s_int = jnp.exp(inter - m_t)
        pv = _dot(wgt.astype(BF16), vb)
        num_intra = jnp.where(sel, pv, num_intra)
        sint_l = jnp.where(sel, s_int, sint_l)
        wsum_l = jnp.where(sel, jnp.sum(wgt, axis=1, keepdims=True), wsum_l)
        mt_l = jnp.where(sel, m_t, mt_l)
        b_end = br[:, t - 1:t]
        m_new = jnp.maximum(b_end + mmh, jnp.max(b_end - br + lir, axis=1, keepdims=True))
        wk_l = jnp.where(sel, jnp.exp(b_end - bc + lic - m_new), wk_l)
        decay_l = jnp.where(sel, jnp.exp(b_end + mmh - m_new), decay_l)
        mm_ref[h:h + 1, :] = jnp.broadcast_to(m_new - b_end, (1, mm_ref.shape[1]))

    ct = ct_ref[...]
    nrow = n_ref[0:1, :]
    q_c = _dot_nt(q.astype(BF16), ct.astype(BF16))
    qn = q * nrow
    qn_hi = qn.astype(BF16)
    qn_lo = (qn - qn_hi.astype(F32)).astype(BF16)
    qn_l = _dot(qn_hi, m_bf) + _dot(qn_lo, m_bf)
    num = num_intra + sint_l * q_c
    den = wsum_l + sint_l * qn_l
    hval = num / jnp.maximum(jnp.abs(den), jnp.exp(-mt_l))

    kw = k * wk_l
    upd = _dot_tn(vb, kw.astype(BF16))
    ct_ref[...] = decay_l * ct + upd * m_bf.astype(F32)
    n_ref[...] = jnp.broadcast_to(decay_l * nrow + jnp.sum(kw, axis=0, keepdims=True), n_ref.shape)

    og = og_ref[...].astype(F32)
    yv = hval * lax.rsqrt(_head_mean_sq(hval, m_bf) + EPS) * gain_ref[...] * _sigmoid(og)
    o_ref[...] = yv.astype(o_ref.dtype)


def mlstm(main, g_row, g_col, conv_w, gain, m_bf, batch):
    n = main.shape[0]
    s = n // batch
    t = min(T_ML, s)
    nc = s // t
    w = ML_W
    col = lambda cidx: pl.BlockSpec((t, w), lambda b, c: (b * nc + c, cidx))
    full = lambda shape: pl.BlockSpec(shape, lambda b, c: (0, 0))
    return pl.pallas_call(
        _mlstm_kernel,
        out_shape=jax.ShapeDtypeStruct((n, w), BF16),
        grid=(batch, nc),
        in_specs=[col(4), col(5), col(6), col(7),
                  pl.BlockSpec((N_GATE_ROWS, t), lambda b, c: (0, b * nc + c)),
                  pl.BlockSpec((t, N_GATE_ROWS), lambda b, c: (b * nc + c, 0)),
                  full((CONV_K, 2 * w)), full((1, w)), full((w, w))],
        out_specs=pl.BlockSpec((t, w), lambda b, c: (b * nc + c, 0)),
        scratch_shapes=[pltpu.VMEM((t + 8, 2 * w), F32), pltpu.VMEM((w, w), F32),
                        pltpu.VMEM((8, w), F32), pltpu.VMEM((8, LANES), F32)],
        compiler_params=_cparams(("arbitrary", "arbitrary")),
        name="mlstm",
    )(main, main, main, main, g_row, g_col, conv_w, gain, m_bf)


def _fox_kernel(qi_ref, kj_ref, q_ref, k_ref, v_ref, c_ref, o_ref, m_ref, acc_ref):
    tq = q_ref.shape[0]
    tk = k_ref.shape[0]
    step_id = pl.program_id(2)
    qi = qi_ref[step_id]
    kj = kj_ref[step_id]
    lane = lax.broadcasted_iota(jnp.int32, (1, LANES), 1)
    first = lane < HEAD_DIM

    @pl.when(kj == 0)
    def _():
        m_ref[...] = jnp.full(m_ref.shape, NEG_INF, F32)
        acc_ref[...] = jnp.zeros_like(acc_ref)

    def step(diag):
        q2 = q_ref[...]
        v2 = v_ref[...]
        k_aug = jnp.concatenate([k_ref[...], c_ref[...]], axis=1)
        for a in range(2):
            sel = first if a == 0 else jnp.logical_not(first)
            ones_lanes = jnp.logical_and(lane >= 3 * a, lane < 3 * a + 3)
            q_bias = jnp.broadcast_to(jnp.where(ones_lanes, 1.0, 0.0).astype(q2.dtype), q2.shape)
            q_aug = jnp.concatenate([jnp.where(sel, q2, jnp.zeros_like(q2)), q_bias], axis=1)
            s = _dot_nt(q_aug, k_aug)
            if diag:
                r_i = lax.broadcasted_iota(jnp.int32, (tq, tk), 0)
                c_i = lax.broadcasted_iota(jnp.int32, (tq, tk), 1)
                s = jnp.where(c_i <= r_i, s, NEG_INF)
            m_prev = m_ref[a]
            m_new = jnp.maximum(m_prev, jnp.max(s, axis=1, keepdims=True))
            p = jnp.concatenate([jnp.exp2(s[:, c * LANES:(c + 1) * LANES] - m_new).astype(v2.dtype)
                                 for c in range(tk // LANES)], axis=1)
            v_aug = jnp.where(sel, v2, jnp.ones_like(v2))
            acc_ref[a] = jnp.exp2(m_prev - m_new) * acc_ref[a] + _dot(p, v_aug)
            m_ref[a] = m_new

    @pl.when(kj < qi)
    def _():
        step(False)

    @pl.when(kj == qi)
    def _():
        step(True)
        acc_a = acc_ref[0]
        acc_b = acc_ref[1]
        half = HEAD_DIM
        out = jnp.where(first, acc_a / pltpu.roll(acc_a, half, 1), acc_b / pltpu.roll(acc_b, half, 1))
        o_ref[...] = out.astype(o_ref.dtype)


def fox(main, c_aug, batch):
    n = main.shape[0]
    s = n // batch
    tq = min(TQ_FX, s)
    tk = tq
    nq = s // tq
    pairs = FX_HEADS // 2
    qcol, kcol, vcol = 2048 // LANES, 2560 // LANES, 3072 // LANES
    tri = [(i, j) for i in range(nq) for j in range(i + 1)]
    qi_tab = jnp.asarray([i for i, _ in tri], jnp.int32)
    kj_tab = jnp.asarray([j for _, j in tri], jnp.int32)
    return pl.pallas_call(
        _fox_kernel,
        out_shape=jax.ShapeDtypeStruct((n, FX_W), BF16),
        grid_spec=pltpu.PrefetchScalarGridSpec(
            num_scalar_prefetch=2, grid=(batch, pairs, len(tri)),
            in_specs=[pl.BlockSpec((tq, LANES), lambda b, p, t, qt, kt: (b * nq + qt[t], qcol + p)),
                      pl.BlockSpec((tk, LANES), lambda b, p, t, qt, kt: (b * nq + kt[t], kcol + p)),
                      pl.BlockSpec((tk, LANES), lambda b, p, t, qt, kt: (b * nq + kt[t], vcol + p)),
                      pl.BlockSpec((tk, LANES), lambda b, p, t, qt, kt: (b * nq + kt[t], p))],
            out_specs=pl.BlockSpec((tq, LANES), lambda b, p, t, qt, kt: (b * nq + qt[t], p)),
            scratch_shapes=[pltpu.VMEM((2, tq, LANES), F32), pltpu.VMEM((2, tq, LANES), F32)]),
        compiler_params=_cparams(("arbitrary", "arbitrary", "arbitrary")),
        name="fox",
    )(qi_tab, kj_tab, main, main, main, c_aug)


def _outproj_body(x_ref, ohg_ref, oml_ref, ofx_ref, w_ref, g_ref):
    acc = x_ref[...]
    acc = acc + _dot(ohg_ref[...], w_ref[0:HG_W, :])
    acc = acc + _dot(oml_ref[...], w_ref[HG_W:HG_W + ML_W, :])
    acc = acc + _dot(ofx_ref[...], w_ref[HG_W + ML_W:, :])
    ms = jnp.mean(acc * acc, axis=-1, keepdims=True)
    h = (acc * lax.rsqrt(ms + EPS)) * g_ref[...]
    return acc, h


def _outproj_dense_kernel(x_ref, ohg_ref, oml_ref, ofx_ref, w_ref, g_ref, xo_ref, h_ref):
    acc, h = _outproj_body(x_ref, ohg_ref, oml_ref, ofx_ref, w_ref, g_ref)
    xo_ref[...] = acc
    h_ref[...] = h.astype(h_ref.dtype)


def _outproj_moe_kernel(x_ref, ohg_ref, oml_ref, ofx_ref, w_ref, g_ref, rhi_ref, rlo_ref,
                        xo_ref, h_ref, idx_ref, gate_ref):
    acc, h = _outproj_body(x_ref, ohg_ref, oml_ref, ofx_ref, w_ref, g_ref)
    xo_ref[...] = acc
    h_ref[...] = h
    h_hi = h.astype(BF16)
    h_lo = (h - h_hi.astype(F32)).astype(BF16)
    logits = _dot(h_hi, rhi_ref[...]) + _dot(h_hi, rlo_ref[...]) + _dot(h_lo, rhi_ref[...])
    lane_i = lax.broadcasted_iota(jnp.int32, logits.shape, 1)
    lane = lane_i.astype(F32)
    lg = jnp.where(lane_i < N_EXPERTS, logits, NEG_INF)
    m1 = jnp.max(lg, axis=1, keepdims=True)
    i1 = jnp.min(jnp.where(lg == m1, lane, float(LANES)), axis=1, keepdims=True)
    lg2 = jnp.where(lane == i1, NEG_INF, lg)
    m2 = jnp.max(lg2, axis=1, keepdims=True)
    i2 = jnp.min(jnp.where(lg2 == m2, lane, float(LANES)), axis=1, keepdims=True)
    e = jnp.exp(m2 - m1)
    g1 = 1.0 / (1.0 + e)
    g2 = e / (1.0 + e)
    idx_ref[...] = jnp.where(lane_i == 0, i1, jnp.where(lane_i == 1, i2, 0.0)).astype(jnp.int32)
    gate_ref[...] = jnp.where(lane_i == 0, g1, jnp.where(lane_i == 1, g2, 0.0))


def outproj(x, o_hg, o_ml, o_fx, w_out, g, router_parts=None):
    n, d = x.shape
    tm = min(TM_PROJ, n)
    row = lambda width: pl.BlockSpec((tm, width), lambda i: (i, 0))
    full = lambda shape: pl.BlockSpec(shape, lambda i: (0, 0))
    in_specs = [row(d), row(HG_W), row(ML_W), row(FX_W), full(w_out.shape), full((1, d))]
    args = [x, o_hg, o_ml, o_fx, w_out, g]
    if router_parts is None:
        kern = _outproj_dense_kernel
        out_shape = (jax.ShapeDtypeStruct((n, d), F32), jax.ShapeDtypeStruct((n, d), BF16))
        out_specs = (row(d), row(d))
    else:
        kern = _outproj_moe_kernel
        in_specs += [full(router_parts[0].shape), full(router_parts[1].shape)]
        args += list(router_parts)
        out_shape = (jax.ShapeDtypeStruct((n, d), F32), jax.ShapeDtypeStruct((n, d), F32),
                     jax.ShapeDtypeStruct((n, LANES), jnp.int32), jax.ShapeDtypeStruct((n, LANES), F32))
        out_specs = (row(d), row(d), row(LANES), row(LANES))
    return pl.pallas_call(
        kern, out_shape=out_shape, grid=(n // tm,), in_specs=in_specs, out_specs=out_specs,
        compiler_params=_cparams(("arbitrary",)), name="outproj",
    )(*args)


def _ffn_kernel(h_ref, x_ref, wg_ref, wu_ref, wd_ref, o_ref):
    @pl.when(pl.program_id(1) == 0)
    def _():
        o_ref[...] = x_ref[...]

    h = h_ref[...]
    gt = _dot(h, wg_ref[...])
    up = _dot(h, wu_ref[...])
    hid = (gt * _sigmoid(gt) * up).astype(BF16)
    o_ref[...] += _dot(hid, wd_ref[...])


def dense_ffn(h, x, wg, wu, wd):
    n, d = x.shape
    ff = wg.shape[1]
    tm = min(TM_FFN, n)
    tf = ff // 2 if (ff // 2) % LANES == 0 else ff
    return pl.pallas_call(
        _ffn_kernel,
        out_shape=jax.ShapeDtypeStruct((n, d), F32),
        grid=(n // tm, ff // tf),
        in_specs=[pl.BlockSpec((tm, d), lambda i, f: (i, 0)),
                  pl.BlockSpec((tm, d), lambda i, f: (i, 0)),
                  pl.BlockSpec((d, tf), lambda i, f: (0, f)),
                  pl.BlockSpec((d, tf), lambda i, f: (0, f)),
                  pl.BlockSpec((tf, d), lambda i, f: (f, 0))],
        out_specs=pl.BlockSpec((tm, d), lambda i, f: (i, 0)),
        compiler_params=_cparams(("arbitrary", "arbitrary")),
        name="dense_ffn",
    )(h, x, wg, wu, wd)


def _row_copy(src_ref, src_row, dst_ref, dst_row, sem):
    return pltpu.make_async_copy(src_ref.at[pl.ds(src_row, 1)], dst_ref.at[pl.ds(dst_row, 1)], sem)


def _dispatch_kernel(dest_ref, h_ref, xs_in_ref, xs_ref, sem, *, tb):
    del xs_in_ref
    base = pl.program_id(0) * tb

    def copy(i, kk):
        return _row_copy(h_ref, i, xs_ref, dest_ref[(base + i) * TOP_K + kk], sem)

    def issue(i, carry):
        for kk in range(TOP_K):
            copy(i, kk).start()
        return carry

    lax.fori_loop(0, tb, issue, 0, unroll=8)

    def drain(i, carry):
        for kk in range(TOP_K):
            copy(i, kk).wait()
        return carry

    lax.fori_loop(0, tb, drain, 0, unroll=8)


def moe_dispatch(dest, h, p_rows):
    n, d = h.shape
    tb = min(T_DISPATCH, n)
    xs0 = jnp.zeros((p_rows, d), h.dtype)
    return pl.pallas_call(
        functools.partial(_dispatch_kernel, tb=tb),
        out_shape=jax.ShapeDtypeStruct((p_rows, d), h.dtype),
        grid_spec=pltpu.PrefetchScalarGridSpec(
            num_scalar_prefetch=1, grid=(n // tb,),
            in_specs=[pl.BlockSpec((tb, d), lambda i, dr: (i, 0)), pl.BlockSpec(memory_space=pl.ANY)],
            out_specs=pl.BlockSpec(memory_space=pl.ANY),
            scratch_shapes=[pltpu.SemaphoreType.DMA(())]),
        input_output_aliases={2: 0},
        compiler_params=_cparams(("arbitrary",)),
        name="moe_dispatch",
    )(dest, h, xs0)


def _experts_kernel(blk_e_ref, nused_ref, x_ref, wg_ref, wu_ref, wd_ref, o_ref):
    m = pl.program_id(0)
    f = pl.program_id(1)

    @pl.when(f == 0)
    def _():
        o_ref[...] = jnp.zeros_like(o_ref)

    @pl.when(m < nused_ref[0])
    def _():
        xb = x_ref[...].astype(BF16)
        gt = _dot(xb, wg_ref[0])
        up = _dot(xb, wu_ref[0])
        hid = (gt * _sigmoid(gt) * up).astype(BF16)
        o_ref[...] += _dot(hid, wd_ref[0])


def moe_experts(blk_expert, n_used, xs, wg, wu, wd):
    p_rows, d = xs.shape
    ff = wg.shape[2]
    bm = BM_MOE
    tf = TF_MOE
    nf = ff // tf

    def w_idx(m, f, be, nu):
        live = m < nu[0]
        return be[m], jnp.where(live, f, nf - 1)

    return pl.pallas_call(
        _experts_kernel,
        out_shape=jax.ShapeDtypeStruct((p_rows, d), F32),
        grid_spec=pltpu.PrefetchScalarGridSpec(
            num_scalar_prefetch=2, grid=(p_rows // bm, nf),
            in_specs=[pl.BlockSpec((bm, d), lambda m, f, be, nu: (m, 0)),
                      pl.BlockSpec((1, d, tf), lambda m, f, be, nu: (w_idx(m, f, be, nu)[0], 0, w_idx(m, f, be, nu)[1])),
                      pl.BlockSpec((1, d, tf), lambda m, f, be, nu: (w_idx(m, f, be, nu)[0], 0, w_idx(m, f, be, nu)[1])),
                      pl.BlockSpec((1, tf, d), lambda m, f, be, nu: (w_idx(m, f, be, nu)[0], w_idx(m, f, be, nu)[1], 0))],
            out_specs=pl.BlockSpec((bm, d), lambda m, f, be, nu: (m, 0))),
        compiler_params=_cparams(("arbitrary", "arbitrary"), VMEM_LIMIT_BIG),
        name="moe_experts",
    )(blk_expert, n_used, xs, wg, wu, wd)


def _combine_kernel(dest_ref, x_ref, gate_ref, fg_ref, ys_ref, o_ref, buf_ref, sem, *, tb, final):
    base = pl.program_id(0) * tb

    def copy(i, kk):
        return _row_copy(ys_ref, dest_ref[(base + i) * TOP_K + kk], buf_ref.at[kk], i, sem)

    def issue(i, carry):
        for kk in range(TOP_K):
            copy(i, kk).start()
        return carry

    lax.fori_loop(0, tb, issue, 0, unroll=8)

    def drain(i, carry):
        for kk in range(TOP_K):
            copy(i, kk).wait()
        return carry

    lax.fori_loop(0, tb, drain, 0, unroll=8)
    gate = gate_ref[...]
    y = x_ref[...] + gate[:, 0:1] * buf_ref[0] + gate[:, 1:2] * buf_ref[1]
    if final:
        ms = jnp.mean(y * y, axis=-1, keepdims=True)
        y = (y * lax.rsqrt(ms + EPS)) * fg_ref[...]
    o_ref[...] = y


def moe_combine(dest, x, gate, final_g, ys, final):
    n, d = x.shape
    tb = min(T_DISPATCH, n)
    return pl.pallas_call(
        functools.partial(_combine_kernel, tb=tb, final=final),
        out_shape=jax.ShapeDtypeStruct((n, d), F32),
        grid_spec=pltpu.PrefetchScalarGridSpec(
            num_scalar_prefetch=1, grid=(n // tb,),
            in_specs=[pl.BlockSpec((tb, d), lambda i, dr: (i, 0)),
                      pl.BlockSpec((tb, LANES), lambda i, dr: (i, 0)),
                      pl.BlockSpec((1, d), lambda i, dr: (0, 0)),
                      pl.BlockSpec(memory_space=pl.ANY)],
            out_specs=pl.BlockSpec((tb, d), lambda i, dr: (i, 0)),
            scratch_shapes=[pltpu.VMEM((TOP_K, tb, d), F32), pltpu.SemaphoreType.DMA(())]),
        compiler_params=_cparams(("arbitrary",)),
        name="moe_combine",
    )(dest, x, gate, final_g, ys)


def _final_norm_kernel(x_ref, g_ref, o_ref):
    x = x_ref[...]
    ms = jnp.mean(x * x, axis=-1, keepdims=True)
    o_ref[...] = (x * lax.rsqrt(ms + EPS)) * g_ref[...]


def final_norm(x, g):
    n, d = x.shape
    tm = min(TM_PROJ, n)
    return pl.pallas_call(
        _final_norm_kernel,
        out_shape=jax.ShapeDtypeStruct((n, d), F32),
        grid=(n // tm,),
        in_specs=[pl.BlockSpec((tm, d), lambda i: (i, 0)), pl.BlockSpec((1, d), lambda i: (0, 0))],
        out_specs=pl.BlockSpec((tm, d), lambda i: (i, 0)),
        compiler_params=_cparams(("arbitrary",)),
        name="final_norm",
    )(x, g)


def _moe_plan(top_idx, bm):
    n = top_idx.shape[0]
    a = n * TOP_K
    e_flat = top_idx.reshape(a)
    onehot = (e_flat[:, None] == jnp.arange(N_EXPERTS, dtype=jnp.int32)[None, :]).astype(jnp.int32)
    incl = jnp.cumsum(onehot, axis=0)
    rank = jnp.sum((incl - onehot) * onehot, axis=1)
    counts = incl[-1]
    padded = (counts + bm - 1) // bm * bm
    pad_end = jnp.cumsum(padded)
    pad_start = pad_end - padded
    dest = (jnp.sum(pad_start[None, :] * onehot, axis=1) + rank).astype(jnp.int32)
    p_rows = a + N_EXPERTS * bm
    n_blocks = p_rows // bm
    n_used = (pad_end[-1] // bm).astype(jnp.int32)
    blk_start = jnp.arange(n_blocks, dtype=jnp.int32) * bm
    blk_expert = jnp.sum((blk_start[:, None] >= pad_end[None, :]).astype(jnp.int32), axis=1)
    blk_expert = jnp.minimum(blk_expert, N_EXPERTS - 1)
    last_e = blk_expert[jnp.maximum(n_used - 1, 0)]
    blk_expert = jnp.where(jnp.arange(n_blocks) < n_used, blk_expert, last_e).astype(jnp.int32)
    return dest, blk_expert, n_used.reshape(1), p_rows


def _head_block_mask(width):
    r = jnp.arange(width) // HEAD_DIM
    return (r[:, None] == r[None, :]).astype(BF16)


def kernel(x, attn_norm, w_in, hgrn_lb, hgrn_norm, mlstm_conv, mlstm_b_i, mlstm_b_f, mlstm_norm, fox_b_f, w_out, ffn_norm, dense_w_gate, dense_w_up, dense_w_down, router, moe_w_gate, moe_w_up, moe_w_down, final_norm_g):
    batch, seq, d = x.shape
    depth = w_in.shape[0]
    n = batch * seq
    xf = x.reshape(n, d)
    m_bf = _head_block_mask(HG_W)
    n_main_a = 4 * HG_W + 3 * ML_W
    gate_a = n_main_a
    mo_a = gate_a + 2 * ML_HEADS
    fx_a = mo_a + ML_W
    ff_a = fx_a + 3 * FX_W
    done = False
    for l in range(depth):
        wl = w_in[l]
        wq_fx = wl[:, fx_a:fx_a + FX_W] * (LOG2E * HEAD_DIM ** -0.5)
        w_main = jnp.concatenate([wl[:, :n_main_a], wl[:, mo_a:fx_a], wq_fx, wl[:, fx_a + FX_W:ff_a]],
                                 axis=1).astype(BF16)
        w_gate_t = jnp.concatenate([wl[:, gate_a:mo_a], wl[:, ff_a:]], axis=1).T.astype(BF16)
        main, g_row_raw = norm_inproj(xf, attn_norm[l].reshape(1, d), w_main, w_gate_t)
        bias = jnp.concatenate([mlstm_b_i[l], mlstm_b_f[l], fox_b_f[l]]).reshape(N_GATE_ROWS, 1).astype(F32)
        g_row, g_col, c_aug = gates(g_row_raw, bias, batch)
        o_hg = hgrn2(main, hgrn_lb.astype(F32), hgrn_norm[l].reshape(1, HG_W), m_bf, batch, l)
        o_ml = mlstm(main, g_row, g_col, mlstm_conv[l], mlstm_norm[l].reshape(1, ML_W), m_bf, batch)
        o_fx = fox(main, c_aug, batch)
        wo = w_out[l].astype(BF16)
        fg = ffn_norm[l].reshape(1, d)
        j = l // 2
        if l % 2 == 0:
            x_res, h = outproj(xf, o_hg, o_ml, o_fx, wo, fg)
            xf = dense_ffn(h, x_res, dense_w_gate[j].astype(BF16), dense_w_up[j].astype(BF16),
                           dense_w_down[j].astype(BF16))
        else:
            r = jnp.pad(router[j], ((0, 0), (0, LANES - N_EXPERTS)))
            r_hi = r.astype(BF16)
            r_lo = (r - r_hi.astype(F32)).astype(BF16)
            x_res, h, idx, gate = outproj(xf, o_hg, o_ml, o_fx, wo, fg, (r_hi, r_lo))
            dest, blk_expert, n_used, p_rows = _moe_plan(idx[:, :TOP_K], BM_MOE)
            xs = moe_dispatch(dest, h, p_rows)
            ys = moe_experts(blk_expert, n_used, xs, moe_w_gate[j].astype(BF16), moe_w_up[j].astype(BF16),
                             moe_w_down[j].astype(BF16))
            done = l == depth - 1
            xf = moe_combine(dest, x_res, gate, final_norm_g.reshape(1, d), ys, done)
    if not done:
        xf = final_norm(xf, final_norm_g.reshape(1, d))
    return xf.reshape(batch, seq, d)
```

```python
import functools

import jax
import jax.numpy as jnp
from jax import lax
from jax.experimental import pallas as pl
from jax.experimental.pallas import tpu as pltpu

F32 = jnp.float32
BF16 = jnp.bfloat16
EPS = 1e-6
NEG_INF = float("-inf")
LOG2E = 1.4426950408889634

HEAD_DIM = 64
HG_W = 256
ML_W = 256
FX_W = 512
ML_HEADS = 4
FX_HEADS = 8
CONV_K = 4
N_EXPERTS = 8
TOP_K = 2
N_GATE_ROWS = 16

LANES = 128
VMEM_LIMIT = 48 * 1024 * 1024
VMEM_LIMIT_BIG = 56 * 1024 * 1024

TM_PROJ = 512
T_GATE = 256
T_HG = 64
T_ML = 128
TQ_FX = 1024
TM_FFN = 512
BM_MOE = 512
TF_MOE = 1792
T_DISPATCH = 256


def _cparams(sem, vmem=VMEM_LIMIT):
    return pltpu.CompilerParams(dimension_semantics=sem, vmem_limit_bytes=vmem)


def _split3(x):
    hi = x.astype(BF16)
    r = x - hi.astype(F32)
    mid = r.astype(BF16)
    lo = (r - mid.astype(F32)).astype(BF16)
    return hi, mid, lo


def _dot(a, b):
    return jnp.dot(a, b, preferred_element_type=F32)


def _dot_nt(a, b):
    return lax.dot_general(a, b, (((1,), (1,)), ((), ())), preferred_element_type=F32)


def _dot_tn(a, b):
    return lax.dot_general(a, b, (((0,), (0,)), ((), ())), preferred_element_type=F32)


def _dot3(parts, b):
    return _dot(parts[0], b) + _dot(parts[1], b) + _dot(parts[2], b)


def _log_sigmoid(z):
    return -(jnp.maximum(-z, 0.0) + jnp.log1p(jnp.exp(-jnp.abs(z))))


def _sigmoid(z):
    return 1.0 / (1.0 + jnp.exp(-z))


def _head_mean_sq(o, m_bf):
    o2 = o * o
    hi = o2.astype(BF16)
    lo = (o2 - hi.astype(F32)).astype(BF16)
    return (_dot(hi, m_bf) + _dot(lo, m_bf)) * (1.0 / HEAD_DIM)


def _norm_inproj_kernel(x_ref, g_ref, w_ref, wgt_ref, main_ref, grow_ref, *, tn):
    x = x_ref[...]
    ms = jnp.mean(x * x, axis=-1, keepdims=True)
    h = ((x * lax.rsqrt(ms + EPS)) * g_ref[...]).astype(BF16)
    for j in range(w_ref.shape[1] // tn):
        main_ref[:, j * tn:(j + 1) * tn] = _dot(h, w_ref[:, j * tn:(j + 1) * tn]).astype(BF16)
    grow_ref[...] = _dot_nt(wgt_ref[...], h)


def norm_inproj(x, g, w_main, w_gate_t):
    n, d = x.shape
    wm = w_main.shape[1]
    tm = min(TM_PROJ, n)
    return pl.pallas_call(
        functools.partial(_norm_inproj_kernel, tn=512),
        out_shape=(jax.ShapeDtypeStruct((n, wm), BF16), jax.ShapeDtypeStruct((N_GATE_ROWS, n), F32)),
        grid=(n // tm,),
        in_specs=[pl.BlockSpec((tm, d), lambda i: (i, 0)),
                  pl.BlockSpec((1, d), lambda i: (0, 0)),
                  pl.BlockSpec((d, wm), lambda i: (0, 0)),
                  pl.BlockSpec((N_GATE_ROWS, d), lambda i: (0, 0))],
        out_specs=(pl.BlockSpec((tm, wm), lambda i: (i, 0)),
                   pl.BlockSpec((N_GATE_ROWS, tm), lambda i: (0, i))),
        compiler_params=_cparams(("arbitrary",)),
        name="norm_inproj",
    )(x, g, w_main, w_gate_t)


def _gates_kernel(g_ref, bias_ref, sel_ref, grow_ref, gcol_ref, caug_ref, carry_ref):
    t = g_ref.shape[1]

    @pl.when(pl.program_id(1) == 0)
    def _():
        carry_ref[...] = jnp.zeros_like(carry_ref)

    z = g_ref[...] + bias_ref[...]
    row = lax.broadcasted_iota(jnp.int32, z.shape, 0)
    is_input_gate = row < ML_HEADS
    val = jnp.where(is_input_gate, 0.0, _log_sigmoid(z))
    r_i = lax.broadcasted_iota(jnp.int32, (t, t), 0)
    c_i = lax.broadcasted_iota(jnp.int32, (t, t), 1)
    upper = jnp.where(r_i <= c_i, 1.0, 0.0).astype(BF16)
    tot = _dot3(_split3(val), upper) + carry_ref[:, 0:1]
    out = jnp.where(is_input_gate, z, tot)
    grow_ref[...] = out
    carry_ref[...] = jnp.broadcast_to(tot[:, t - 1:t], carry_ref.shape)
    eye = jnp.where(r_i == c_i, 1.0, 0.0).astype(BF16)
    p0, p1, p2 = _split3(out)
    gcol_ref[...] = _dot_nt(eye, p0) + _dot_nt(eye, p1) + _dot_nt(eye, p2)
    n0, n1, n2 = _split3(out * (-LOG2E))
    zrows = _dot(sel_ref[0], n0) + _dot(sel_ref[1], n1) + _dot(sel_ref[2], n2)
    caug_ref[...] = _dot_nt(eye, zrows.astype(BF16)).astype(BF16)


def _bias_lane_selectors():
    sel = [[[0.0] * N_GATE_ROWS for _ in range(FX_HEADS // 2 * LANES)] for _ in range(3)]
    for j in range(3):
        for p in range(FX_HEADS // 2):
            for a in range(2):
                sel[j][LANES * p + 3 * a + j][2 * ML_HEADS + 2 * p + a] = 1.0
    return jnp.asarray(sel, BF16)


def gates(g_row, bias, batch):
    r, n = g_row.shape
    s = n // batch
    t = min(T_GATE, s)
    nb = s // t
    sel = _bias_lane_selectors()
    wc = sel.shape[1]
    return pl.pallas_call(
        _gates_kernel,
        out_shape=(jax.ShapeDtypeStruct((r, n), F32), jax.ShapeDtypeStruct((n, r), F32),
                   jax.ShapeDtypeStruct((n, wc), BF16)),
        grid=(batch, nb),
        in_specs=[pl.BlockSpec((r, t), lambda b, j: (0, b * nb + j)),
                  pl.BlockSpec((r, 1), lambda b, j: (0, 0)),
                  pl.BlockSpec(sel.shape, lambda b, j: (0, 0, 0))],
        out_specs=(pl.BlockSpec((r, t), lambda b, j: (0, b * nb + j)),
                   pl.BlockSpec((t, r), lambda b, j: (b * nb + j, 0)),
                   pl.BlockSpec((t, wc), lambda b, j: (b * nb + j, 0))),
        scratch_shapes=[pltpu.VMEM((r, LANES), F32)],
        compiler_params=_cparams(("arbitrary", "arbitrary")),
        name="gates",
    )(g_row, bias, sel)


def _hgrn_kernel(q_ref, f_ref, i_ref, g_ref, lb_ref, gain_ref, m_ref, o_ref, st_ref, x_ref, y_ref, *, layer):
    t = q_ref.shape[0]
    grp = 16
    ngrp = t // grp

    @pl.when(pl.program_id(1) == 0)
    def _():
        st_ref[...] = jnp.zeros_like(st_ref)

    lbp = lb_ref[...]
    rows = [lbp[r:r + 1, :] for r in range(lbp.shape[0])]
    mx = functools.reduce(jnp.maximum, rows)
    es = [jnp.exp(r - mx) for r in rows]
    tot = functools.reduce(lambda a, b: a + b, es)
    cs, run = [], None
    for e in es:
        run = e / tot if run is None else run + e / tot
        cs.append(run)
    lb = cs[layer] - cs[0]

    z = f_ref[...].astype(F32)
    a = jnp.log(lb)
    bb = jnp.log1p(-lb) + _log_sigmoid(z)
    log_f = jnp.maximum(a, bb) + jnp.log1p(jnp.exp(-jnp.abs(a - bb)))
    k = (1.0 - lb) * _sigmoid(-z)
    q = q_ref[...].astype(F32)
    v = i_ref[...].astype(F32)
    m_bf = m_ref[...]

    r_i = lax.broadcasted_iota(jnp.int32, (t, t), 0)
    c_i = lax.broadcasted_iota(jnp.int32, (t, t), 1)
    lower = jnp.where(c_i <= r_i, 1.0, 0.0).astype(BF16)
    f0, f1, f2 = _split3(log_f)
    b = _dot(lower, f0) + _dot(lower, f1) + _dot(lower, f2)

    st = st_ref[...]
    o_inter = _dot_nt((q * jnp.exp(b)).astype(BF16), st.astype(BF16))

    qg = [q[g * grp:(g + 1) * grp, :] for g in range(ngrp)]
    b2 = b * LOG2E
    bg = [b2[g * grp:(g + 1) * grp, :] for g in range(ngrp)]
    t_in_grp = lax.broadcasted_iota(jnp.int32, (grp, q.shape[1]), 0)
    off = 0
    for s in range(t):
        gs = s // grp
        bs = b2[s:s + 1, :]
        ks = k[s:s + 1, :]
        for g in range(gs, ngrp):
            diff = bg[g] - bs
            if g == gs:
                diff = jnp.where(t_in_grp >= (s - gs * grp), diff, NEG_INF)
            x_ref[off:off + grp, :] = (qg[g] * (ks * jnp.exp2(diff))).astype(BF16)
            off += grp
    y_ref[...] = _dot(x_ref[...], m_bf)
    og = [o_inter[g * grp:(g + 1) * grp, :] for g in range(ngrp)]
    off = 0
    for s in range(t):
        gs = s // grp
        vs = v[s:s + 1, :]
        for g in range(gs, ngrp):
            og[g] = og[g] + y_ref[off:off + grp, :] * vs
            off += grp
    o = jnp.concatenate(og, axis=0)

    b_end = b[t - 1:t, :]
    kd = k * jnp.exp(b_end - b)
    upd = _dot_tn(v.astype(BF16), kd.astype(BF16))
    st_ref[...] = st * jnp.exp(b_end) + upd * m_bf.astype(F32)

    gt = g_ref[...].astype(F32)
    y = o * lax.rsqrt(_head_mean_sq(o, m_bf) + EPS) * gain_ref[...] * (gt * _sigmoid(gt))
    o_ref[...] = y.astype(o_ref.dtype)


def hgrn2(main, lb_all, gain, m_bf, batch, layer):
    n = main.shape[0]
    s = n // batch
    t = T_HG
    nc = s // t
    w = HG_W
    n_pairs = sum(t // 16 - si // 16 for si in range(t)) * 16
    col = lambda cidx: pl.BlockSpec((t, w), lambda b, c: (b * nc + c, cidx))
    full = lambda shape: pl.BlockSpec(shape, lambda b, c: (0, 0))
    return pl.pallas_call(
        functools.partial(_hgrn_kernel, layer=layer),
        out_shape=jax.ShapeDtypeStruct((n, w), BF16),
        grid=(batch, nc),
        in_specs=[col(0), col(1), col(2), col(3), full(lb_all.shape), full((1, w)), full((w, w))],
        out_specs=pl.BlockSpec((t, w), lambda b, c: (b * nc + c, 0)),
        scratch_shapes=[pltpu.VMEM((w, w), F32), pltpu.VMEM((n_pairs, w), BF16), pltpu.VMEM((n_pairs, w), F32)],
        compiler_params=_cparams(("arbitrary", "arbitrary")),
        name="hgrn2",
    )(main, main, main, main, lb_all, gain, m_bf)


def _mlstm_kernel(q_ref, k_ref, v_ref, og_ref, grow_ref, gcol_ref, cw_ref, gain_ref, m_ref, o_ref,
                  ext_ref, ct_ref, n_ref, mm_ref):
    t = q_ref.shape[0]
    w = q_ref.shape[1]
    halo = 8

    @pl.when(pl.program_id(1) == 0)
    def _():
        ext_ref[0:halo, :] = jnp.zeros((halo, 2 * w), F32)
        ct_ref[...] = jnp.zeros_like(ct_ref)
        n_ref[...] = jnp.zeros_like(n_ref)
        mm_ref[...] = jnp.zeros_like(mm_ref)

    ext_ref[halo:halo + t, 0:w] = q_ref[...].astype(F32)
    ext_ref[halo:halo + t, w:2 * w] = k_ref[...].astype(F32)
    cw = cw_ref[...]
    y = None
    for j in range(CONV_K):
        term = ext_ref[halo - (CONV_K - 1) + j:halo - (CONV_K - 1) + j + t, :] * cw[j:j + 1, :]
        y = term if y is None else y + term
    tail = ext_ref[t:t + halo, :]
    ext_ref[0:halo, :] = tail
    qk = y * _sigmoid(y)
    q = qk[:, 0:w]
    k = qk[:, w:2 * w] * (HEAD_DIM ** -0.5)
    kb = k.astype(BF16)
    vb = v_ref[...]
    m_bf = m_ref[...]

    grow = grow_ref[...]
    gcol = gcol_ref[...]
    lane_head = lax.broadcasted_iota(jnp.int32, (1, w), 1) // HEAD_DIM
    r_i = lax.broadcasted_iota(jnp.int32, (t, t), 0)
    c_i = lax.broadcasted_iota(jnp.int32, (t, t), 1)
    causal = c_i <= r_i

    num_intra = jnp.zeros((t, w), F32)
    sint_l = jnp.zeros((t, w), F32)
    wsum_l = jnp.zeros((t, w), F32)
    mt_l = jnp.zeros((t, w), F32)
    wk_l = jnp.zeros((t, w), F32)
    decay_l = jnp.zeros((1, w), F32)
    for h in range(ML_HEADS):
        sel = lane_head == h
        qh = jnp.where(sel, q, 0.0).astype(BF16)
        s = _dot_nt(qh, kb)
        bc = gcol[:, ML_HEADS + h:ML_HEADS + h + 1]
        br = grow[ML_HEADS + h:ML_HEADS + h + 1, :]
        lir = grow[h:h + 1, :]
        lic = gcol[:, h:h + 1]
        dlog = jnp.where(causal, bc - br + lir, NEG_INF)
        mmh = mm_ref[h:h + 1, 0:1]
        inter = bc + mmh
        m_t = jnp.maximum(jnp.max(dlog, axis=1, keepdims=True), inter)
        wgt = s * jnp.exp(dlog - m_t)
        s_int = jnp.exp(inter - m_t)
        pv = _dot(wgt.astype(BF16), vb)
        num_intra = jnp.where(sel, pv, num_intra)
        sint_l = jnp.where(sel, s_int, sint_l)
        wsum_l = jnp.where(sel, jnp.sum(wgt, axis=1, keepdims=True), wsum_l)
        mt_l = jnp.where(sel, m_t, mt_l)
        b_end = br[:, t - 1:t]
        m_new = jnp.maximum(b_end + mmh, jnp.max(b_end - br + lir, axis=1, keepdims=True))
        wk_l = jnp.where(sel, jnp.exp(b_end - bc + lic - m_new), wk_l)
        decay_l = jnp.where(sel, jnp.exp(b_end + mmh - m_new), decay_l)
        mm_ref[h:h + 1, :] = jnp.broadcast_to(m_new - b_end, (1, mm_ref.shape[1]))

    ct = ct_ref[...]
    nrow = n_ref[0:1, :]
    q_c = _dot_nt(q.astype(BF16), ct.astype(BF16))
    qn = q * nrow
    qn_hi = qn.astype(BF16)
    qn_lo = (qn - qn_hi.astype(F32)).astype(BF16)
    qn_l = _dot(qn_hi, m_bf) + _dot(qn_lo, m_bf)
    num = num_intra + sint_l * q_c
    den = wsum_l + sint_l * qn_l
    hval = num / jnp.maximum(jnp.abs(den), jnp.exp(-mt_l))

    kw = k * wk_l
    upd = _dot_tn(vb, kw.astype(BF16))
    ct_ref[...] = decay_l * ct + upd * m_bf.astype(F32)
    n_ref[...] = jnp.broadcast_to(decay_l * nrow + jnp.sum(kw, axis=0, keepdims=True), n_ref.shape)

    og = og_ref[...].astype(F32)
    yv = hval * lax.rsqrt(_head_mean_sq(hval, m_bf) + EPS) * gain_ref[...] * _sigmoid(og)
    o_ref[...] = yv.astype(o_ref.dtype)


def mlstm(main, g_row, g_col, conv_w, gain, m_bf, batch):
    n = main.shape[0]
    s = n // batch
    t = min(T_ML, s)
    nc = s // t
    w = ML_W
    col = lambda cidx: pl.BlockSpec((t, w), lambda b, c: (b * nc + c, cidx))
    full = lambda shape: pl.BlockSpec(shape, lambda b, c: (0, 0))
    return pl.pallas_call(
        _mlstm_kernel,
        out_shape=jax.ShapeDtypeStruct((n, w), BF16),
        grid=(batch, nc),
        in_specs=[col(4), col(5), col(6), col(7),
                  pl.BlockSpec((N_GATE_ROWS, t), lambda b, c: (0, b * nc + c)),
                  pl.BlockSpec((t, N_GATE_ROWS), lambda b, c: (b * nc + c, 0)),
                  full((CONV_K, 2 * w)), full((1, w)), full((w, w))],
        out_specs=pl.BlockSpec((t, w), lambda b, c: (b * nc + c, 0)),
        scratch_shapes=[pltpu.VMEM((t + 8, 2 * w), F32), pltpu.VMEM((w, w), F32),
                        pltpu.VMEM((8, w), F32), pltpu.VMEM((8, LANES), F32)],
        compiler_params=_cparams(("arbitrary", "arbitrary")),
        name="mlstm",
    )(main, main, main, main, g_row, g_col, conv_w, gain, m_bf)


def _fox_kernel(q_ref, k_ref, v_ref, c_ref, o_ref, m_ref, acc_ref):
    tq = q_ref.shape[0]
    half = tq // 2
    qi = pl.program_id(2)
    lane = lax.broadcasted_iota(jnp.int32, (1, LANES), 1)
    first = lane < HEAD_DIM

    m_ref[...] = jnp.full(m_ref.shape, NEG_INF, F32)
    acc_ref[...] = jnp.zeros_like(acc_ref)

    def attend(r0, r1, kstart, klen, triangular):
        q2 = q_ref[r0:r1, :]
        v2 = v_ref[pl.ds(kstart, klen), :]
        k_aug = jnp.concatenate([k_ref[pl.ds(kstart, klen), :], c_ref[pl.ds(kstart, klen), :]], axis=1)
        for a in range(2):
            sel = first if a == 0 else jnp.logical_not(first)
            ones_lanes = jnp.logical_and(lane >= 3 * a, lane < 3 * a + 3)
            q_bias = jnp.broadcast_to(jnp.where(ones_lanes, 1.0, 0.0).astype(q2.dtype), q2.shape)
            q_aug = jnp.concatenate([jnp.where(sel, q2, jnp.zeros_like(q2)), q_bias], axis=1)
            s = _dot_nt(q_aug, k_aug)
            if triangular:
                r_i = lax.broadcasted_iota(jnp.int32, s.shape, 0)
                c_i = lax.broadcasted_iota(jnp.int32, s.shape, 1)
                s = jnp.where(c_i <= r_i, s, NEG_INF)
            m_prev = m_ref[a, r0:r1, :]
            m_new = jnp.maximum(m_prev, jnp.max(s, axis=1, keepdims=True))
            p = jnp.concatenate([jnp.exp2(s[:, c * LANES:(c + 1) * LANES] - m_new).astype(v2.dtype)
                                 for c in range(klen // LANES)], axis=1)
            v_aug = jnp.where(sel, v2, jnp.ones_like(v2))
            acc_ref[a, r0:r1, :] = jnp.exp2(m_prev - m_new) * acc_ref[a, r0:r1, :] + _dot(p, v_aug)
            m_ref[a, r0:r1, :] = m_new

    def past_block(j, carry):
        attend(0, tq, pl.multiple_of(j * tq, tq), tq, False)
        return carry

    lax.fori_loop(0, qi, past_block, 0)
    d0 = pl.multiple_of(qi * tq, tq)
    attend(0, half, d0, half, True)
    attend(half, tq, d0, half, False)
    attend(half, tq, d0 + half, half, True)

    acc_a = acc_ref[0]
    acc_b = acc_ref[1]
    out = jnp.where(first, acc_a / pltpu.roll(acc_a, HEAD_DIM, 1), acc_b / pltpu.roll(acc_b, HEAD_DIM, 1))
    o_ref[...] = out.astype(o_ref.dtype)


def fox(main, c_aug, batch):
    n = main.shape[0]
    s = n // batch
    tq = min(TQ_FX, s)
    nq = s // tq
    pairs = FX_HEADS // 2
    qcol, kcol, vcol = 2048 // LANES, 2560 // LANES, 3072 // LANES
    seq_blk = lambda col0: pl.BlockSpec((s, LANES), lambda b, p, i: (b, col0 + p))
    return pl.pallas_call(
        _fox_kernel,
        out_shape=jax.ShapeDtypeStruct((n, FX_W), BF16),
        grid=(batch, pairs, nq),
        in_specs=[pl.BlockSpec((tq, LANES), lambda b, p, i: (b * nq + i, qcol + p)),
                  seq_blk(kcol), seq_blk(vcol), seq_blk(0)],
        out_specs=pl.BlockSpec((tq, LANES), lambda b, p, i: (b * nq + i, p)),
        scratch_shapes=[pltpu.VMEM((2, tq, LANES), F32), pltpu.VMEM((2, tq, LANES), F32)],
        compiler_params=_cparams(("arbitrary", "arbitrary", "arbitrary")),
        name="fox",
    )(main, main, main, c_aug)


def _outproj_body(x_ref, ohg_ref, oml_ref, ofx_ref, w_ref, g_ref):
    acc = x_ref[...]
    acc = acc + _dot(ohg_ref[...], w_ref[0:HG_W, :])
    acc = acc + _dot(oml_ref[...], w_ref[HG_W:HG_W + ML_W, :])
    acc = acc + _dot(ofx_ref[...], w_ref[HG_W + ML_W:, :])
    ms = jnp.mean(acc * acc, axis=-1, keepdims=True)
    h = (acc * lax.rsqrt(ms + EPS)) * g_ref[...]
    return acc, h


def _outproj_dense_kernel(x_ref, ohg_ref, oml_ref, ofx_ref, w_ref, g_ref, xo_ref, h_ref):
    acc, h = _outproj_body(x_ref, ohg_ref, oml_ref, ofx_ref, w_ref, g_ref)
    xo_ref[...] = acc
    h_ref[...] = h.astype(h_ref.dtype)


def _outproj_moe_kernel(x_ref, ohg_ref, oml_ref, ofx_ref, w_ref, g_ref, rhi_ref, rlo_ref,
                        xo_ref, h_ref, idx_ref, gate_ref):
    acc, h = _outproj_body(x_ref, ohg_ref, oml_ref, ofx_ref, w_ref, g_ref)
    xo_ref[...] = acc
    h_ref[...] = h
    h_hi = h.astype(BF16)
    h_lo = (h - h_hi.astype(F32)).astype(BF16)
    logits = _dot(h_hi, rhi_ref[...]) + _dot(h_hi, rlo_ref[...]) + _dot(h_lo, rhi_ref[...])
    lane_i = lax.broadcasted_iota(jnp.int32, logits.shape, 1)
    lane = lane_i.astype(F32)
    lg = jnp.where(lane_i < N_EXPERTS, logits, NEG_INF)
    m1 = jnp.max(lg, axis=1, keepdims=True)
    i1 = jnp.min(jnp.where(lg == m1, lane, float(LANES)), axis=1, keepdims=True)
    lg2 = jnp.where(lane == i1, NEG_INF, lg)
    m2 = jnp.max(lg2, axis=1, keepdims=True)
    i2 = jnp.min(jnp.where(lg2 == m2, lane, float(LANES)), axis=1, keepdims=True)
    e = jnp.exp(m2 - m1)
    g1 = 1.0 / (1.0 + e)
    g2 = e / (1.0 + e)
    idx_ref[...] = jnp.where(lane_i == 0, i1, jnp.where(lane_i == 1, i2, 0.0)).astype(jnp.int32)
    gate_ref[...] = jnp.where(lane_i == 0, g1, jnp.where(lane_i == 1, g2, 0.0))


def outproj(x, o_hg, o_ml, o_fx, w_out, g, router_parts=None):
    n, d = x.shape
    tm = min(TM_PROJ, n)
    row = lambda width: pl.BlockSpec((tm, width), lambda i: (i, 0))
    full = lambda shape: pl.BlockSpec(shape, lambda i: (0, 0))
    in_specs = [row(d), row(HG_W), row(ML_W), row(FX_W), full(w_out.shape), full((1, d))]
    args = [x, o_hg, o_ml, o_fx, w_out, g]
    if router_parts is None:
        kern = _outproj_dense_kernel
        out_shape = (jax.ShapeDtypeStruct((n, d), F32), jax.ShapeDtypeStruct((n, d), BF16))
        out_specs = (row(d), row(d))
    else:
        kern = _outproj_moe_kernel
        in_specs += [full(router_parts[0].shape), full(router_parts[1].shape)]
        args += list(router_parts)
        out_shape = (jax.ShapeDtypeStruct((n, d), F32), jax.ShapeDtypeStruct((n, d), F32),
                     jax.ShapeDtypeStruct((n, LANES), jnp.int32), jax.ShapeDtypeStruct((n, LANES), F32))
        out_specs = (row(d), row(d), row(LANES), row(LANES))
    return pl.pallas_call(
        kern, out_shape=out_shape, grid=(n // tm,), in_specs=in_specs, out_specs=out_specs,
        compiler_params=_cparams(("arbitrary",)), name="outproj",
    )(*args)


def _ffn_kernel(h_ref, x_ref, wg_ref, wu_ref, wd_ref, o_ref):
    @pl.when(pl.program_id(1) == 0)
    def _():
        o_ref[...] = x_ref[...]

    h = h_ref[...]
    gt = _dot(h, wg_ref[...])
    up = _dot(h, wu_ref[...])
    hid = (gt * _sigmoid(gt) * up).astype(BF16)
    o_ref[...] += _dot(hid, wd_ref[...])


def dense_ffn(h, x, wg, wu, wd):
    n, d = x.shape
    ff = wg.shape[1]
    tm = min(TM_FFN, n)
    tf = ff // 2 if (ff // 2) % LANES == 0 else ff
    return pl.pallas_call(
        _ffn_kernel,
        out_shape=jax.ShapeDtypeStruct((n, d), F32),
        grid=(n // tm, ff // tf),
        in_specs=[pl.BlockSpec((tm, d), lambda i, f: (i, 0)),
                  pl.BlockSpec((tm, d), lambda i, f: (i, 0)),
                  pl.BlockSpec((d, tf), lambda i, f: (0, f)),
                  pl.BlockSpec((d, tf), lambda i, f: (0, f)),
                  pl.BlockSpec((tf, d), lambda i, f: (f, 0))],
        out_specs=pl.BlockSpec((tm, d), lambda i, f: (i, 0)),
        compiler_params=_cparams(("arbitrary", "arbitrary")),
        name="dense_ffn",
    )(h, x, wg, wu, wd)


def _row_copy(src_ref, src_row, dst_ref, dst_row, sem):
    return pltpu.make_async_copy(src_ref.at[pl.ds(src_row, 1)], dst_ref.at[pl.ds(dst_row, 1)], sem)


def _dispatch_kernel(dest_ref, h_ref, xs_in_ref, xs_ref, sem, *, tb):
    del xs_in_ref
    base = pl.program_id(0) * tb

    def copy(i, kk):
        return _row_copy(h_ref, i, xs_ref, dest_ref[(base + i) * TOP_K + kk], sem)

    def issue(i, carry):
        for kk in range(TOP_K):
            copy(i, kk).start()
        return carry

    lax.fori_loop(0, tb, issue, 0, unroll=8)

    def drain(i, carry):
        for kk in range(TOP_K):
            copy(i, kk).wait()
        return carry

    lax.fori_loop(0, tb, drain, 0, unroll=8)


def moe_dispatch(dest, h, p_rows):
    n, d = h.shape
    tb = min(T_DISPATCH, n)
    xs0 = jnp.zeros((p_rows, d), h.dtype)
    return pl.pallas_call(
        functools.partial(_dispatch_kernel, tb=tb),
        out_shape=jax.ShapeDtypeStruct((p_rows, d), h.dtype),
        grid_spec=pltpu.PrefetchScalarGridSpec(
            num_scalar_prefetch=1, grid=(n // tb,),
            in_specs=[pl.BlockSpec((tb, d), lambda i, dr: (i, 0)), pl.BlockSpec(memory_space=pl.ANY)],
            out_specs=pl.BlockSpec(memory_space=pl.ANY),
            scratch_shapes=[pltpu.SemaphoreType.DMA(())]),
        input_output_aliases={2: 0},
        compiler_params=_cparams(("arbitrary",)),
        name="moe_dispatch",
    )(dest, h, xs0)


def _experts_kernel(blk_e_ref, nused_ref, x_ref, wg_ref, wu_ref, wd_ref, o_ref):
    m = pl.program_id(0)
    f = pl.program_id(1)

    @pl.when(f == 0)
    def _():
        o_ref[...] = jnp.zeros_like(o_ref)

    @pl.when(m < nused_ref[0])
    def _():
        xb = x_ref[...].astype(BF16)
        gt = _dot(xb, wg_ref[0])
        up = _dot(xb, wu_ref[0])
        hid = (gt * _sigmoid(gt) * up).astype(BF16)
        o_ref[...] += _dot(hid, wd_ref[0])


def moe_experts(blk_expert, n_used, xs, wg, wu, wd):
    p_rows, d = xs.shape
    ff = wg.shape[2]
    bm = BM_MOE
    tf = TF_MOE
    nf = ff // tf

    def w_idx(m, f, be, nu):
        f_eff = jnp.where(m % 2 == 0, f, nf - 1 - f)
        last = jnp.where((nu[0] - 1) % 2 == 0, nf - 1, 0)
        return be[m], jnp.where(m < nu[0], f_eff, last)

    return pl.pallas_call(
        _experts_kernel,
        out_shape=jax.ShapeDtypeStruct((p_rows, d), F32),
        grid_spec=pltpu.PrefetchScalarGridSpec(
            num_scalar_prefetch=2, grid=(p_rows // bm, nf),
            in_specs=[pl.BlockSpec((bm, d), lambda m, f, be, nu: (m, 0)),
                      pl.BlockSpec((1, d, tf), lambda m, f, be, nu: (w_idx(m, f, be, nu)[0], 0, w_idx(m, f, be, nu)[1])),
                      pl.BlockSpec((1, d, tf), lambda m, f, be, nu: (w_idx(m, f, be, nu)[0], 0, w_idx(m, f, be, nu)[1])),
                      pl.BlockSpec((1, tf, d), lambda m, f, be, nu: (w_idx(m, f, be, nu)[0], w_idx(m, f, be, nu)[1], 0))],
            out_specs=pl.BlockSpec((bm, d), lambda m, f, be, nu: (m, 0))),
        compiler_params=_cparams(("arbitrary", "arbitrary"), VMEM_LIMIT_BIG),
        name="moe_experts",
    )(blk_expert, n_used, xs, wg, wu, wd)


def _combine_kernel(dest_ref, x_ref, gate_ref, fg_ref, ys_ref, o_ref, buf_ref, sem, *, tb, final):
    base = pl.program_id(0) * tb

    def copy(i, kk):
        return _row_copy(ys_ref, dest_ref[(base + i) * TOP_K + kk], buf_ref.at[kk], i, sem)

    def issue(i, carry):
        for kk in range(TOP_K):
            copy(i, kk).start()
        return carry

    lax.fori_loop(0, tb, issue, 0, unroll=8)

    def drain(i, carry):
        for kk in range(TOP_K):
            copy(i, kk).wait()
        return carry

    lax.fori_loop(0, tb, drain, 0, unroll=8)
    gate = gate_ref[...]
    y = x_ref[...] + gate[:, 0:1] * buf_ref[0] + gate[:, 1:2] * buf_ref[1]
    if final:
        ms = jnp.mean(y * y, axis=-1, keepdims=True)
        y = (y * lax.rsqrt(ms + EPS)) * fg_ref[...]
    o_ref[...] = y


def moe_combine(dest, x, gate, final_g, ys, final):
    n, d = x.shape
    tb = min(T_DISPATCH, n)
    return pl.pallas_call(
        functools.partial(_combine_kernel, tb=tb, final=final),
        out_shape=jax.ShapeDtypeStruct((n, d), F32),
        grid_spec=pltpu.PrefetchScalarGridSpec(
            num_scalar_prefetch=1, grid=(n // tb,),
            in_specs=[pl.BlockSpec((tb, d), lambda i, dr: (i, 0)),
                      pl.BlockSpec((tb, LANES), lambda i, dr: (i, 0)),
                      pl.BlockSpec((1, d), lambda i, dr: (0, 0)),
                      pl.BlockSpec(memory_space=pl.ANY)],
            out_specs=pl.BlockSpec((tb, d), lambda i, dr: (i, 0)),
            scratch_shapes=[pltpu.VMEM((TOP_K, tb, d), F32), pltpu.SemaphoreType.DMA(())]),
        compiler_params=_cparams(("arbitrary",)),
        name="moe_combine",
    )(dest, x, gate, final_g, ys)


def _final_norm_kernel(x_ref, g_ref, o_ref):
    x = x_ref[...]
    ms = jnp.mean(x * x, axis=-1, keepdims=True)
    o_ref[...] = (x * lax.rsqrt(ms + EPS)) * g_ref[...]


def final_norm(x, g):
    n, d = x.shape
    tm = min(TM_PROJ, n)
    return pl.pallas_call(
        _final_norm_kernel,
        out_shape=jax.ShapeDtypeStruct((n, d), F32),
        grid=(n // tm,),
        in_specs=[pl.BlockSpec((tm, d), lambda i: (i, 0)), pl.BlockSpec((1, d), lambda i: (0, 0))],
        out_specs=pl.BlockSpec((tm, d), lambda i: (i, 0)),
        compiler_params=_cparams(("arbitrary",)),
        name="final_norm",
    )(x, g)


def _moe_plan(top_idx, bm):
    n = top_idx.shape[0]
    a = n * TOP_K
    e_flat = top_idx.reshape(a)
    onehot = (e_flat[:, None] == jnp.arange(N_EXPERTS, dtype=jnp.int32)[None, :]).astype(jnp.int32)
    incl = jnp.cumsum(onehot, axis=0)
    rank = jnp.sum((incl - onehot) * onehot, axis=1)
    counts = incl[-1]
    padded = (counts + bm - 1) // bm * bm
    pad_end = jnp.cumsum(padded)
    pad_start = pad_end - padded
    dest = (jnp.sum(pad_start[None, :] * onehot, axis=1) + rank).astype(jnp.int32)
    p_rows = a + N_EXPERTS * bm
    n_blocks = p_rows // bm
    n_used = (pad_end[-1] // bm).astype(jnp.int32)
    blk_start = jnp.arange(n_blocks, dtype=jnp.int32) * bm
    blk_expert = jnp.sum((blk_start[:, None] >= pad_end[None, :]).astype(jnp.int32), axis=1)
    blk_expert = jnp.minimum(blk_expert, N_EXPERTS - 1)
    last_e = blk_expert[jnp.maximum(n_used - 1, 0)]
    blk_expert = jnp.where(jnp.arange(n_blocks) < n_used, blk_expert, last_e).astype(jnp.int32)
    return dest, blk_expert, n_used.reshape(1), p_rows


def _head_block_mask(width):
    r = jnp.arange(width) // HEAD_DIM
    return (r[:, None] == r[None, :]).astype(BF16)


def kernel(x, attn_norm, w_in, hgrn_lb, hgrn_norm, mlstm_conv, mlstm_b_i, mlstm_b_f, mlstm_norm, fox_b_f, w_out, ffn_norm, dense_w_gate, dense_w_up, dense_w_down, router, moe_w_gate, moe_w_up, moe_w_down, final_norm_g):
    batch, seq, d = x.shape
    depth = w_in.shape[0]
    n = batch * seq
    xf = x.reshape(n, d)
    m_bf = _head_block_mask(HG_W)
    n_main_a = 4 * HG_W + 3 * ML_W
    gate_a = n_main_a
    mo_a = gate_a + 2 * ML_HEADS
    fx_a = mo_a + ML_W
    ff_a = fx_a + 3 * FX_W
    done = False
    for l in range(depth):
        wl = w_in[l]
        wq_fx = wl[:, fx_a:fx_a + FX_W] * (LOG2E * HEAD_DIM ** -0.5)
        w_main = jnp.concatenate([wl[:, :n_main_a], wl[:, mo_a:fx_a], wq_fx, wl[:, fx_a + FX_W:ff_a]],
                                 axis=1).astype(BF16)
        w_gate_t = jnp.concatenate([wl[:, gate_a:mo_a], wl[:, ff_a:]], axis=1).T.astype(BF16)
        main, g_row_raw = norm_inproj(xf, attn_norm[l].reshape(1, d), w_main, w_gate_t)
        bias = jnp.concatenate([mlstm_b_i[l], mlstm_b_f[l], fox_b_f[l]]).reshape(N_GATE_ROWS, 1).astype(F32)
        g_row, g_col, c_aug = gates(g_row_raw, bias, batch)
        o_hg = hgrn2(main, hgrn_lb.astype(F32), hgrn_norm[l].reshape(1, HG_W), m_bf, batch, l)
        o_ml = mlstm(main, g_row, g_col, mlstm_conv[l], mlstm_norm[l].reshape(1, ML_W), m_bf, batch)
        o_fx = fox(main, c_aug, batch)
        wo = w_out[l].astype(BF16)
        fg = ffn_norm[l].reshape(1, d)
        j = l // 2
        if l % 2 == 0:
            x_res, h = outproj(xf, o_hg, o_ml, o_fx, wo, fg)
            xf = dense_ffn(h, x_res, dense_w_gate[j].astype(BF16), dense_w_up[j].astype(BF16),
                           dense_w_down[j].astype(BF16))
        else:
            r = jnp.pad(router[j], ((0, 0), (0, LANES - N_EXPERTS)))
            r_hi = r.astype(BF16)
            r_lo = (r - r_hi.astype(F32)).astype(BF16)
            x_res, h, idx, gate = outproj(xf, o_hg, o_ml, o_fx, wo, fg, (r_hi, r_lo))
            dest, blk_expert, n_used, p_rows = _moe_plan(idx[:, :TOP_K], BM_MOE)
            xs = moe_dispatch(dest, h, p_rows)
            ys = moe_experts(blk_expert, n_used, xs, moe_w_gate[j].astype(BF16), moe_w_up[j].astype(BF16),
                             moe_w_down[j].astype(BF16))
            done = l == depth - 1
            xf = moe_combine(dest, x_res, gate, final_norm_g.reshape(1, d), ys, done)
    if not done:
        xf = final_norm(xf, final_norm_g.reshape(1, d))
    return xf.reshape(batch, seq, d)
```

```python
import functools

import jax
import jax.numpy as jnp
from jax import lax
from jax.experimental import pallas as pl
from jax.experimental.pallas import tpu as pltpu

F32 = jnp.float32
BF16 = jnp.bfloat16
EPS = 1e-6
NEG_INF = float("-inf")
LOG2E = 1.4426950408889634

HEAD_DIM = 64
HG_W = 256
ML_W = 256
FX_W = 512
ML_HEADS = 4
FX_HEADS = 8
CONV_K = 4
N_EXPERTS = 8
TOP_K = 2
N_GATE_ROWS = 16

LANES = 128
VMEM_LIMIT = 48 * 1024 * 1024
VMEM_LIMIT_BIG = 56 * 1024 * 1024

TM_PROJ = 512
T_GATE = 512
T_HG = 128
SUB_HG = 16
T_ML = 128
TQ_FX = 1024
TM_FFN = 512
BM_MOE = 512
TF_MOE = 1792
T_DISPATCH = 256


def _cparams(sem, vmem=VMEM_LIMIT):
    return pltpu.CompilerParams(dimension_semantics=sem, vmem_limit_bytes=vmem)


def _split3(x):
    hi = x.astype(BF16)
    r = x - hi.astype(F32)
    mid = r.astype(BF16)
    lo = (r - mid.astype(F32)).astype(BF16)
    return hi, mid, lo


def _dot(a, b):
    return jnp.dot(a, b, preferred_element_type=F32)


def _dot_nt(a, b):
    return lax.dot_general(a, b, (((1,), (1,)), ((), ())), preferred_element_type=F32)


def _dot_tn(a, b):
    return lax.dot_general(a, b, (((0,), (0,)), ((), ())), preferred_element_type=F32)


def _dot3(parts, b):
    return _dot(parts[0], b) + _dot(parts[1], b) + _dot(parts[2], b)


def _log_sigmoid(z):
    return -(jnp.maximum(-z, 0.0) + jnp.log1p(jnp.exp(-jnp.abs(z))))


def _sigmoid(z):
    return 1.0 / (1.0 + jnp.exp(-z))


def _head_mean_sq(o, m_bf):
    o2 = o * o
    hi = o2.astype(BF16)
    lo = (o2 - hi.astype(F32)).astype(BF16)
    return (_dot(hi, m_bf) + _dot(lo, m_bf)) * (1.0 / HEAD_DIM)


def _norm_inproj_kernel(x_ref, g_ref, w_ref, wgt_ref, main_ref, grow_ref, *, tn):
    x = x_ref[...]
    ms = jnp.mean(x * x, axis=-1, keepdims=True)
    h = ((x * lax.rsqrt(ms + EPS)) * g_ref[...]).astype(BF16)
    for j in range(w_ref.shape[1] // tn):
        main_ref[:, j * tn:(j + 1) * tn] = _dot(h, w_ref[:, j * tn:(j + 1) * tn]).astype(BF16)
    grow_ref[...] = _dot_nt(wgt_ref[...], h)


def norm_inproj(x, g, w_main, w_gate_t):
    n, d = x.shape
    wm = w_main.shape[1]
    tm = min(TM_PROJ, n)
    return pl.pallas_call(
        functools.partial(_norm_inproj_kernel, tn=512),
        out_shape=(jax.ShapeDtypeStruct((n, wm), BF16), jax.ShapeDtypeStruct((N_GATE_ROWS, n), F32)),
        grid=(n // tm,),
        in_specs=[pl.BlockSpec((tm, d), lambda i: (i, 0)),
                  pl.BlockSpec((1, d), lambda i: (0, 0)),
                  pl.BlockSpec((d, wm), lambda i: (0, 0)),
                  pl.BlockSpec((N_GATE_ROWS, d), lambda i: (0, 0))],
        out_specs=(pl.BlockSpec((tm, wm), lambda i: (i, 0)),
                   pl.BlockSpec((N_GATE_ROWS, tm), lambda i: (0, i))),
        compiler_params=_cparams(("arbitrary",)),
        name="norm_inproj",
    )(x, g, w_main, w_gate_t)


def _gates_kernel(g_ref, bias_ref, sel_ref, grow_ref, gcol_ref, caug_ref, carry_ref):
    t = g_ref.shape[1]

    @pl.when(pl.program_id(1) == 0)
    def _():
        carry_ref[...] = jnp.zeros_like(carry_ref)

    z = g_ref[...] + bias_ref[...]
    row = lax.broadcasted_iota(jnp.int32, z.shape, 0)
    is_input_gate = row < ML_HEADS
    val = jnp.where(is_input_gate, 0.0, _log_sigmoid(z))
    r_i = lax.broadcasted_iota(jnp.int32, (t, t), 0)
    c_i = lax.broadcasted_iota(jnp.int32, (t, t), 1)
    upper = jnp.where(r_i <= c_i, 1.0, 0.0).astype(BF16)
    tot = _dot3(_split3(val), upper) + carry_ref[:, 0:1]
    out = jnp.where(is_input_gate, z, tot)
    grow_ref[...] = out
    carry_ref[...] = jnp.broadcast_to(tot[:, t - 1:t], carry_ref.shape)
    eye = jnp.where(r_i == c_i, 1.0, 0.0).astype(BF16)
    p0, p1, p2 = _split3(out)
    gcol_ref[...] = _dot_nt(eye, p0) + _dot_nt(eye, p1) + _dot_nt(eye, p2)
    n0, n1, n2 = _split3(out * (-LOG2E))
    zrows = _dot(sel_ref[0], n0) + _dot(sel_ref[1], n1) + _dot(sel_ref[2], n2)
    caug_ref[...] = _dot_nt(eye, zrows.astype(BF16)).astype(BF16)


def _bias_lane_selectors():
    sel = [[[0.0] * N_GATE_ROWS for _ in range(FX_HEADS // 2 * LANES)] for _ in range(3)]
    for j in range(3):
        for p in range(FX_HEADS // 2):
            for a in range(2):
                sel[j][LANES * p + 3 * a + j][2 * ML_HEADS + 2 * p + a] = 1.0
    return jnp.asarray(sel, BF16)


def gates(g_row, bias, batch):
    r, n = g_row.shape
    s = n // batch
    t = min(T_GATE, s)
    nb = s // t
    sel = _bias_lane_selectors()
    wc = sel.shape[1]
    return pl.pallas_call(
        _gates_kernel,
        out_shape=(jax.ShapeDtypeStruct((r, n), F32), jax.ShapeDtypeStruct((n, r), F32),
                   jax.ShapeDtypeStruct((n, wc), BF16)),
        grid=(batch, nb),
        in_specs=[pl.BlockSpec((r, t), lambda b, j: (0, b * nb + j)),
                  pl.BlockSpec((r, 1), lambda b, j: (0, 0)),
                  pl.BlockSpec(sel.shape, lambda b, j: (0, 0, 0))],
        out_specs=(pl.BlockSpec((r, t), lambda b, j: (0, b * nb + j)),
                   pl.BlockSpec((t, r), lambda b, j: (b * nb + j, 0)),
                   pl.BlockSpec((t, wc), lambda b, j: (b * nb + j, 0))),
        scratch_shapes=[pltpu.VMEM((r, LANES), F32)],
        compiler_params=_cparams(("arbitrary", "arbitrary")),
        name="gates",
    )(g_row, bias, sel)


def _hgrn_kernel(q_ref, f_ref, i_ref, g_ref, lb_ref, gain_ref, m_ref, o_ref, st_ref, x_ref, y_ref, *, layer):
    t = q_ref.shape[0]
    sub = SUB_HG
    nsub = t // sub

    @pl.when(pl.program_id(1) == 0)
    def _():
        st_ref[...] = jnp.zeros_like(st_ref)

    lbp = lb_ref[...]
    rows = [lbp[r:r + 1, :] for r in range(lbp.shape[0])]
    mx = functools.reduce(jnp.maximum, rows)
    es = [jnp.exp(r - mx) for r in rows]
    tot = functools.reduce(lambda a, b: a + b, es)
    cs, run = [], None
    for e in es:
        run = e / tot if run is None else run + e / tot
        cs.append(run)
    lb = cs[layer] - cs[0]

    z = f_ref[...].astype(F32)
    a = jnp.log(lb)
    bb = jnp.log1p(-lb) + _log_sigmoid(z)
    log_f = jnp.maximum(a, bb) + jnp.log1p(jnp.exp(-jnp.abs(a - bb)))
    k = (1.0 - lb) * _sigmoid(-z)
    q = q_ref[...].astype(F32)
    v = i_ref[...].astype(F32)
    m_bf = m_ref[...]

    r_i = lax.broadcasted_iota(jnp.int32, (t, t), 0)
    c_i = lax.broadcasted_iota(jnp.int32, (t, t), 1)
    same_sub = (c_i // sub) == (r_i // sub)
    lower = jnp.where(jnp.logical_and(c_i <= r_i, same_sub), 1.0, 0.0).astype(BF16)
    f0, f1, f2 = _split3(log_f)
    b = _dot(lower, f0) + _dot(lower, f1) + _dot(lower, f2)
    b2 = b * LOG2E
    m_f32 = m_bf.astype(F32)
    t_in_sub = lax.broadcasted_iota(jnp.int32, (sub, q.shape[1]), 0)

    st = st_ref[...]
    o_inter = []
    for i in range(nsub):
        rows = slice(i * sub, (i + 1) * sub)
        bi, b2i, qi, ki = b[rows], b2[rows], q[rows], k[rows]
        for s in range(sub):
            diff = b2i - b2i[s:s + 1, :]
            if s > 0:
                diff = jnp.where(t_in_sub >= s, diff, NEG_INF)
            base = (i * sub + s) * sub
            x_ref[base:base + sub, :] = (qi * (ki[s:s + 1, :] * jnp.exp2(diff))).astype(BF16)
        o_inter.append(_dot_nt((qi * jnp.exp(bi)).astype(BF16), st.astype(BF16)))
        b_end = bi[sub - 1:sub, :]
        kd = ki * jnp.exp(b_end - bi)
        st = st * jnp.exp(b_end) + _dot_tn(v[rows].astype(BF16), kd.astype(BF16)) * m_f32
    st_ref[...] = st
    y_ref[...] = _dot(x_ref[...], m_bf)
    outs = []
    for i in range(nsub):
        vi = v[i * sub:(i + 1) * sub]
        acc = o_inter[i]
        for s in range(sub):
            base = (i * sub + s) * sub
            acc = acc + y_ref[base:base + sub, :] * vi[s:s + 1, :]
        outs.append(acc)
    o = jnp.concatenate(outs, axis=0)

    gt = g_ref[...].astype(F32)
    y = o * lax.rsqrt(_head_mean_sq(o, m_bf) + EPS) * gain_ref[...] * (gt * _sigmoid(gt))
    o_ref[...] = y.astype(o_ref.dtype)


def hgrn2(main, lb_all, gain, m_bf, batch, layer):
    n = main.shape[0]
    s = n // batch
    t = T_HG
    nc = s // t
    w = HG_W
    n_pairs = t * SUB_HG
    col = lambda cidx: pl.BlockSpec((t, w), lambda b, c: (b * nc + c, cidx))
    full = lambda shape: pl.BlockSpec(shape, lambda b, c: (0, 0))
    return pl.pallas_call(
        functools.partial(_hgrn_kernel, layer=layer),
        out_shape=jax.ShapeDtypeStruct((n, w), BF16),
        grid=(batch, nc),
        in_specs=[col(0), col(1), col(2), col(3), full(lb_all.shape), full((1, w)), full((w, w))],
        out_specs=pl.BlockSpec((t, w), lambda b, c: (b * nc + c, 0)),
        scratch_shapes=[pltpu.VMEM((w, w), F32), pltpu.VMEM((n_pairs, w), BF16), pltpu.VMEM((n_pairs, w), F32)],
        compiler_params=_cparams(("arbitrary", "arbitrary")),
        name="hgrn2",
    )(main, main, main, main, lb_all, gain, m_bf)


def _mlstm_kernel(q_ref, k_ref, v_ref, og_ref, grow_ref, gcol_ref, cw_ref, gain_ref, m_ref, o_ref,
                  ext_ref, ct_ref, n_ref, mm_ref):
    t = q_ref.shape[0]
    w = q_ref.shape[1]
    halo = 8

    @pl.when(pl.program_id(1) == 0)
    def _():
        ext_ref[0:halo, :] = jnp.zeros((halo, 2 * w), F32)
        ct_ref[...] = jnp.zeros_like(ct_ref)
        n_ref[...] = jnp.zeros_like(n_ref)
        mm_ref[...] = jnp.zeros_like(mm_ref)

    ext_ref[halo:halo + t, 0:w] = q_ref[...].astype(F32)
    ext_ref[halo:halo + t, w:2 * w] = k_ref[...].astype(F32)
    cw = cw_ref[...]
    y = None
    for j in range(CONV_K):
        term = ext_ref[halo - (CONV_K - 1) + j:halo - (CONV_K - 1) + j + t, :] * cw[j:j + 1, :]
        y = term if y is None else y + term
    tail = ext_ref[t:t + halo, :]
    ext_ref[0:halo, :] = tail
    qk = y * _sigmoid(y)
    q = qk[:, 0:w]
    k = qk[:, w:2 * w] * (HEAD_DIM ** -0.5)
    kb = k.astype(BF16)
    vb = v_ref[...]
    m_bf = m_ref[...]

    grow = grow_ref[...]
    gcol = gcol_ref[...]
    lane_head = lax.broadcasted_iota(jnp.int32, (1, w), 1) // HEAD_DIM
    r_i = lax.broadcasted_iota(jnp.int32, (t, t), 0)
    c_i = lax.broadcasted_iota(jnp.int32, (t, t), 1)
    causal = c_i <= r_i

    num_intra = jnp.zeros((t, w), F32)
    sint_l = jnp.zeros((t, w), F32)
    wsum_l = jnp.zeros((t, w), F32)
    mt_l = jnp.zeros((t, w), F32)
    wk_l = jnp.zeros((t, w), F32)
    decay_l = jnp.zeros((1, w), F32)
    for h in range(ML_HEADS):
        sel = lane_head == h
        qh = jnp.where(sel, q, 0.0).astype(BF16)
        s = _dot_nt(qh, kb)
        bc = gcol[:, ML_HEADS + h:ML_HEADS + h + 1]
        br = grow[ML_HEADS + h:ML_HEADS + h + 1, :]
        lir = grow[h:h + 1, :]
        lic = gcol[:, h:h + 1]
        dlog = jnp.where(causal, bc - br + lir, NEG_INF)
        mmh = mm_ref[h:h + 1, 0:1]
        inter = bc + mmh
        m_t = jnp.maximum(jnp.max(dlog, axis=1, keepdims=True), inter)
        wgt = s * jnp.exp(dlog - m_t)
        s_int = jnp.exp(inter - m_t)
        pv = _dot(wgt.astype(BF16), vb)
        num_intra = jnp.where(sel, pv, num_intra)
        sint_l = jnp.where(sel, s_int, sint_l)
        wsum_l = jnp.where(sel, jnp.sum(wgt, axis=1, keepdims=True), wsum_l)
        mt_l = jnp.where(sel, m_t, mt_l)
        b_end = br[:, t - 1:t]
        m_new = jnp.maximum(b_end + mmh, jnp.max(b_end - br + lir, axis=1, keepdims=True))
        wk_l = jnp.where(sel, jnp.exp(b_end - bc + lic - m_new), wk_l)
        decay_l = jnp.where(sel, jnp.exp(b_end + mmh - m_new), decay_l)
        mm_ref[h:h + 1, :] = jnp.broadcast_to(m_new - b_end, (1, mm_ref.shape[1]))

    ct = ct_ref[...]
    nrow = n_ref[0:1, :]
    q_c = _dot_nt(q.astype(BF16), ct.astype(BF16))
    qn = q * nrow
    qn_hi = qn.astype(BF16)
    qn_lo = (qn - qn_hi.astype(F32)).astype(BF16)
    qn_l = _dot(qn_hi, m_bf) + _dot(qn_lo, m_bf)
    num = num_intra + sint_l * q_c
    den = wsum_l + sint_l * qn_l
    hval = num / jnp.maximum(jnp.abs(den), jnp.exp(-mt_l))

    kw = k * wk_l
    upd = _dot_tn(vb, kw.astype(BF16))
    ct_ref[...] = decay_l * ct + upd * m_bf.astype(F32)
    n_ref[...] = jnp.broadcast_to(decay_l * nrow + jnp.sum(kw, axis=0, keepdims=True), n_ref.shape)

    og = og_ref[...].astype(F32)
    yv = hval * lax.rsqrt(_head_mean_sq(hval, m_bf) + EPS) * gain_ref[...] * _sigmoid(og)
    o_ref[...] = yv.astype(o_ref.dtype)


def mlstm(main, g_row, g_col, conv_w, gain, m_bf, batch):
    n = main.shape[0]
    s = n // batch
    t = min(T_ML, s)
    nc = s // t
    w = ML_W
    col = lambda cidx: pl.BlockSpec((t, w), lambda b, c: (b * nc + c, cidx))
    full = lambda shape: pl.BlockSpec(shape, lambda b, c: (0, 0))
    return pl.pallas_call(
        _mlstm_kernel,
        out_shape=jax.ShapeDtypeStruct((n, w), BF16),
        grid=(batch, nc),
        in_specs=[col(4), col(5), col(6), col(7),
                  pl.BlockSpec((N_GATE_ROWS, t), lambda b, c: (0, b * nc + c)),
                  pl.BlockSpec((t, N_GATE_ROWS), lambda b, c: (b * nc + c, 0)),
                  full((CONV_K, 2 * w)), full((1, w)), full((w, w))],
        out_specs=pl.BlockSpec((t, w), lambda b, c: (b * nc + c, 0)),
        scratch_shapes=[pltpu.VMEM((t + 8, 2 * w), F32), pltpu.VMEM((w, w), F32),
                        pltpu.VMEM((8, w), F32), pltpu.VMEM((8, LANES), F32)],
        compiler_params=_cparams(("arbitrary", "arbitrary")),
        name="mlstm",
    )(main, main, main, main, g_row, g_col, conv_w, gain, m_bf)


def _fox_kernel(q_ref, k_ref, v_ref, c_ref, o_ref, m_ref, acc_ref):
    tq = q_ref.shape[0]
    half = tq // 2
    qi = pl.program_id(2)
    lane = lax.broadcasted_iota(jnp.int32, (1, LANES), 1)
    first = lane < HEAD_DIM

    m_ref[...] = jnp.full(m_ref.shape, NEG_INF, F32)
    acc_ref[...] = jnp.zeros_like(acc_ref)

    def attend(r0, r1, kstart, klen, triangular):
        q2 = q_ref[r0:r1, :]
        v2 = v_ref[pl.ds(kstart, klen), :]
        k_aug = jnp.concatenate([k_ref[pl.ds(kstart, klen), :], c_ref[pl.ds(kstart, klen), :]], axis=1)
        for a in range(2):
            sel = first if a == 0 else jnp.logical_not(first)
            ones_lanes = jnp.logical_and(lane >= 3 * a, lane < 3 * a + 3)
            q_bias = jnp.broadcast_to(jnp.where(ones_lanes, 1.0, 0.0).astype(q2.dtype), q2.shape)
            q_aug = jnp.concatenate([jnp.where(sel, q2, jnp.zeros_like(q2)), q_bias], axis=1)
            s = _dot_nt(q_aug, k_aug)
            if triangular:
                r_i = lax.broadcasted_iota(jnp.int32, s.shape, 0)
                c_i = lax.broadcasted_iota(jnp.int32, s.shape, 1)
                s = jnp.where(c_i <= r_i, s, NEG_INF)
            m_prev = m_ref[a, r0:r1, :]
            m_new = jnp.maximum(m_prev, jnp.max(s, axis=1, keepdims=True))
            p = jnp.concatenate([jnp.exp2(s[:, c * LANES:(c + 1) * LANES] - m_new).astype(v2.dtype)
                                 for c in range(klen // LANES)], axis=1)
            v_aug = jnp.where(sel, v2, jnp.ones_like(v2))
            acc_ref[a, r0:r1, :] = jnp.exp2(m_prev - m_new) * acc_ref[a, r0:r1, :] + _dot(p, v_aug)
            m_ref[a, r0:r1, :] = m_new

    def past_block(j, carry):
        attend(0, tq, pl.multiple_of(j * tq, tq), tq, False)
        return carry

    lax.fori_loop(0, qi, past_block, 0)
    d0 = pl.multiple_of(qi * tq, tq)
    attend(0, half, d0, half, True)
    attend(half, tq, d0, half, False)
    attend(half, tq, d0 + half, half, True)

    acc_a = acc_ref[0]
    acc_b = acc_ref[1]
    out = jnp.where(first, acc_a / pltpu.roll(acc_a, HEAD_DIM, 1), acc_b / pltpu.roll(acc_b, HEAD_DIM, 1))
    o_ref[...] = out.astype(o_ref.dtype)


def fox(main, c_aug, batch):
    n = main.shape[0]
    s = n // batch
    tq = min(TQ_FX, s)
    nq = s // tq
    pairs = FX_HEADS // 2
    qcol, kcol, vcol = 2048 // LANES, 2560 // LANES, 3072 // LANES
    seq_blk = lambda col0: pl.BlockSpec((s, LANES), lambda b, p, i: (b, col0 + p))
    return pl.pallas_call(
        _fox_kernel,
        out_shape=jax.ShapeDtypeStruct((n, FX_W), BF16),
        grid=(batch, pairs, nq),
        in_specs=[pl.BlockSpec((tq, LANES), lambda b, p, i: (b * nq + i, qcol + p)),
                  seq_blk(kcol), seq_blk(vcol), seq_blk(0)],
        out_specs=pl.BlockSpec((tq, LANES), lambda b, p, i: (b * nq + i, p)),
        scratch_shapes=[pltpu.VMEM((2, tq, LANES), F32), pltpu.VMEM((2, tq, LANES), F32)],
        compiler_params=_cparams(("arbitrary", "arbitrary", "arbitrary")),
        name="fox",
    )(main, main, main, c_aug)


def _outproj_body(x_ref, ohg_ref, oml_ref, ofx_ref, w_ref, g_ref):
    acc = x_ref[...]
    acc = acc + _dot(ohg_ref[...], w_ref[0:HG_W, :])
    acc = acc + _dot(oml_ref[...], w_ref[HG_W:HG_W + ML_W, :])
    acc = acc + _dot(ofx_ref[...], w_ref[HG_W + ML_W:, :])
    ms = jnp.mean(acc * acc, axis=-1, keepdims=True)
    h = (acc * lax.rsqrt(ms + EPS)) * g_ref[...]
    return acc, h


def _outproj_dense_kernel(x_ref, ohg_ref, oml_ref, ofx_ref, w_ref, g_ref, xo_ref, h_ref):
    acc, h = _outproj_body(x_ref, ohg_ref, oml_ref, ofx_ref, w_ref, g_ref)
    xo_ref[...] = acc
    h_ref[...] = h.astype(h_ref.dtype)


def _outproj_moe_kernel(x_ref, ohg_ref, oml_ref, ofx_ref, w_ref, g_ref, rhi_ref, rlo_ref,
                        xo_ref, h_ref, idx_ref, gate_ref):
    acc, h = _outproj_body(x_ref, ohg_ref, oml_ref, ofx_ref, w_ref, g_ref)
    xo_ref[...] = acc
    h_ref[...] = h
    h_hi = h.astype(BF16)
    h_lo = (h - h_hi.astype(F32)).astype(BF16)
    logits = _dot(h_hi, rhi_ref[...]) + _dot(h_hi, rlo_ref[...]) + _dot(h_lo, rhi_ref[...])
    lane_i = lax.broadcasted_iota(jnp.int32, logits.shape, 1)
    lane = lane_i.astype(F32)
    lg = jnp.where(lane_i < N_EXPERTS, logits, NEG_INF)
    m1 = jnp.max(lg, axis=1, keepdims=True)
    i1 = jnp.min(jnp.where(lg == m1, lane, float(LANES)), axis=1, keepdims=True)
    lg2 = jnp.where(lane == i1, NEG_INF, lg)
    m2 = jnp.max(lg2, axis=1, keepdims=True)
    i2 = jnp.min(jnp.where(lg2 == m2, lane, float(LANES)), axis=1, keepdims=True)
    e = jnp.exp(m2 - m1)
    g1 = 1.0 / (1.0 + e)
    g2 = e / (1.0 + e)
    idx_ref[...] = jnp.where(lane_i == 0, i1, jnp.where(lane_i == 1, i2, 0.0)).astype(jnp.int32)
    gate_ref[...] = jnp.where(lane_i == 0, g1, jnp.where(lane_i == 1, g2, 0.0))


def outproj(x, o_hg, o_ml, o_fx, w_out, g, router_parts=None):
    n, d = x.shape
    tm = min(TM_PROJ, n)
    row = lambda width: pl.BlockSpec((tm, width), lambda i: (i, 0))
    full = lambda shape: pl.BlockSpec(shape, lambda i: (0, 0))
    in_specs = [row(d), row(HG_W), row(ML_W), row(FX_W), full(w_out.shape), full((1, d))]
    args = [x, o_hg, o_ml, o_fx, w_out, g]
    if router_parts is None:
        kern = _outproj_dense_kernel
        out_shape = (jax.ShapeDtypeStruct((n, d), F32), jax.ShapeDtypeStruct((n, d), BF16))
        out_specs = (row(d), row(d))
    else:
        kern = _outproj_moe_kernel
        in_specs += [full(router_parts[0].shape), full(router_parts[1].shape)]
        args += list(router_parts)
        out_shape = (jax.ShapeDtypeStruct((n, d), F32), jax.ShapeDtypeStruct((n, d), F32),
                     jax.ShapeDtypeStruct((n, LANES), jnp.int32), jax.ShapeDtypeStruct((n, LANES), F32))
        out_specs = (row(d), row(d), row(LANES), row(LANES))
    return pl.pallas_call(
        kern, out_shape=out_shape, grid=(n // tm,), in_specs=in_specs, out_specs=out_specs,
        compiler_params=_cparams(("arbitrary",)), name="outproj",
    )(*args)


def _ffn_kernel(h_ref, x_ref, wg_ref, wu_ref, wd_ref, o_ref):
    @pl.when(pl.program_id(1) == 0)
    def _():
        o_ref[...] = x_ref[...]

    h = h_ref[...]
    gt = _dot(h, wg_ref[...])
    up = _dot(h, wu_ref[...])
    hid = (gt * _sigmoid(gt) * up).astype(BF16)
    o_ref[...] += _dot(hid, wd_ref[...])


def dense_ffn(h, x, wg, wu, wd):
    n, d = x.shape
    ff = wg.shape[1]
    tm = min(TM_FFN, n)
    tf = ff // 2 if (ff // 2) % LANES == 0 else ff
    return pl.pallas_call(
        _ffn_kernel,
        out_shape=jax.ShapeDtypeStruct((n, d), F32),
        grid=(n // tm, ff // tf),
        in_specs=[pl.BlockSpec((tm, d), lambda i, f: (i, 0)),
                  pl.BlockSpec((tm, d), lambda i, f: (i, 0)),
                  pl.BlockSpec((d, tf), lambda i, f: (0, f)),
                  pl.BlockSpec((d, tf), lambda i, f: (0, f)),
                  pl.BlockSpec((tf, d), lambda i, f: (f, 0))],
        out_specs=pl.BlockSpec((tm, d), lambda i, f: (i, 0)),
        compiler_params=_cparams(("arbitrary", "arbitrary")),
        name="dense_ffn",
    )(h, x, wg, wu, wd)


def _row_copy(src_ref, src_row, dst_ref, dst_row, sem):
    return pltpu.make_async_copy(src_ref.at[pl.ds(src_row, 1)], dst_ref.at[pl.ds(dst_row, 1)], sem)


def _dispatch_kernel(dest_ref, h_ref, xs_in_ref, xs_ref, sem, *, tb):
    del xs_in_ref
    base = pl.program_id(0) * tb

    def copy(i, kk):
        return _row_copy(h_ref, i, xs_ref, dest_ref[(base + i) * TOP_K + kk], sem)

    def issue(i, carry):
        for kk in range(TOP_K):
            copy(i, kk).start(priority=kk)
        return carry

    lax.fori_loop(0, tb, issue, 0, unroll=8)

    def drain(i, carry):
        for kk in range(TOP_K):
            copy(i, kk).wait()
        return carry

    lax.fori_loop(0, tb, drain, 0, unroll=8)


def moe_dispatch(dest, h, p_rows):
    n, d = h.shape
    tb = min(T_DISPATCH, n)
    xs0 = jnp.zeros((p_rows, d), h.dtype)
    return pl.pallas_call(
        functools.partial(_dispatch_kernel, tb=tb),
        out_shape=jax.ShapeDtypeStruct((p_rows, d), h.dtype),
        grid_spec=pltpu.PrefetchScalarGridSpec(
            num_scalar_prefetch=1, grid=(n // tb,),
            in_specs=[pl.BlockSpec((tb, d), lambda i, dr: (i, 0)), pl.BlockSpec(memory_space=pl.ANY)],
            out_specs=pl.BlockSpec(memory_space=pl.ANY),
            scratch_shapes=[pltpu.SemaphoreType.DMA(())]),
        input_output_aliases={2: 0},
        compiler_params=_cparams(("arbitrary",)),
        name="moe_dispatch",
    )(dest, h, xs0)


def _experts_kernel(blk_e_ref, nused_ref, x_ref, wg_ref, wu_ref, wd_ref, o_ref):
    m = pl.program_id(0)
    f = pl.program_id(1)

    @pl.when(f == 0)
    def _():
        o_ref[...] = jnp.zeros_like(o_ref)

    @pl.when(m < nused_ref[0])
    def _():
        xb = x_ref[...].astype(BF16)
        gt = _dot(xb, wg_ref[0])
        up = _dot(xb, wu_ref[0])
        hid = (gt * _sigmoid(gt) * up).astype(BF16)
        o_ref[...] += _dot(hid, wd_ref[0])


def moe_experts(blk_expert, n_used, xs, wg, wu, wd):
    p_rows, d = xs.shape
    ff = wg.shape[2]
    bm = BM_MOE
    tf = TF_MOE
    nf = ff // tf

    def w_idx(m, f, be, nu):
        f_eff = jnp.where(m % 2 == 0, f, nf - 1 - f)
        last = jnp.where((nu[0] - 1) % 2 == 0, nf - 1, 0)
        return be[m], jnp.where(m < nu[0], f_eff, last)

    return pl.pallas_call(
        _experts_kernel,
        out_shape=jax.ShapeDtypeStruct((p_rows, d), F32),
        grid_spec=pltpu.PrefetchScalarGridSpec(
            num_scalar_prefetch=2, grid=(p_rows // bm, nf),
            in_specs=[pl.BlockSpec((bm, d), lambda m, f, be, nu: (m, 0)),
                      pl.BlockSpec((1, d, tf), lambda m, f, be, nu: (w_idx(m, f, be, nu)[0], 0, w_idx(m, f, be, nu)[1])),
                      pl.BlockSpec((1, d, tf), lambda m, f, be, nu: (w_idx(m, f, be, nu)[0], 0, w_idx(m, f, be, nu)[1])),
                      pl.BlockSpec((1, tf, d), lambda m, f, be, nu: (w_idx(m, f, be, nu)[0], w_idx(m, f, be, nu)[1], 0))],
            out_specs=pl.BlockSpec((bm, d), lambda m, f, be, nu: (m, 0))),
        compiler_params=_cparams(("arbitrary", "arbitrary"), VMEM_LIMIT_BIG),
        name="moe_experts",
    )(blk_expert, n_used, xs, wg, wu, wd)


def _combine_kernel(dest_ref, x_ref, gate_ref, fg_ref, ys_ref, o_ref, buf_ref, sem, *, tb, final):
    base = pl.program_id(0) * tb

    def copy(i, kk):
        return _row_copy(ys_ref, dest_ref[(base + i) * TOP_K + kk], buf_ref.at[kk], i, sem)

    def issue(i, carry):
        for kk in range(TOP_K):
            copy(i, kk).start(priority=kk)
        return carry

    lax.fori_loop(0, tb, issue, 0, unroll=8)

    def drain(i, carry):
        for kk in range(TOP_K):
            copy(i, kk).wait()
        return carry

    lax.fori_loop(0, tb, drain, 0, unroll=8)
    gate = gate_ref[...]
    y = x_ref[...] + gate[:, 0:1] * buf_ref[0] + gate[:, 1:2] * buf_ref[1]
    if final:
        ms = jnp.mean(y * y, axis=-1, keepdims=True)
        y = (y * lax.rsqrt(ms + EPS)) * fg_ref[...]
    o_ref[...] = y


def moe_combine(dest, x, gate, final_g, ys, final):
    n, d = x.shape
    tb = min(T_DISPATCH, n)
    return pl.pallas_call(
        functools.partial(_combine_kernel, tb=tb, final=final),
        out_shape=jax.ShapeDtypeStruct((n, d), F32),
        grid_spec=pltpu.PrefetchScalarGridSpec(
            num_scalar_prefetch=1, grid=(n // tb,),
            in_specs=[pl.BlockSpec((tb, d), lambda i, dr: (i, 0)),
                      pl.BlockSpec((tb, LANES), lambda i, dr: (i, 0)),
                      pl.BlockSpec((1, d), lambda i, dr: (0, 0)),
                      pl.BlockSpec(memory_space=pl.ANY)],
            out_specs=pl.BlockSpec((tb, d), lambda i, dr: (i, 0)),
            scratch_shapes=[pltpu.VMEM((TOP_K, tb, d), F32), pltpu.SemaphoreType.DMA(())]),
        compiler_params=_cparams(("arbitrary",)),
        name="moe_combine",
    )(dest, x, gate, final_g, ys)


def _final_norm_kernel(x_ref, g_ref, o_ref):
    x = x_ref[...]
    ms = jnp.mean(x * x, axis=-1, keepdims=True)
    o_ref[...] = (x * lax.rsqrt(ms + EPS)) * g_ref[...]


def final_norm(x, g):
    n, d = x.shape
    tm = min(TM_PROJ, n)
    return pl.pallas_call(
        _final_norm_kernel,
        out_shape=jax.ShapeDtypeStruct((n, d), F32),
        grid=(n // tm,),
        in_specs=[pl.BlockSpec((tm, d), lambda i: (i, 0)), pl.BlockSpec((1, d), lambda i: (0, 0))],
        out_specs=pl.BlockSpec((tm, d), lambda i: (i, 0)),
        compiler_params=_cparams(("arbitrary",)),
        name="final_norm",
    )(x, g)


def _moe_plan(top_idx, bm):
    n = top_idx.shape[0]
    a = n * TOP_K
    e_flat = top_idx.reshape(a)
    onehot = (e_flat[:, None] == jnp.arange(N_EXPERTS, dtype=jnp.int32)[None, :]).astype(jnp.int32)
    incl = jnp.cumsum(onehot, axis=0)
    rank = jnp.sum((incl - onehot) * onehot, axis=1)
    counts = incl[-1]
    padded = (counts + bm - 1) // bm * bm
    pad_end = jnp.cumsum(padded)
    pad_start = pad_end - padded
    dest = (jnp.sum(pad_start[None, :] * onehot, axis=1) + rank).astype(jnp.int32)
    p_rows = a + N_EXPERTS * bm
    n_blocks = p_rows // bm
    n_used = (pad_end[-1] // bm).astype(jnp.int32)
    blk_start = jnp.arange(n_blocks, dtype=jnp.int32) * bm
    blk_expert = jnp.sum((blk_start[:, None] >= pad_end[None, :]).astype(jnp.int32), axis=1)
    blk_expert = jnp.minimum(blk_expert, N_EXPERTS - 1)
    last_e = blk_expert[jnp.maximum(n_used - 1, 0)]
    blk_expert = jnp.where(jnp.arange(n_blocks) < n_used, blk_expert, last_e).astype(jnp.int32)
    return dest, blk_expert, n_used.reshape(1), p_rows


def _head_block_mask(width):
    r = jnp.arange(width) // HEAD_DIM
    return (r[:, None] == r[None, :]).astype(BF16)


def kernel(x, attn_norm, w_in, hgrn_lb, hgrn_norm, mlstm_conv, mlstm_b_i, mlstm_b_f, mlstm_norm, fox_b_f, w_out, ffn_norm, dense_w_gate, dense_w_up, dense_w_down, router, moe_w_gate, moe_w_up, moe_w_down, final_norm_g):
    batch, seq, d = x.shape
    depth = w_in.shape[0]
    n = batch * seq
    xf = x.reshape(n, d)
    m_bf = _head_block_mask(HG_W)
    n_main_a = 4 * HG_W + 3 * ML_W
    gate_a = n_main_a
    mo_a = gate_a + 2 * ML_HEADS
    fx_a = mo_a + ML_W
    ff_a = fx_a + 3 * FX_W
    done = False
    for l in range(depth):
        wl = w_in[l]
        wq_fx = wl[:, fx_a:fx_a + FX_W] * (LOG2E * HEAD_DIM ** -0.5)
        w_main = jnp.concatenate([wl[:, :n_main_a], wl[:, mo_a:fx_a], wq_fx, wl[:, fx_a + FX_W:ff_a]],
                                 axis=1).astype(BF16)
        w_gate_t = jnp.concatenate([wl[:, gate_a:mo_a], wl[:, ff_a:]], axis=1).T.astype(BF16)
        main, g_row_raw = norm_inproj(xf, attn_norm[l].reshape(1, d), w_main, w_gate_t)
        bias = jnp.concatenate([mlstm_b_i[l], mlstm_b_f[l], fox_b_f[l]]).reshape(N_GATE_ROWS, 1).astype(F32)
        g_row, g_col, c_aug = gates(g_row_raw, bias, batch)
        o_hg = hgrn2(main, hgrn_lb.astype(F32), hgrn_norm[l].reshape(1, HG_W), m_bf, batch, l)
        o_ml = mlstm(main, g_row, g_col, mlstm_conv[l], mlstm_norm[l].reshape(1, ML_W), m_bf, batch)
        o_fx = fox(main, c_aug, batch)
        wo = w_out[l].astype(BF16)
        fg = ffn_norm[l].reshape(1, d)
        j = l // 2
        if l % 2 == 0:
            x_res, h = outproj(xf, o_hg, o_ml, o_fx, wo, fg)
            xf = dense_ffn(h, x_res, dense_w_gate[j].astype(BF16), dense_w_up[j].astype(BF16),
                           dense_w_down[j].astype(BF16))
        else:
            r = jnp.pad(router[j], ((0, 0), (0, LANES - N_EXPERTS)))
            r_hi = r.astype(BF16)
            r_lo = (r - r_hi.astype(F32)).astype(BF16)
            x_res, h, idx, gate = outproj(xf, o_hg, o_ml, o_fx, wo, fg, (r_hi, r_lo))
            dest, blk_expert, n_used, p_rows = _moe_plan(idx[:, :TOP_K], BM_MOE)
            xs = moe_dispatch(dest, h, p_rows)
            ys = moe_experts(blk_expert, n_used, xs, moe_w_gate[j].astype(BF16), moe_w_up[j].astype(BF16),
                             moe_w_down[j].astype(BF16))
            done = l == depth - 1
            xf = moe_combine(dest, x_res, gate, final_norm_g.reshape(1, d), ys, done)
    if not done:
        xf = final_norm(xf, final_norm_g.reshape(1, d))
    return xf.reshape(batch, seq, d)
```

```python
import functools

import jax
import jax.numpy as jnp
from jax import lax
from jax.experimental import pallas as pl
from jax.experimental.pallas import tpu as pltpu

F32 = jnp.float32
BF16 = jnp.bfloat16
EPS = 1e-6
NEG_INF = float("-inf")
LOG2E = 1.4426950408889634

HEAD_DIM = 64
HG_W = 256
ML_W = 256
FX_W = 512
ML_HEADS = 4
FX_HEADS = 8
CONV_K = 4
N_EXPERTS = 8
TOP_K = 2
N_GATE_ROWS = 16

LANES = 128
VMEM_LIMIT = 48 * 1024 * 1024
VMEM_LIMIT_BIG = 56 * 1024 * 1024

TM_PROJ = 512
T_GATE = 512
T_HG = 128
SUB_HG = 16
T_ML = 128
TQ_FX = 1024
TM_FFN = 512
BM_MOE = 512
TF_MOE = 1792
CHUNK_MOE = 256


def _cparams(sem, vmem=VMEM_LIMIT):
    return pltpu.CompilerParams(dimension_semantics=sem, vmem_limit_bytes=vmem)


def _split3(x):
    hi = x.astype(BF16)
    r = x - hi.astype(F32)
    mid = r.astype(BF16)
    lo = (r - mid.astype(F32)).astype(BF16)
    return hi, mid, lo


def _dot(a, b):
    return jnp.dot(a, b, preferred_element_type=F32)


def _dot_nt(a, b):
    return lax.dot_general(a, b, (((1,), (1,)), ((), ())), preferred_element_type=F32)


def _dot_tn(a, b):
    return lax.dot_general(a, b, (((0,), (0,)), ((), ())), preferred_element_type=F32)


def _dot3(parts, b):
    return _dot(parts[0], b) + _dot(parts[1], b) + _dot(parts[2], b)


def _log_sigmoid(z):
    return -(jnp.maximum(-z, 0.0) + jnp.log1p(jnp.exp(-jnp.abs(z))))


def _sigmoid(z):
    return 1.0 / (1.0 + jnp.exp(-z))


def _head_mean_sq(o, m_bf):
    o2 = o * o
    hi = o2.astype(BF16)
    lo = (o2 - hi.astype(F32)).astype(BF16)
    return (_dot(hi, m_bf) + _dot(lo, m_bf)) * (1.0 / HEAD_DIM)


def _norm_inproj_kernel(x_ref, g_ref, w_ref, wgt_ref, main_ref, grow_ref, *, tn):
    x = x_ref[...]
    ms = jnp.mean(x * x, axis=-1, keepdims=True)
    h = ((x * lax.rsqrt(ms + EPS)) * g_ref[...]).astype(BF16)
    for j in range(w_ref.shape[1] // tn):
        main_ref[:, j * tn:(j + 1) * tn] = _dot(h, w_ref[:, j * tn:(j + 1) * tn]).astype(BF16)
    grow_ref[...] = _dot_nt(wgt_ref[...], h)


def norm_inproj(x, g, w_main, w_gate_t):
    n, d = x.shape
    wm = w_main.shape[1]
    tm = min(TM_PROJ, n)
    return pl.pallas_call(
        functools.partial(_norm_inproj_kernel, tn=512),
        out_shape=(jax.ShapeDtypeStruct((n, wm), BF16), jax.ShapeDtypeStruct((N_GATE_ROWS, n), F32)),
        grid=(n // tm,),
        in_specs=[pl.BlockSpec((tm, d), lambda i: (i, 0)),
                  pl.BlockSpec((1, d), lambda i: (0, 0)),
                  pl.BlockSpec((d, wm), lambda i: (0, 0)),
                  pl.BlockSpec((N_GATE_ROWS, d), lambda i: (0, 0))],
        out_specs=(pl.BlockSpec((tm, wm), lambda i: (i, 0)),
                   pl.BlockSpec((N_GATE_ROWS, tm), lambda i: (0, i))),
        compiler_params=_cparams(("arbitrary",)),
        name="norm_inproj",
    )(x, g, w_main, w_gate_t)


def _gates_kernel(g_ref, bias_ref, sel_ref, grow_ref, gcol_ref, caug_ref, carry_ref):
    t = g_ref.shape[1]

    @pl.when(pl.program_id(1) == 0)
    def _():
        carry_ref[...] = jnp.zeros_like(carry_ref)

    z = g_ref[...] + bias_ref[...]
    row = lax.broadcasted_iota(jnp.int32, z.shape, 0)
    is_input_gate = row < ML_HEADS
    val = jnp.where(is_input_gate, 0.0, _log_sigmoid(z))
    r_i = lax.broadcasted_iota(jnp.int32, (t, t), 0)
    c_i = lax.broadcasted_iota(jnp.int32, (t, t), 1)
    upper = jnp.where(r_i <= c_i, 1.0, 0.0).astype(BF16)
    tot = _dot3(_split3(val), upper) + carry_ref[:, 0:1]
    out = jnp.where(is_input_gate, z, tot)
    grow_ref[...] = out
    carry_ref[...] = jnp.broadcast_to(tot[:, t - 1:t], carry_ref.shape)
    eye = jnp.where(r_i == c_i, 1.0, 0.0).astype(BF16)
    p0, p1, p2 = _split3(out)
    gcol_ref[...] = _dot_nt(eye, p0) + _dot_nt(eye, p1) + _dot_nt(eye, p2)
    n0, n1, n2 = _split3(out * (-LOG2E))
    zrows = _dot(sel_ref[0], n0) + _dot(sel_ref[1], n1) + _dot(sel_ref[2], n2)
    caug_ref[...] = _dot_nt(eye, zrows.astype(BF16)).astype(BF16)


def _bias_lane_selectors():
    sel = [[[0.0] * N_GATE_ROWS for _ in range(FX_HEADS // 2 * LANES)] for _ in range(3)]
    for j in range(3):
        for p in range(FX_HEADS // 2):
            for a in range(2):
                sel[j][LANES * p + 3 * a + j][2 * ML_HEADS + 2 * p + a] = 1.0
    return jnp.asarray(sel, BF16)


def gates(g_row, bias, batch):
    r, n = g_row.shape
    s = n // batch
    t = min(T_GATE, s)
    nb = s // t
    sel = _bias_lane_selectors()
    wc = sel.shape[1]
    return pl.pallas_call(
        _gates_kernel,
        out_shape=(jax.ShapeDtypeStruct((r, n), F32), jax.ShapeDtypeStruct((n, r), F32),
                   jax.ShapeDtypeStruct((n, wc), BF16)),
        grid=(batch, nb),
        in_specs=[pl.BlockSpec((r, t), lambda b, j: (0, b * nb + j)),
                  pl.BlockSpec((r, 1), lambda b, j: (0, 0)),
                  pl.BlockSpec(sel.shape, lambda b, j: (0, 0, 0))],
        out_specs=(pl.BlockSpec((r, t), lambda b, j: (0, b * nb + j)),
                   pl.BlockSpec((t, r), lambda b, j: (b * nb + j, 0)),
                   pl.BlockSpec((t, wc), lambda b, j: (b * nb + j, 0))),
        scratch_shapes=[pltpu.VMEM((r, LANES), F32)],
        compiler_params=_cparams(("arbitrary", "arbitrary")),
        name="gates",
    )(g_row, bias, sel)


def _hgrn_kernel(q_ref, f_ref, i_ref, g_ref, lb_ref, gain_ref, m_ref, o_ref, st_ref, x_ref, y_ref, *, layer):
    t = q_ref.shape[0]
    sub = SUB_HG
    nsub = t // sub

    @pl.when(pl.program_id(1) == 0)
    def _():
        st_ref[...] = jnp.zeros_like(st_ref)

    lbp = lb_ref[...]
    rows = [lbp[r:r + 1, :] for r in range(lbp.shape[0])]
    mx = functools.reduce(jnp.maximum, rows)
    es = [jnp.exp(r - mx) for r in rows]
    tot = functools.reduce(lambda a, b: a + b, es)
    cs, run = [], None
    for e in es:
        run = e / tot if run is None else run + e / tot
        cs.append(run)
    lb = cs[layer] - cs[0]

    z = f_ref[...].astype(F32)
    a = jnp.log(lb)
    bb = jnp.log1p(-lb) + _log_sigmoid(z)
    log_f = jnp.maximum(a, bb) + jnp.log1p(jnp.exp(-jnp.abs(a - bb)))
    k = (1.0 - lb) * _sigmoid(-z)
    q = q_ref[...].astype(F32)
    v = i_ref[...].astype(F32)
    m_bf = m_ref[...]

    r_i = lax.broadcasted_iota(jnp.int32, (t, t), 0)
    c_i = lax.broadcasted_iota(jnp.int32, (t, t), 1)
    same_sub = (c_i // sub) == (r_i // sub)
    lower = jnp.where(jnp.logical_and(c_i <= r_i, same_sub), 1.0, 0.0).astype(BF16)
    f0, f1, f2 = _split3(log_f)
    b = _dot(lower, f0) + _dot(lower, f1) + _dot(lower, f2)
    b2 = b * LOG2E
    m_f32 = m_bf.astype(F32)
    t_in_sub = lax.broadcasted_iota(jnp.int32, (sub, q.shape[1]), 0)

    st = st_ref[...]
    o_inter = []
    for i in range(nsub):
        rows = slice(i * sub, (i + 1) * sub)
        bi, b2i, qi, ki = b[rows], b2[rows], q[rows], k[rows]
        for s in range(sub):
            diff = b2i - b2i[s:s + 1, :]
            if s > 0:
                diff = jnp.where(t_in_sub >= s, diff, NEG_INF)
            base = (i * sub + s) * sub
            x_ref[base:base + sub, :] = (qi * (ki[s:s + 1, :] * jnp.exp2(diff))).astype(BF16)
        o_inter.append(_dot_nt((qi * jnp.exp(bi)).astype(BF16), st.astype(BF16)))
        b_end = bi[sub - 1:sub, :]
        kd = ki * jnp.exp(b_end - bi)
        st = st * jnp.exp(b_end) + _dot_tn(v[rows].astype(BF16), kd.astype(BF16)) * m_f32
    st_ref[...] = st
    y_ref[...] = _dot(x_ref[...], m_bf)
    outs = []
    for i in range(nsub):
        vi = v[i * sub:(i + 1) * sub]
        acc = o_inter[i]
        for s in range(sub):
            base = (i * sub + s) * sub
            acc = acc + y_ref[base:base + sub, :] * vi[s:s + 1, :]
        outs.append(acc)
    o = jnp.concatenate(outs, axis=0)

    gt = g_ref[...].astype(F32)
    y = o * lax.rsqrt(_head_mean_sq(o, m_bf) + EPS) * gain_ref[...] * (gt * _sigmoid(gt))
    o_ref[...] = y.astype(o_ref.dtype)


def hgrn2(main, lb_all, gain, m_bf, batch, layer):
    n = main.shape[0]
    s = n // batch
    t = T_HG
    nc = s // t
    w = HG_W
    n_pairs = t * SUB_HG
    col = lambda cidx: pl.BlockSpec((t, w), lambda b, c: (b * nc + c, cidx))
    full = lambda shape: pl.BlockSpec(shape, lambda b, c: (0, 0))
    return pl.pallas_call(
        functools.partial(_hgrn_kernel, layer=layer),
        out_shape=jax.ShapeDtypeStruct((n, w), BF16),
        grid=(batch, nc),
        in_specs=[col(0), col(1), col(2), col(3), full(lb_all.shape), full((1, w)), full((w, w))],
        out_specs=pl.BlockSpec((t, w), lambda b, c: (b * nc + c, 0)),
        scratch_shapes=[pltpu.VMEM((w, w), F32), pltpu.VMEM((n_pairs, w), BF16), pltpu.VMEM((n_pairs, w), F32)],
        compiler_params=_cparams(("arbitrary", "arbitrary")),
        name="hgrn2",
    )(main, main, main, main, lb_all, gain, m_bf)


def _mlstm_kernel(q_ref, k_ref, v_ref, og_ref, grow_ref, gcol_ref, cw_ref, gain_ref, m_ref, o_ref,
                  ext_ref, ct_ref, n_ref, mm_ref):
    t = q_ref.shape[0]
    w = q_ref.shape[1]
    halo = 8

    @pl.when(pl.program_id(1) == 0)
    def _():
        ext_ref[0:halo, :] = jnp.zeros((halo, 2 * w), F32)
        ct_ref[...] = jnp.zeros_like(ct_ref)
        n_ref[...] = jnp.zeros_like(n_ref)
        mm_ref[...] = jnp.zeros_like(mm_ref)

    ext_ref[halo:halo + t, 0:w] = q_ref[...].astype(F32)
    ext_ref[halo:halo + t, w:2 * w] = k_ref[...].astype(F32)
    cw = cw_ref[...]
    y = None
    for j in range(CONV_K):
        term = ext_ref[halo - (CONV_K - 1) + j:halo - (CONV_K - 1) + j + t, :] * cw[j:j + 1, :]
        y = term if y is None else y + term
    tail = ext_ref[t:t + halo, :]
    ext_ref[0:halo, :] = tail
    qk = y * _sigmoid(y)
    q = qk[:, 0:w]
    k = qk[:, w:2 * w] * (HEAD_DIM ** -0.5)
    kb = k.astype(BF16)
    vb = v_ref[...]
    m_bf = m_ref[...]

    grow = grow_ref[...]
    gcol = gcol_ref[...]
    lane_head = lax.broadcasted_iota(jnp.int32, (1, w), 1) // HEAD_DIM
    r_i = lax.broadcasted_iota(jnp.int32, (t, t), 0)
    c_i = lax.broadcasted_iota(jnp.int32, (t, t), 1)
    causal = c_i <= r_i

    num_intra = jnp.zeros((t, w), F32)
    sint_l = jnp.zeros((t, w), F32)
    wsum_l = jnp.zeros((t, w), F32)
    mt_l = jnp.zeros((t, w), F32)
    wk_l = jnp.zeros((t, w), F32)
    decay_l = jnp.zeros((1, w), F32)
    for h in range(ML_HEADS):
        sel = lane_head == h
        qh = jnp.where(sel, q, 0.0).astype(BF16)
        s = _dot_nt(qh, kb)
        bc = gcol[:, ML_HEADS + h:ML_HEADS + h + 1]
        br = grow[ML_HEADS + h:ML_HEADS + h + 1, :]
        lir = grow[h:h + 1, :]
        lic = gcol[:, h:h + 1]
        dlog = jnp.where(causal, bc - br + lir, NEG_INF)
        mmh = mm_ref[h:h + 1, 0:1]
        inter = bc + mmh
        m_t = jnp.maximum(jnp.max(dlog, axis=1, keepdims=True), inter)
        wgt = s * jnp.exp(dlog - m_t)
        s_int = jnp.exp(inter - m_t)
        pv = _dot(wgt.astype(BF16), vb)
        num_intra = jnp.where(sel, pv, num_intra)
        sint_l = jnp.where(sel, s_int, sint_l)
        wsum_l = jnp.where(sel, jnp.sum(wgt, axis=1, keepdims=True), wsum_l)
        mt_l = jnp.where(sel, m_t, mt_l)
        b_end = br[:, t - 1:t]
        m_new = jnp.maximum(b_end + mmh, jnp.max(b_end - br + lir, axis=1, keepdims=True))
        wk_l = jnp.where(sel, jnp.exp(b_end - bc + lic - m_new), wk_l)
        decay_l = jnp.where(sel, jnp.exp(b_end + mmh - m_new), decay_l)
        mm_ref[h:h + 1, :] = jnp.broadcast_to(m_new - b_end, (1, mm_ref.shape[1]))

    ct = ct_ref[...]
    nrow = n_ref[0:1, :]
    q_c = _dot_nt(q.astype(BF16), ct.astype(BF16))
    qn = q * nrow
    qn_hi = qn.astype(BF16)
    qn_lo = (qn - qn_hi.astype(F32)).astype(BF16)
    qn_l = _dot(qn_hi, m_bf) + _dot(qn_lo, m_bf)
    num = num_intra + sint_l * q_c
    den = wsum_l + sint_l * qn_l
    hval = num / jnp.maximum(jnp.abs(den), jnp.exp(-mt_l))

    kw = k * wk_l
    upd = _dot_tn(vb, kw.astype(BF16))
    ct_ref[...] = decay_l * ct + upd * m_bf.astype(F32)
    n_ref[...] = jnp.broadcast_to(decay_l * nrow + jnp.sum(kw, axis=0, keepdims=True), n_ref.shape)

    og = og_ref[...].astype(F32)
    yv = hval * lax.rsqrt(_head_mean_sq(hval, m_bf) + EPS) * gain_ref[...] * _sigmoid(og)
    o_ref[...] = yv.astype(o_ref.dtype)


def mlstm(main, g_row, g_col, conv_w, gain, m_bf, batch):
    n = main.shape[0]
    s = n // batch
    t = min(T_ML, s)
    nc = s // t
    w = ML_W
    col = lambda cidx: pl.BlockSpec((t, w), lambda b, c: (b * nc + c, cidx))
    full = lambda shape: pl.BlockSpec(shape, lambda b, c: (0, 0))
    return pl.pallas_call(
        _mlstm_kernel,
        out_shape=jax.ShapeDtypeStruct((n, w), BF16),
        grid=(batch, nc),
        in_specs=[col(4), col(5), col(6), col(7),
                  pl.BlockSpec((N_GATE_ROWS, t), lambda b, c: (0, b * nc + c)),
                  pl.BlockSpec((t, N_GATE_ROWS), lambda b, c: (b * nc + c, 0)),
                  full((CONV_K, 2 * w)), full((1, w)), full((w, w))],
        out_specs=pl.BlockSpec((t, w), lambda b, c: (b * nc + c, 0)),
        scratch_shapes=[pltpu.VMEM((t + 8, 2 * w), F32), pltpu.VMEM((w, w), F32),
                        pltpu.VMEM((8, w), F32), pltpu.VMEM((8, LANES), F32)],
        compiler_params=_cparams(("arbitrary", "arbitrary")),
        name="mlstm",
    )(main, main, main, main, g_row, g_col, conv_w, gain, m_bf)


def _fox_kernel(q_ref, k_ref, v_ref, c_ref, o_ref, m_ref, acc_ref):
    tq = q_ref.shape[0]
    half = tq // 2
    qi = pl.program_id(2)
    lane = lax.broadcasted_iota(jnp.int32, (1, LANES), 1)
    first = lane < HEAD_DIM

    m_ref[...] = jnp.full(m_ref.shape, NEG_INF, F32)
    acc_ref[...] = jnp.zeros_like(acc_ref)

    def attend(r0, r1, kstart, klen, triangular):
        q2 = q_ref[r0:r1, :]
        v2 = v_ref[pl.ds(kstart, klen), :]
        k_aug = jnp.concatenate([k_ref[pl.ds(kstart, klen), :], c_ref[pl.ds(kstart, klen), :]], axis=1)
        for a in range(2):
            sel = first if a == 0 else jnp.logical_not(first)
            ones_lanes = jnp.logical_and(lane >= 3 * a, lane < 3 * a + 3)
            q_bias = jnp.broadcast_to(jnp.where(ones_lanes, 1.0, 0.0).astype(q2.dtype), q2.shape)
            q_aug = jnp.concatenate([jnp.where(sel, q2, jnp.zeros_like(q2)), q_bias], axis=1)
            s = _dot_nt(q_aug, k_aug)
            if triangular:
                r_i = lax.broadcasted_iota(jnp.int32, s.shape, 0)
                c_i = lax.broadcasted_iota(jnp.int32, s.shape, 1)
                s = jnp.where(c_i <= r_i, s, NEG_INF)
            m_prev = m_ref[a, r0:r1, :]
            m_new = jnp.maximum(m_prev, jnp.max(s, axis=1, keepdims=True))
            p = jnp.concatenate([jnp.exp2(s[:, c * LANES:(c + 1) * LANES] - m_new).astype(v2.dtype)
                                 for c in range(klen // LANES)], axis=1)
            v_aug = jnp.where(sel, v2, jnp.ones_like(v2))
            acc_ref[a, r0:r1, :] = jnp.exp2(m_prev - m_new) * acc_ref[a, r0:r1, :] + _dot(p, v_aug)
            m_ref[a, r0:r1, :] = m_new

    def past_block(j, carry):
        attend(0, tq, pl.multiple_of(j * tq, tq), tq, False)
        return carry

    lax.fori_loop(0, qi, past_block, 0)
    d0 = pl.multiple_of(qi * tq, tq)
    attend(0, half, d0, half, True)
    attend(half, tq, d0, half, False)
    attend(half, tq, d0 + half, half, True)

    acc_a = acc_ref[0]
    acc_b = acc_ref[1]
    out = jnp.where(first, acc_a / pltpu.roll(acc_a, HEAD_DIM, 1), acc_b / pltpu.roll(acc_b, HEAD_DIM, 1))
    o_ref[...] = out.astype(o_ref.dtype)


def fox(main, c_aug, batch):
    n = main.shape[0]
    s = n // batch
    tq = min(TQ_FX, s)
    nq = s // tq
    pairs = FX_HEADS // 2
    qcol, kcol, vcol = 2048 // LANES, 2560 // LANES, 3072 // LANES
    seq_blk = lambda col0: pl.BlockSpec((s, LANES), lambda b, p, i: (b, col0 + p))
    return pl.pallas_call(
        _fox_kernel,
        out_shape=jax.ShapeDtypeStruct((n, FX_W), BF16),
        grid=(batch, pairs, nq),
        in_specs=[pl.BlockSpec((tq, LANES), lambda b, p, i: (b * nq + i, qcol + p)),
                  seq_blk(kcol), seq_blk(vcol), seq_blk(0)],
        out_specs=pl.BlockSpec((tq, LANES), lambda b, p, i: (b * nq + i, p)),
        scratch_shapes=[pltpu.VMEM((2, tq, LANES), F32), pltpu.VMEM((2, tq, LANES), F32)],
        compiler_params=_cparams(("arbitrary", "arbitrary", "arbitrary")),
        name="fox",
    )(main, main, main, c_aug)


def _outproj_body(x_ref, ohg_ref, oml_ref, ofx_ref, w_ref, g_ref):
    acc = x_ref[...]
    acc = acc + _dot(ohg_ref[...], w_ref[0:HG_W, :])
    acc = acc + _dot(oml_ref[...], w_ref[HG_W:HG_W + ML_W, :])
    acc = acc + _dot(ofx_ref[...], w_ref[HG_W + ML_W:, :])
    ms = jnp.mean(acc * acc, axis=-1, keepdims=True)
    h = (acc * lax.rsqrt(ms + EPS)) * g_ref[...]
    return acc, h


def _outproj_dense_kernel(x_ref, ohg_ref, oml_ref, ofx_ref, w_ref, g_ref, xo_ref, h_ref):
    acc, h = _outproj_body(x_ref, ohg_ref, oml_ref, ofx_ref, w_ref, g_ref)
    xo_ref[...] = acc
    h_ref[...] = h.astype(h_ref.dtype)


def _outproj_moe_kernel(x_ref, ohg_ref, oml_ref, ofx_ref, w_ref, g_ref, rhi_ref, rlo_ref,
                        xo_ref, h_ref, idx_ref, gate_ref):
    acc, h = _outproj_body(x_ref, ohg_ref, oml_ref, ofx_ref, w_ref, g_ref)
    xo_ref[...] = acc
    h_ref[...] = h
    h_hi = h.astype(BF16)
    h_lo = (h - h_hi.astype(F32)).astype(BF16)
    logits = _dot(h_hi, rhi_ref[...]) + _dot(h_hi, rlo_ref[...]) + _dot(h_lo, rhi_ref[...])
    lane_i = lax.broadcasted_iota(jnp.int32, logits.shape, 1)
    lane = lane_i.astype(F32)
    lg = jnp.where(lane_i < N_EXPERTS, logits, NEG_INF)
    m1 = jnp.max(lg, axis=1, keepdims=True)
    i1 = jnp.min(jnp.where(lg == m1, lane, float(LANES)), axis=1, keepdims=True)
    lg2 = jnp.where(lane == i1, NEG_INF, lg)
    m2 = jnp.max(lg2, axis=1, keepdims=True)
    i2 = jnp.min(jnp.where(lg2 == m2, lane, float(LANES)), axis=1, keepdims=True)
    e = jnp.exp(m2 - m1)
    g1 = 1.0 / (1.0 + e)
    g2 = e / (1.0 + e)
    idx_ref[...] = jnp.where(lane_i == 0, i1, jnp.where(lane_i == 1, i2, 0.0)).astype(jnp.int32)
    gate_ref[...] = jnp.where(lane_i == 0, g1, jnp.where(lane_i == 1, g2, 0.0))


def outproj(x, o_hg, o_ml, o_fx, w_out, g, router_parts=None):
    n, d = x.shape
    tm = min(TM_PROJ, n)
    row = lambda width: pl.BlockSpec((tm, width), lambda i: (i, 0))
    full = lambda shape: pl.BlockSpec(shape, lambda i: (0, 0))
    in_specs = [row(d), row(HG_W), row(ML_W), row(FX_W), full(w_out.shape), full((1, d))]
    args = [x, o_hg, o_ml, o_fx, w_out, g]
    if router_parts is None:
        kern = _outproj_dense_kernel
        out_shape = (jax.ShapeDtypeStruct((n, d), F32), jax.ShapeDtypeStruct((n, d), BF16))
        out_specs = (row(d), row(d))
    else:
        kern = _outproj_moe_kernel
        in_specs += [full(router_parts[0].shape), full(router_parts[1].shape)]
        args += list(router_parts)
        out_shape = (jax.ShapeDtypeStruct((n, d), F32), jax.ShapeDtypeStruct((n, d), F32),
                     jax.ShapeDtypeStruct((n, LANES), jnp.int32), jax.ShapeDtypeStruct((n, LANES), F32))
        out_specs = (row(d), row(d), row(LANES), row(LANES))
    return pl.pallas_call(
        kern, out_shape=out_shape, grid=(n // tm,), in_specs=in_specs, out_specs=out_specs,
        compiler_params=_cparams(("arbitrary",)), name="outproj",
    )(*args)


def _ffn_kernel(h_ref, x_ref, wg_ref, wu_ref, wd_ref, o_ref):
    @pl.when(pl.program_id(1) == 0)
    def _():
        o_ref[...] = x_ref[...]

    h = h_ref[...]
    gt = _dot(h, wg_ref[...])
    up = _dot(h, wu_ref[...])
    hid = (gt * _sigmoid(gt) * up).astype(BF16)
    o_ref[...] += _dot(hid, wd_ref[...])


def dense_ffn(h, x, wg, wu, wd):
    n, d = x.shape
    ff = wg.shape[1]
    tm = min(TM_FFN, n)
    tf = ff // 2 if (ff // 2) % LANES == 0 else ff
    return pl.pallas_call(
        _ffn_kernel,
        out_shape=jax.ShapeDtypeStruct((n, d), F32),
        grid=(n // tm, ff // tf),
        in_specs=[pl.BlockSpec((tm, d), lambda i, f: (i, 0)),
                  pl.BlockSpec((tm, d), lambda i, f: (i, 0)),
                  pl.BlockSpec((d, tf), lambda i, f: (0, f)),
                  pl.BlockSpec((d, tf), lambda i, f: (0, f)),
                  pl.BlockSpec((tf, d), lambda i, f: (f, 0))],
        out_specs=pl.BlockSpec((tm, d), lambda i, f: (i, 0)),
        compiler_params=_cparams(("arbitrary", "arbitrary")),
        name="dense_ffn",
    )(h, x, wg, wu, wd)


def _row_copy(src_ref, src_row, dst_ref, dst_row, sem):
    return pltpu.make_async_copy(src_ref.at[pl.ds(src_row, 1)], dst_ref.at[pl.ds(dst_row, 1)], sem)


def _invert_kernel(dest_ref, lo_ref, hi_ref, src_ref, dst_ref, *, n_tok, bm):
    def fill(i, carry):
        src_ref[i] = 0
        dst_ref[i] = TOP_K * n_tok + (i & (2 * bm - 1))
        return carry

    for r in range(lo_ref.shape[0]):
        lax.fori_loop(lo_ref[r], hi_ref[r], fill, 0)

    def put(tok, carry):
        for slot in range(TOP_K):
            row = dest_ref[TOP_K * tok + slot]
            src_ref[row] = tok
            dst_ref[row] = slot * n_tok + tok
        return carry

    lax.fori_loop(0, n_tok, put, 0, unroll=8)


def moe_invert(dest, pad_lo, pad_hi, p_rows, n_tok, bm):
    smem = pl.BlockSpec(memory_space=pltpu.SMEM)
    rows = jax.ShapeDtypeStruct((p_rows,), jnp.int32)
    return pl.pallas_call(
        functools.partial(_invert_kernel, n_tok=n_tok, bm=bm),
        out_shape=(rows, rows),
        in_specs=[smem, smem, smem],
        out_specs=(smem, smem),
        name="moe_invert",
    )(dest, pad_lo, pad_hi)


def _experts_kernel(src_ref, dst_ref, blk_e_ref, nused_ref, h_ref, wg_ref, wu_ref, wd_ref, y_ref,
                    xbuf_ref, xb_ref, acc_ref, obuf_ref, gsem, ssem, *, n_tok, chunk):
    del blk_e_ref
    bm = xb_ref.shape[0]
    tf = wg_ref.shape[2]
    nchunk = tf // chunk
    m = pl.program_id(0)
    f = pl.program_id(1)
    nb = pl.num_programs(0)
    nf = pl.num_programs(1)
    slot = m % 2
    other = 1 - slot
    n_used = nused_ref[0]
    live = m < n_used

    def gather_copy(block, s, i):
        return _row_copy(h_ref, src_ref[block * bm + i], xbuf_ref.at[s], i, gsem.at[s])

    def scatter_copy(block, s, i, to_spare):
        row = jnp.where(to_spare, TOP_K * n_tok + s * bm + i, dst_ref[block * bm + i])
        return _row_copy(obuf_ref.at[s], i, y_ref, row, ssem.at[s])

    def for_rows(fn):
        def body(i, carry):
            fn(i)
            return carry
        lax.fori_loop(0, bm, body, 0, unroll=8)

    def wait_gather(s):
        for_rows(lambda i: _row_copy(h_ref, 0, xbuf_ref.at[s], i, gsem.at[s]).wait())

    def wait_scatter(s):
        for_rows(lambda i: _row_copy(obuf_ref.at[s], i, y_ref, 0, ssem.at[s]).wait())

    def swiglu(first, issue):
        xb = xb_ref[...]
        for c in range(nchunk):
            cols = slice(c * chunk, (c + 1) * chunk)
            gt = _dot(xb, wg_ref[0, :, cols])
            up = _dot(xb, wu_ref[0, :, cols])
            hid = (gt * _sigmoid(gt) * up).astype(BF16)
            part = _dot(hid, wd_ref[0, cols, :])
            if first and c == 0:
                acc_ref[...] = part
            else:
                acc_ref[...] += part
            if issue is not None:
                for i in range(c * bm // nchunk, (c + 1) * bm // nchunk):
                    issue(i)

    @pl.when(jnp.logical_and(f == 0, live))
    def _():
        @pl.when(m == 0)
        def _():
            obuf_ref[...] = jnp.zeros_like(obuf_ref)
            spare0 = pltpu.make_async_copy(obuf_ref.at[0], y_ref.at[pl.ds(TOP_K * n_tok, bm)], ssem.at[0])
            spare0.start()
            spare0.wait()
            for_rows(lambda i: gather_copy(0, 0, i).start())

        wait_gather(slot)
        xb_ref[...] = xbuf_ref[slot].astype(BF16)
        prev = jnp.maximum(m - 1, 0)
        swiglu(True, lambda i: scatter_copy(prev, other, i, m == 0).start())

    @pl.when(jnp.logical_and(jnp.logical_and(f > 0, f < nf - 1), live))
    def _():
        swiglu(False, None)

    @pl.when(jnp.logical_and(f == nf - 1, live))
    def _():
        nxt = jnp.minimum(m + 1, nb - 1)
        swiglu(False, lambda i: gather_copy(nxt, other, i).start())

        @pl.when(m >= 1)
        def _():
            wait_scatter(slot)

        obuf_ref[slot] = acc_ref[...]

        @pl.when(m == n_used - 1)
        def _():
            wait_gather(other)
            for_rows(lambda i: scatter_copy(m, slot, i, False).start())
            wait_scatter(slot)
            wait_scatter(other)


def moe_experts(src_rows, dst_rows, blk_expert, n_used, h, wg, wu, wd):
    n_tok, d = h.shape
    p_rows = src_rows.shape[0]
    ff = wg.shape[2]
    bm = BM_MOE
    tf = TF_MOE
    nf = ff // tf
    assert nf >= 2, "the first and last hidden tile of a block carry different row copies"

    def w_idx(m, f, be, nu):
        f_eff = jnp.where(m % 2 == 0, f, nf - 1 - f)
        last = jnp.where((nu[0] - 1) % 2 == 0, nf - 1, 0)
        return be[m], jnp.where(m < nu[0], f_eff, last)

    return pl.pallas_call(
        functools.partial(_experts_kernel, n_tok=n_tok, chunk=CHUNK_MOE),
        out_shape=jax.ShapeDtypeStruct((TOP_K * n_tok + 2 * bm, d), F32),
        grid_spec=pltpu.PrefetchScalarGridSpec(
            num_scalar_prefetch=4, grid=(p_rows // bm, nf),
            in_specs=[pl.BlockSpec(memory_space=pl.ANY),
                      pl.BlockSpec((1, d, tf), lambda m, f, sr, ds, be, nu: (w_idx(m, f, be, nu)[0], 0, w_idx(m, f, be, nu)[1])),
                      pl.BlockSpec((1, d, tf), lambda m, f, sr, ds, be, nu: (w_idx(m, f, be, nu)[0], 0, w_idx(m, f, be, nu)[1])),
                      pl.BlockSpec((1, tf, d), lambda m, f, sr, ds, be, nu: (w_idx(m, f, be, nu)[0], w_idx(m, f, be, nu)[1], 0))],
            out_specs=pl.BlockSpec(memory_space=pl.ANY),
            scratch_shapes=[pltpu.VMEM((2, bm, d), F32), pltpu.VMEM((bm, d), BF16), pltpu.VMEM((bm, d), F32),
                            pltpu.VMEM((2, bm, d), F32), pltpu.SemaphoreType.DMA((2,)),
                            pltpu.SemaphoreType.DMA((2,))]),
        compiler_params=_cparams(("arbitrary", "arbitrary"), VMEM_LIMIT_BIG),
        name="moe_experts",
    )(src_rows, dst_rows, blk_expert, n_used, h, wg, wu, wd)


def _combine_kernel(x_ref, gate_ref, fg_ref, y0_ref, y1_ref, o_ref, *, final):
    gate = gate_ref[...]
    y = x_ref[...] + gate[:, 0:1] * y0_ref[...] + gate[:, 1:2] * y1_ref[...]
    if final:
        ms = jnp.mean(y * y, axis=-1, keepdims=True)
        y = (y * lax.rsqrt(ms + EPS)) * fg_ref[...]
    o_ref[...] = y


def moe_combine(x, gate, final_g, y, final):
    n, d = x.shape
    tb = min(TM_PROJ, n)
    return pl.pallas_call(
        functools.partial(_combine_kernel, final=final),
        out_shape=jax.ShapeDtypeStruct((n, d), F32),
        grid=(n // tb,),
        in_specs=[pl.BlockSpec((tb, d), lambda i: (i, 0)),
                  pl.BlockSpec((tb, LANES), lambda i: (i, 0)),
                  pl.BlockSpec((1, d), lambda i: (0, 0)),
                  pl.BlockSpec((tb, d), lambda i: (i, 0)),
                  pl.BlockSpec((tb, d), lambda i: (i + n // tb, 0))],
        out_specs=pl.BlockSpec((tb, d), lambda i: (i, 0)),
        compiler_params=_cparams(("arbitrary",)),
        name="moe_combine",
    )(x, gate, final_g, y, y)


def _final_norm_kernel(x_ref, g_ref, o_ref):
    x = x_ref[...]
    ms = jnp.mean(x * x, axis=-1, keepdims=True)
    o_ref[...] = (x * lax.rsqrt(ms + EPS)) * g_ref[...]


def final_norm(x, g):
    n, d = x.shape
    tm = min(TM_PROJ, n)
    return pl.pallas_call(
        _final_norm_kernel,
        out_shape=jax.ShapeDtypeStruct((n, d), F32),
        grid=(n // tm,),
        in_specs=[pl.BlockSpec((tm, d), lambda i: (i, 0)), pl.BlockSpec((1, d), lambda i: (0, 0))],
        out_specs=pl.BlockSpec((tm, d), lambda i: (i, 0)),
        compiler_params=_cparams(("arbitrary",)),
        name="final_norm",
    )(x, g)


def _moe_plan(top_idx, bm):
    n = top_idx.shape[0]
    a = n * TOP_K
    e_flat = top_idx.reshape(a)
    onehot = (e_flat[:, None] == jnp.arange(N_EXPERTS, dtype=jnp.int32)[None, :]).astype(jnp.int32)
    incl = jnp.cumsum(onehot, axis=0)
    rank = jnp.sum((incl - onehot) * onehot, axis=1)
    counts = incl[-1]
    padded = (counts + bm - 1) // bm * bm
    pad_end = jnp.cumsum(padded)
    pad_start = pad_end - padded
    dest = (jnp.sum(pad_start[None, :] * onehot, axis=1) + rank).astype(jnp.int32)
    p_rows = a + N_EXPERTS * bm
    n_blocks = p_rows // bm
    n_used = (pad_end[-1] // bm).astype(jnp.int32)
    blk_start = jnp.arange(n_blocks, dtype=jnp.int32) * bm
    blk_expert = jnp.sum((blk_start[:, None] >= pad_end[None, :]).astype(jnp.int32), axis=1)
    blk_expert = jnp.minimum(blk_expert, N_EXPERTS - 1)
    last_e = blk_expert[jnp.maximum(n_used - 1, 0)]
    blk_expert = jnp.where(jnp.arange(n_blocks) < n_used, blk_expert, last_e).astype(jnp.int32)
    pad_lo = jnp.concatenate([pad_start + counts, pad_end[-1:]]).astype(jnp.int32)
    pad_hi = jnp.concatenate([pad_end, jnp.full((1,), p_rows, jnp.int32)]).astype(jnp.int32)
    return dest, blk_expert, n_used.reshape(1), pad_lo, pad_hi, p_rows


def _head_block_mask(width):
    r = jnp.arange(width) // HEAD_DIM
    return (r[:, None] == r[None, :]).astype(BF16)


def kernel(x, attn_norm, w_in, hgrn_lb, hgrn_norm, mlstm_conv, mlstm_b_i, mlstm_b_f, mlstm_norm, fox_b_f, w_out, ffn_norm, dense_w_gate, dense_w_up, dense_w_down, router, moe_w_gate, moe_w_up, moe_w_down, final_norm_g):
    batch, seq, d = x.shape
    depth = w_in.shape[0]
    n = batch * seq
    xf = x.reshape(n, d)
    m_bf = _head_block_mask(HG_W)
    n_main_a = 4 * HG_W + 3 * ML_W
    gate_a = n_main_a
    mo_a = gate_a + 2 * ML_HEADS
    fx_a = mo_a + ML_W
    ff_a = fx_a + 3 * FX_W
    done = False
    for l in range(depth):
        wl = w_in[l]
        wq_fx = wl[:, fx_a:fx_a + FX_W] * (LOG2E * HEAD_DIM ** -0.5)
        w_main = jnp.concatenate([wl[:, :n_main_a], wl[:, mo_a:fx_a], wq_fx, wl[:, fx_a + FX_W:ff_a]],
                                 axis=1).astype(BF16)
        w_gate_t = jnp.concatenate([wl[:, gate_a:mo_a], wl[:, ff_a:]], axis=1).T.astype(BF16)
        main, g_row_raw = norm_inproj(xf, attn_norm[l].reshape(1, d), w_main, w_gate_t)
        bias = jnp.concatenate([mlstm_b_i[l], mlstm_b_f[l], fox_b_f[l]]).reshape(N_GATE_ROWS, 1).astype(F32)
        g_row, g_col, c_aug = gates(g_row_raw, bias, batch)
        o_hg = hgrn2(main, hgrn_lb.astype(F32), hgrn_norm[l].reshape(1, HG_W), m_bf, batch, l)
        o_ml = mlstm(main, g_row, g_col, mlstm_conv[l], mlstm_norm[l].reshape(1, ML_W), m_bf, batch)
        o_fx = fox(main, c_aug, batch)
        wo = w_out[l].astype(BF16)
        fg = ffn_norm[l].reshape(1, d)
        j = l // 2
        if l % 2 == 0:
            x_res, h = outproj(xf, o_hg, o_ml, o_fx, wo, fg)
            xf = dense_ffn(h, x_res, dense_w_gate[j].astype(BF16), dense_w_up[j].astype(BF16),
                           dense_w_down[j].astype(BF16))
        else:
            r = jnp.pad(router[j], ((0, 0), (0, LANES - N_EXPERTS)))
            r_hi = r.astype(BF16)
            r_lo = (r - r_hi.astype(F32)).astype(BF16)
            x_res, h, idx, gate = outproj(xf, o_hg, o_ml, o_fx, wo, fg, (r_hi, r_lo))
            dest, blk_expert, n_used, pad_lo, pad_hi, p_rows = _moe_plan(idx[:, :TOP_K], BM_MOE)
            src_rows, dst_rows = moe_invert(dest, pad_lo, pad_hi, p_rows, n, BM_MOE)
            y = moe_experts(src_rows, dst_rows, blk_expert, n_used, h, moe_w_gate[j].astype(BF16),
                            moe_w_up[j].astype(BF16), moe_w_down[j].astype(BF16))
            done = l == depth - 1
            xf = moe_combine(x_res, gate, final_norm_g.reshape(1, d), y, done)
    if not done:
        xf = final_norm(xf, final_norm_g.reshape(1, d))
    return xf.reshape(batch, seq, d)
```

```python
import functools

import jax
import jax.numpy as jnp
from jax import lax
from jax.experimental import pallas as pl
from jax.experimental.pallas import tpu as pltpu

F32 = jnp.float32
BF16 = jnp.bfloat16
EPS = 1e-6
NEG_INF = float("-inf")
LOG2E = 1.4426950408889634

HEAD_DIM = 64
HG_W = 256
ML_W = 256
FX_W = 512
ML_HEADS = 4
FX_HEADS = 8
CONV_K = 4
N_EXPERTS = 8
TOP_K = 2
N_GATE_ROWS = 16

LANES = 128
VMEM_LIMIT = 48 * 1024 * 1024
VMEM_LIMIT_BIG = 56 * 1024 * 1024

TM_PROJ = 512
TM_INPROJ = 1024
T_GATE = 512
T_HG = 128
SUB_HG = 16
T_ML = 128
TQ_FX = 1024
TM_FFN = 512
BM_MOE = 512
TF_MOE = 1792
CHUNK_MOE = 256
ROW_DMA_PRIORITY = 1


def _cparams(sem, vmem=VMEM_LIMIT):
    return pltpu.CompilerParams(dimension_semantics=sem, vmem_limit_bytes=vmem)


def _split3(x):
    hi = x.astype(BF16)
    r = x - hi.astype(F32)
    mid = r.astype(BF16)
    lo = (r - mid.astype(F32)).astype(BF16)
    return hi, mid, lo


def _dot(a, b):
    return jnp.dot(a, b, preferred_element_type=F32)


def _dot_nt(a, b):
    return lax.dot_general(a, b, (((1,), (1,)), ((), ())), preferred_element_type=F32)


def _dot_tn(a, b):
    return lax.dot_general(a, b, (((0,), (0,)), ((), ())), preferred_element_type=F32)


def _dot3(parts, b):
    return _dot(parts[0], b) + _dot(parts[1], b) + _dot(parts[2], b)


def _log_sigmoid(z):
    return -(jnp.maximum(-z, 0.0) + jnp.log1p(jnp.exp(-jnp.abs(z))))


def _sigmoid(z):
    return 1.0 / (1.0 + jnp.exp(-z))


def _head_mean_sq(o, m_bf):
    o2 = o * o
    hi = o2.astype(BF16)
    lo = (o2 - hi.astype(F32)).astype(BF16)
    return (_dot(hi, m_bf) + _dot(lo, m_bf)) * (1.0 / HEAD_DIM)


def _norm_inproj_kernel(x_ref, g_ref, w_ref, wgt_ref, main_ref, grow_ref, *, tn):
    x = x_ref[...]
    ms = jnp.mean(x * x, axis=-1, keepdims=True)
    h = ((x * lax.rsqrt(ms + EPS)) * g_ref[...]).astype(BF16)
    for j in range(w_ref.shape[1] // tn):
        main_ref[:, j * tn:(j + 1) * tn] = _dot(h, w_ref[:, j * tn:(j + 1) * tn]).astype(BF16)
    grow_ref[...] = _dot_nt(wgt_ref[...], h)


def norm_inproj(x, g, w_main, w_gate_t):
    n, d = x.shape
    wm = w_main.shape[1]
    tm = min(TM_INPROJ, n)
    return pl.pallas_call(
        functools.partial(_norm_inproj_kernel, tn=512),
        out_shape=(jax.ShapeDtypeStruct((n, wm), BF16), jax.ShapeDtypeStruct((N_GATE_ROWS, n), F32)),
        grid=(n // tm,),
        in_specs=[pl.BlockSpec((tm, d), lambda i: (i, 0)),
                  pl.BlockSpec((1, d), lambda i: (0, 0)),
                  pl.BlockSpec((d, wm), lambda i: (0, 0)),
                  pl.BlockSpec((N_GATE_ROWS, d), lambda i: (0, 0))],
        out_specs=(pl.BlockSpec((tm, wm), lambda i: (i, 0)),
                   pl.BlockSpec((N_GATE_ROWS, tm), lambda i: (0, i))),
        compiler_params=_cparams(("arbitrary",)),
        name="norm_inproj",
    )(x, g, w_main, w_gate_t)


def _gates_kernel(g_ref, bias_ref, sel_ref, grow_ref, gcol_ref, caug_ref, carry_ref):
    t = g_ref.shape[1]

    @pl.when(pl.program_id(1) == 0)
    def _():
        carry_ref[...] = jnp.zeros_like(carry_ref)

    z = g_ref[...] + bias_ref[...]
    row = lax.broadcasted_iota(jnp.int32, z.shape, 0)
    is_input_gate = row < ML_HEADS
    val = jnp.where(is_input_gate, 0.0, _log_sigmoid(z))
    r_i = lax.broadcasted_iota(jnp.int32, (t, t), 0)
    c_i = lax.broadcasted_iota(jnp.int32, (t, t), 1)
    upper = jnp.where(r_i <= c_i, 1.0, 0.0).astype(BF16)
    tot = _dot3(_split3(val), upper) + carry_ref[:, 0:1]
    out = jnp.where(is_input_gate, z, tot)
    grow_ref[...] = out
    carry_ref[...] = jnp.broadcast_to(tot[:, t - 1:t], carry_ref.shape)
    eye = jnp.where(r_i == c_i, 1.0, 0.0).astype(BF16)
    p0, p1, p2 = _split3(out)
    gcol_ref[...] = _dot_nt(eye, p0) + _dot_nt(eye, p1) + _dot_nt(eye, p2)
    n0, n1, n2 = _split3(out * (-LOG2E))
    zrows = _dot(sel_ref[0], n0) + _dot(sel_ref[1], n1) + _dot(sel_ref[2], n2)
    caug_ref[...] = _dot_nt(eye, zrows.astype(BF16)).astype(BF16)


def _bias_lane_selectors():
    sel = [[[0.0] * N_GATE_ROWS for _ in range(FX_HEADS // 2 * LANES)] for _ in range(3)]
    for j in range(3):
        for p in range(FX_HEADS // 2):
            for a in range(2):
                sel[j][LANES * p + 3 * a + j][2 * ML_HEADS + 2 * p + a] = 1.0
    return jnp.asarray(sel, BF16)


def gates(g_row, bias, batch):
    r, n = g_row.shape
    s = n // batch
    t = min(T_GATE, s)
    nb = s // t
    sel = _bias_lane_selectors()
    wc = sel.shape[1]
    return pl.pallas_call(
        _gates_kernel,
        out_shape=(jax.ShapeDtypeStruct((r, n), F32), jax.ShapeDtypeStruct((n, r), F32),
                   jax.ShapeDtypeStruct((n, wc), BF16)),
        grid=(batch, nb),
        in_specs=[pl.BlockSpec((r, t), lambda b, j: (0, b * nb + j)),
                  pl.BlockSpec((r, 1), lambda b, j: (0, 0)),
                  pl.BlockSpec(sel.shape, lambda b, j: (0, 0, 0))],
        out_specs=(pl.BlockSpec((r, t), lambda b, j: (0, b * nb + j)),
                   pl.BlockSpec((t, r), lambda b, j: (b * nb + j, 0)),
                   pl.BlockSpec((t, wc), lambda b, j: (b * nb + j, 0))),
        scratch_shapes=[pltpu.VMEM((r, LANES), F32)],
        compiler_params=_cparams(("arbitrary", "arbitrary")),
        name="gates",
    )(g_row, bias, sel)


def _hgrn_kernel(q_ref, f_ref, i_ref, g_ref, lb_ref, gain_ref, m_ref, o_ref, st_ref, x_ref, y_ref, *, layer):
    t = q_ref.shape[0]
    sub = SUB_HG
    nsub = t // sub

    @pl.when(pl.program_id(1) == 0)
    def _():
        st_ref[...] = jnp.zeros_like(st_ref)

    lbp = lb_ref[...]
    rows = [lbp[r:r + 1, :] for r in range(lbp.shape[0])]
    mx = functools.reduce(jnp.maximum, rows)
    es = [jnp.exp(r - mx) for r in rows]
    tot = functools.reduce(lambda a, b: a + b, es)
    cs, run = [], None
    for e in es:
        run = e / tot if run is None else run + e / tot
        cs.append(run)
    lb = cs[layer] - cs[0]

    z = f_ref[...].astype(F32)
    a = jnp.log(lb)
    bb = jnp.log1p(-lb) + _log_sigmoid(z)
    log_f = jnp.maximum(a, bb) + jnp.log1p(jnp.exp(-jnp.abs(a - bb)))
    k = (1.0 - lb) * _sigmoid(-z)
    q = q_ref[...].astype(F32)
    v = i_ref[...].astype(F32)
    m_bf = m_ref[...]

    r_i = lax.broadcasted_iota(jnp.int32, (t, t), 0)
    c_i = lax.broadcasted_iota(jnp.int32, (t, t), 1)
    same_sub = (c_i // sub) == (r_i // sub)
    lower = jnp.where(jnp.logical_and(c_i <= r_i, same_sub), 1.0, 0.0).astype(BF16)
    f0, f1, f2 = _split3(log_f)
    b = _dot(lower, f0) + _dot(lower, f1) + _dot(lower, f2)
    b2 = b * LOG2E
    m_f32 = m_bf.astype(F32)
    t_in_sub = lax.broadcasted_iota(jnp.int32, (sub, q.shape[1]), 0)

    st = st_ref[...]
    o_inter = []
    for i in range(nsub):
        rows = slice(i * sub, (i + 1) * sub)
        bi, b2i, qi, ki = b[rows], b2[rows], q[rows], k[rows]
        for s in range(sub):
            diff = b2i - b2i[s:s + 1, :]
            if s > 0:
                diff = jnp.where(t_in_sub >= s, diff, NEG_INF)
            base = (i * sub + s) * sub
            x_ref[base:base + sub, :] = (qi * (ki[s:s + 1, :] * jnp.exp2(diff))).astype(BF16)
        o_inter.append(_dot_nt((qi * jnp.exp(bi)).astype(BF16), st.astype(BF16)))
        b_end = bi[sub - 1:sub, :]
        kd = ki * jnp.exp(b_end - bi)
        st = st * jnp.exp(b_end) + _dot_tn(v[rows].astype(BF16), kd.astype(BF16)) * m_f32
    st_ref[...] = st
    y_ref[...] = _dot(x_ref[...], m_bf)
    outs = []
    for i in range(nsub):
        vi = v[i * sub:(i + 1) * sub]
        acc = o_inter[i]
        for s in range(sub):
            base = (i * sub + s) * sub
            acc = acc + y_ref[base:base + sub, :] * vi[s:s + 1, :]
        outs.append(acc)
    o = jnp.concatenate(outs, axis=0)

    gt = g_ref[...].astype(F32)
    y = o * lax.rsqrt(_head_mean_sq(o, m_bf) + EPS) * gain_ref[...] * (gt * _sigmoid(gt))
    o_ref[...] = y.astype(o_ref.dtype)


def hgrn2(main, lb_all, gain, m_bf, batch, layer):
    n = main.shape[0]
    s = n // batch
    t = T_HG
    nc = s // t
    w = HG_W
    n_pairs = t * SUB_HG
    col = lambda cidx: pl.BlockSpec((t, w), lambda b, c: (b * nc + c, cidx))
    full = lambda shape: pl.BlockSpec(shape, lambda b, c: (0, 0))
    return pl.pallas_call(
        functools.partial(_hgrn_kernel, layer=layer),
        out_shape=jax.ShapeDtypeStruct((n, w), BF16),
        grid=(batch, nc),
        in_specs=[col(0), col(1), col(2), col(3), full(lb_all.shape), full((1, w)), full((w, w))],
        out_specs=pl.BlockSpec((t, w), lambda b, c: (b * nc + c, 0)),
        scratch_shapes=[pltpu.VMEM((w, w), F32), pltpu.VMEM((n_pairs, w), BF16), pltpu.VMEM((n_pairs, w), F32)],
        compiler_params=_cparams(("arbitrary", "arbitrary")),
        name="hgrn2",
    )(main, main, main, main, lb_all, gain, m_bf)


def _mlstm_kernel(q_ref, k_ref, v_ref, og_ref, grow_ref, gcol_ref, cw_ref, gain_ref, m_ref, o_ref,
                  ext_ref, ct_ref, n_ref, mm_ref):
    t = q_ref.shape[0]
    w = q_ref.shape[1]
    halo = 8

    @pl.when(pl.program_id(1) == 0)
    def _():
        ext_ref[0:halo, :] = jnp.zeros((halo, 2 * w), F32)
        ct_ref[...] = jnp.zeros_like(ct_ref)
        n_ref[...] = jnp.zeros_like(n_ref)
        mm_ref[...] = jnp.zeros_like(mm_ref)

    ext_ref[halo:halo + t, 0:w] = q_ref[...].astype(F32)
    ext_ref[halo:halo + t, w:2 * w] = k_ref[...].astype(F32)
    cw = cw_ref[...]
    y = None
    for j in range(CONV_K):
        term = ext_ref[halo - (CONV_K - 1) + j:halo - (CONV_K - 1) + j + t, :] * cw[j:j + 1, :]
        y = term if y is None else y + term
    tail = ext_ref[t:t + halo, :]
    ext_ref[0:halo, :] = tail
    qk = y * _sigmoid(y)
    q = qk[:, 0:w]
    k = qk[:, w:2 * w] * (HEAD_DIM ** -0.5)
    kb = k.astype(BF16)
    vb = v_ref[...]
    m_bf = m_ref[...]

    grow = grow_ref[...]
    gcol = gcol_ref[...]
    lane_head = lax.broadcasted_iota(jnp.int32, (1, w), 1) // HEAD_DIM
    r_i = lax.broadcasted_iota(jnp.int32, (t, t), 0)
    c_i = lax.broadcasted_iota(jnp.int32, (t, t), 1)
    causal = c_i <= r_i

    num_intra = jnp.zeros((t, w), F32)
    sint_l = jnp.zeros((t, w), F32)
    wsum_l = jnp.zeros((t, w), F32)
    mt_l = jnp.zeros((t, w), F32)
    wk_l = jnp.zeros((t, w), F32)
    decay_l = jnp.zeros((1, w), F32)
    for h in range(ML_HEADS):
        sel = lane_head == h
        qh = jnp.where(sel, q, 0.0).astype(BF16)
        s = _dot_nt(qh, kb)
        bc = gcol[:, ML_HEADS + h:ML_HEADS + h + 1]
        br = grow[ML_HEADS + h:ML_HEADS + h + 1, :]
        lir = grow[h:h + 1, :]
        lic = gcol[:, h:h + 1]
        dlog = jnp.where(causal, bc - br + lir, NEG_INF)
        mmh = mm_ref[h:h + 1, 0:1]
        inter = bc + mmh
        m_t = jnp.maximum(jnp.max(dlog, axis=1, keepdims=True), inter)
        wgt = s * jnp.exp(dlog - m_t)
        s_int = jnp.exp(inter - m_t)
        pv = _dot(wgt.astype(BF16), vb)
        num_intra = jnp.where(sel, pv, num_intra)
        sint_l = jnp.where(sel, s_int, sint_l)
        wsum_l = jnp.where(sel, jnp.sum(wgt, axis=1, keepdims=True), wsum_l)
        mt_l = jnp.where(sel, m_t, mt_l)
        b_end = br[:, t - 1:t]
        m_new = jnp.maximum(b_end + mmh, jnp.max(b_end - br + lir, axis=1, keepdims=True))
        wk_l = jnp.where(sel, jnp.exp(b_end - bc + lic - m_new), wk_l)
        decay_l = jnp.where(sel, jnp.exp(b_end + mmh - m_new), decay_l)
        mm_ref[h:h + 1, :] = jnp.broadcast_to(m_new - b_end, (1, mm_ref.shape[1]))

    ct = ct_ref[...]
    nrow = n_ref[0:1, :]
    q_c = _dot_nt(q.astype(BF16), ct.astype(BF16))
    qn = q * nrow
    qn_hi = qn.astype(BF16)
    qn_lo = (qn - qn_hi.astype(F32)).astype(BF16)
    qn_l = _dot(qn_hi, m_bf) + _dot(qn_lo, m_bf)
    num = num_intra + sint_l * q_c
    den = wsum_l + sint_l * qn_l
    hval = num / jnp.maximum(jnp.abs(den), jnp.exp(-mt_l))

    kw = k * wk_l
    upd = _dot_tn(vb, kw.astype(BF16))
    ct_ref[...] = decay_l * ct + upd * m_bf.astype(F32)
    n_ref[...] = jnp.broadcast_to(decay_l * nrow + jnp.sum(kw, axis=0, keepdims=True), n_ref.shape)

    og = og_ref[...].astype(F32)
    yv = hval * lax.rsqrt(_head_mean_sq(hval, m_bf) + EPS) * gain_ref[...] * _sigmoid(og)
    o_ref[...] = yv.astype(o_ref.dtype)


def mlstm(main, g_row, g_col, conv_w, gain, m_bf, batch):
    n = main.shape[0]
    s = n // batch
    t = min(T_ML, s)
    nc = s // t
    w = ML_W
    col = lambda cidx: pl.BlockSpec((t, w), lambda b, c: (b * nc + c, cidx))
    full = lambda shape: pl.BlockSpec(shape, lambda b, c: (0, 0))
    return pl.pallas_call(
        _mlstm_kernel,
        out_shape=jax.ShapeDtypeStruct((n, w), BF16),
        grid=(batch, nc),
        in_specs=[col(4), col(5), col(6), col(7),
                  pl.BlockSpec((N_GATE_ROWS, t), lambda b, c: (0, b * nc + c)),
                  pl.BlockSpec((t, N_GATE_ROWS), lambda b, c: (b * nc + c, 0)),
                  full((CONV_K, 2 * w)), full((1, w)), full((w, w))],
        out_specs=pl.BlockSpec((t, w), lambda b, c: (b * nc + c, 0)),
        scratch_shapes=[pltpu.VMEM((t + 8, 2 * w), F32), pltpu.VMEM((w, w), F32),
                        pltpu.VMEM((8, w), F32), pltpu.VMEM((8, LANES), F32)],
        compiler_params=_cparams(("arbitrary", "arbitrary")),
        name="mlstm",
    )(main, main, main, main, g_row, g_col, conv_w, gain, m_bf)


def _fox_kernel(q_ref, k_ref, v_ref, c_ref, o_ref, m_ref, acc_ref):
    tq = q_ref.shape[0]
    half = tq // 2
    qi = pl.program_id(2)
    lane = lax.broadcasted_iota(jnp.int32, (1, LANES), 1)
    first = lane < HEAD_DIM

    m_ref[...] = jnp.full(m_ref.shape, NEG_INF, F32)
    acc_ref[...] = jnp.zeros_like(acc_ref)

    def attend(r0, r1, kstart, klen, causal_shift):
        q2 = q_ref[r0:r1, :]
        v2 = v_ref[pl.ds(kstart, klen), :]
        k_aug = jnp.concatenate([k_ref[pl.ds(kstart, klen), :], c_ref[pl.ds(kstart, klen), :]], axis=1)
        for a in range(2):
            sel = first if a == 0 else jnp.logical_not(first)
            ones_lanes = jnp.logical_and(lane >= 3 * a, lane < 3 * a + 3)
            q_bias = jnp.broadcast_to(jnp.where(ones_lanes, 1.0, 0.0).astype(q2.dtype), q2.shape)
            q_aug = jnp.concatenate([jnp.where(sel, q2, jnp.zeros_like(q2)), q_bias], axis=1)
            s = _dot_nt(q_aug, k_aug)
            if causal_shift is not None:
                r_i = lax.broadcasted_iota(jnp.int32, s.shape, 0)
                c_i = lax.broadcasted_iota(jnp.int32, s.shape, 1)
                s = jnp.where(c_i <= r_i + causal_shift, s, NEG_INF)
            m_prev = m_ref[a, r0:r1, :]
            m_new = jnp.maximum(m_prev, jnp.max(s, axis=1, keepdims=True))
            p = jnp.concatenate([jnp.exp2(s[:, c * LANES:(c + 1) * LANES] - m_new).astype(v2.dtype)
                                 for c in range(klen // LANES)], axis=1)
            v_aug = jnp.where(sel, v2, jnp.ones_like(v2))
            acc_ref[a, r0:r1, :] = jnp.exp2(m_prev - m_new) * acc_ref[a, r0:r1, :] + _dot(p, v_aug)
            m_ref[a, r0:r1, :] = m_new

    def past_block(j, carry):
        attend(0, tq, pl.multiple_of(j * tq, tq), tq, None)
        return carry

    lax.fori_loop(0, qi, past_block, 0)
    d0 = pl.multiple_of(qi * tq, tq)
    attend(0, half, d0, half, 0)
    attend(half, tq, d0, tq, half)

    acc_a = acc_ref[0]
    acc_b = acc_ref[1]
    out = jnp.where(first, acc_a / pltpu.roll(acc_a, HEAD_DIM, 1), acc_b / pltpu.roll(acc_b, HEAD_DIM, 1))
    o_ref[...] = out.astype(o_ref.dtype)


def fox(main, c_aug, batch):
    n = main.shape[0]
    s = n // batch
    tq = min(TQ_FX, s)
    nq = s // tq
    pairs = FX_HEADS // 2
    qcol, kcol, vcol = 2048 // LANES, 2560 // LANES, 3072 // LANES
    seq_blk = lambda col0: pl.BlockSpec((s, LANES), lambda b, p, i: (b, col0 + p))
    return pl.pallas_call(
        _fox_kernel,
        out_shape=jax.ShapeDtypeStruct((n, FX_W), BF16),
        grid=(batch, pairs, nq),
        in_specs=[pl.BlockSpec((tq, LANES), lambda b, p, i: (b * nq + i, qcol + p)),
                  seq_blk(kcol), seq_blk(vcol), seq_blk(0)],
        out_specs=pl.BlockSpec((tq, LANES), lambda b, p, i: (b * nq + i, p)),
        scratch_shapes=[pltpu.VMEM((2, tq, LANES), F32), pltpu.VMEM((2, tq, LANES), F32)],
        compiler_params=_cparams(("arbitrary", "arbitrary", "arbitrary")),
        name="fox",
    )(main, main, main, c_aug)


def _outproj_body(x_ref, ohg_ref, oml_ref, ofx_ref, w_ref, g_ref):
    acc = x_ref[...]
    acc = acc + _dot(ohg_ref[...], w_ref[0:HG_W, :])
    acc = acc + _dot(oml_ref[...], w_ref[HG_W:HG_W + ML_W, :])
    acc = acc + _dot(ofx_ref[...], w_ref[HG_W + ML_W:, :])
    ms = jnp.mean(acc * acc, axis=-1, keepdims=True)
    h = (acc * lax.rsqrt(ms + EPS)) * g_ref[...]
    return acc, h


def _outproj_dense_kernel(x_ref, ohg_ref, oml_ref, ofx_ref, w_ref, g_ref, xo_ref, h_ref):
    acc, h = _outproj_body(x_ref, ohg_ref, oml_ref, ofx_ref, w_ref, g_ref)
    xo_ref[...] = acc
    h_ref[...] = h.astype(h_ref.dtype)


def _outproj_moe_kernel(x_ref, ohg_ref, oml_ref, ofx_ref, w_ref, g_ref, rhi_ref, rlo_ref,
                        xo_ref, h_ref, idx_ref, gate_ref):
    acc, h = _outproj_body(x_ref, ohg_ref, oml_ref, ofx_ref, w_ref, g_ref)
    xo_ref[...] = acc
    h_ref[...] = h
    h_hi = h.astype(BF16)
    h_lo = (h - h_hi.astype(F32)).astype(BF16)
    logits = _dot(h_hi, rhi_ref[...]) + _dot(h_hi, rlo_ref[...]) + _dot(h_lo, rhi_ref[...])
    lane_i = lax.broadcasted_iota(jnp.int32, logits.shape, 1)
    lane = lane_i.astype(F32)
    lg = jnp.where(lane_i < N_EXPERTS, logits, NEG_INF)
    m1 = jnp.max(lg, axis=1, keepdims=True)
    i1 = jnp.min(jnp.where(lg == m1, lane, float(LANES)), axis=1, keepdims=True)
    lg2 = jnp.where(lane == i1, NEG_INF, lg)
    m2 = jnp.max(lg2, axis=1, keepdims=True)
    i2 = jnp.min(jnp.where(lg2 == m2, lane, float(LANES)), axis=1, keepdims=True)
    e = jnp.exp(m2 - m1)
    g1 = 1.0 / (1.0 + e)
    g2 = e / (1.0 + e)
    idx_ref[...] = jnp.where(lane_i == 0, i1, jnp.where(lane_i == 1, i2, 0.0)).astype(jnp.int32)
    gate_ref[...] = jnp.where(lane_i == 0, g1, jnp.where(lane_i == 1, g2, 0.0))


def outproj(x, o_hg, o_ml, o_fx, w_out, g, router_parts=None):
    n, d = x.shape
    tm = min(TM_PROJ, n)
    row = lambda width: pl.BlockSpec((tm, width), lambda i: (i, 0))
    full = lambda shape: pl.BlockSpec(shape, lambda i: (0, 0))
    in_specs = [row(d), row(HG_W), row(ML_W), row(FX_W), full(w_out.shape), full((1, d))]
    args = [x, o_hg, o_ml, o_fx, w_out, g]
    if router_parts is None:
        kern = _outproj_dense_kernel
        out_shape = (jax.ShapeDtypeStruct((n, d), F32), jax.ShapeDtypeStruct((n, d), BF16))
        out_specs = (row(d), row(d))
    else:
        kern = _outproj_moe_kernel
        in_specs += [full(router_parts[0].shape), full(router_parts[1].shape)]
        args += list(router_parts)
        out_shape = (jax.ShapeDtypeStruct((n, d), F32), jax.ShapeDtypeStruct((n, d), F32),
                     jax.ShapeDtypeStruct((n, LANES), jnp.int32), jax.ShapeDtypeStruct((n, LANES), F32))
        out_specs = (row(d), row(d), row(LANES), row(LANES))
    return pl.pallas_call(
        kern, out_shape=out_shape, grid=(n // tm,), in_specs=in_specs, out_specs=out_specs,
        compiler_params=_cparams(("arbitrary",)), name="outproj",
    )(*args)


def _ffn_kernel(h_ref, x_ref, wg_ref, wu_ref, wd_ref, o_ref):
    @pl.when(pl.program_id(1) == 0)
    def _():
        o_ref[...] = x_ref[...]

    h = h_ref[...]
    gt = _dot(h, wg_ref[...])
    up = _dot(h, wu_ref[...])
    hid = (gt * _sigmoid(gt) * up).astype(BF16)
    o_ref[...] += _dot(hid, wd_ref[...])


def dense_ffn(h, x, wg, wu, wd):
    n, d = x.shape
    ff = wg.shape[1]
    tm = min(TM_FFN, n)
    tf = ff // 2 if (ff // 2) % LANES == 0 else ff
    return pl.pallas_call(
        _ffn_kernel,
        out_shape=jax.ShapeDtypeStruct((n, d), F32),
        grid=(n // tm, ff // tf),
        in_specs=[pl.BlockSpec((tm, d), lambda i, f: (i, 0)),
                  pl.BlockSpec((tm, d), lambda i, f: (i, 0)),
                  pl.BlockSpec((d, tf), lambda i, f: (0, f)),
                  pl.BlockSpec((d, tf), lambda i, f: (0, f)),
                  pl.BlockSpec((tf, d), lambda i, f: (f, 0))],
        out_specs=pl.BlockSpec((tm, d), lambda i, f: (i, 0)),
        compiler_params=_cparams(("arbitrary", "arbitrary")),
        name="dense_ffn",
    )(h, x, wg, wu, wd)


def _row_copy(src_ref, src_row, dst_ref, dst_row, sem):
    return pltpu.make_async_copy(src_ref.at[pl.ds(src_row, 1)], dst_ref.at[pl.ds(dst_row, 1)], sem)


def _invert_kernel(dest_ref, lo_ref, hi_ref, src_ref, dst_ref, *, n_tok, bm):
    def fill(i, carry):
        src_ref[i] = 0
        dst_ref[i] = TOP_K * n_tok + (i & (2 * bm - 1))
        return carry

    for r in range(lo_ref.shape[0]):
        lax.fori_loop(lo_ref[r], hi_ref[r], fill, 0)

    def put(tok, carry):
        for slot in range(TOP_K):
            row = dest_ref[TOP_K * tok + slot]
            src_ref[row] = tok
            dst_ref[row] = slot * n_tok + tok
        return carry

    lax.fori_loop(0, n_tok, put, 0, unroll=8)


def moe_invert(dest, pad_lo, pad_hi, p_rows, n_tok, bm):
    smem = pl.BlockSpec(memory_space=pltpu.SMEM)
    rows = jax.ShapeDtypeStruct((p_rows,), jnp.int32)
    return pl.pallas_call(
        functools.partial(_invert_kernel, n_tok=n_tok, bm=bm),
        out_shape=(rows, rows),
        in_specs=[smem, smem, smem],
        out_specs=(smem, smem),
        name="moe_invert",
    )(dest, pad_lo, pad_hi)


def _experts_kernel(src_ref, dst_ref, blk_e_ref, nused_ref, h_ref, wg_ref, wu_ref, wd_ref, y_ref,
                    xbuf_ref, xb_ref, acc_ref, obuf_ref, gsem, ssem, *, n_tok, chunk):
    del blk_e_ref
    bm = xb_ref.shape[0]
    tf = wg_ref.shape[2]
    nchunk = tf // chunk
    m = pl.program_id(0)
    f = pl.program_id(1)
    nb = pl.num_programs(0)
    nf = pl.num_programs(1)
    slot = m % 2
    other = 1 - slot
    n_used = nused_ref[0]
    live = m < n_used

    def gather_copy(block, s, i):
        return _row_copy(h_ref, src_ref[block * bm + i], xbuf_ref.at[s], i, gsem.at[s])

    def scatter_copy(block, s, i, to_spare):
        row = jnp.where(to_spare, TOP_K * n_tok + s * bm + i, dst_ref[block * bm + i])
        return _row_copy(obuf_ref.at[s], i, y_ref, row, ssem.at[s])

    def for_rows(fn):
        def body(i, carry):
            fn(i)
            return carry
        lax.fori_loop(0, bm, body, 0, unroll=8)

    def wait_gather(s):
        for_rows(lambda i: _row_copy(h_ref, 0, xbuf_ref.at[s], i, gsem.at[s]).wait())

    def wait_scatter(s):
        for_rows(lambda i: _row_copy(obuf_ref.at[s], i, y_ref, 0, ssem.at[s]).wait())

    def swiglu(first, issue):
        xb = xb_ref[...]
        for c in range(nchunk):
            cols = slice(c * chunk, (c + 1) * chunk)
            gt = _dot(xb, wg_ref[0, :, cols])
            up = _dot(xb, wu_ref[0, :, cols])
            hid = (gt * _sigmoid(gt) * up).astype(BF16)
            part = _dot(hid, wd_ref[0, cols, :])
            if first and c == 0:
                acc_ref[...] = part
            else:
                acc_ref[...] += part
            if issue is not None:
                for i in range(c * bm // nchunk, (c + 1) * bm // nchunk):
                    issue(i)

    @pl.when(jnp.logical_and(f == 0, live))
    def _():
        @pl.when(m == 0)
        def _():
            obuf_ref[...] = jnp.zeros_like(obuf_ref)
            spare0 = pltpu.make_async_copy(obuf_ref.at[0], y_ref.at[pl.ds(TOP_K * n_tok, bm)], ssem.at[0])
            spare0.start()
            spare0.wait()
            for_rows(lambda i: gather_copy(0, 0, i).start(priority=ROW_DMA_PRIORITY))

        wait_gather(slot)
        xb_ref[...] = xbuf_ref[slot].astype(BF16)
        prev = jnp.maximum(m - 1, 0)
        swiglu(True, lambda i: scatter_copy(prev, other, i, m == 0).start(priority=ROW_DMA_PRIORITY))

    @pl.when(jnp.logical_and(jnp.logical_and(f > 0, f < nf - 1), live))
    def _():
        swiglu(False, None)

    @pl.when(jnp.logical_and(f == nf - 1, live))
    def _():
        nxt = jnp.minimum(m + 1, nb - 1)
        swiglu(False, lambda i: gather_copy(nxt, other, i).start(priority=ROW_DMA_PRIORITY))

        @pl.when(m >= 1)
        def _():
            wait_scatter(slot)

        obuf_ref[slot] = acc_ref[...]

        @pl.when(m == n_used - 1)
        def _():
            wait_gather(other)
            for_rows(lambda i: scatter_copy(m, slot, i, False).start(priority=ROW_DMA_PRIORITY))
            wait_scatter(slot)
            wait_scatter(other)


def moe_experts(src_rows, dst_rows, blk_expert, n_used, h, wg, wu, wd):
    n_tok, d = h.shape
    p_rows = src_rows.shape[0]
    ff = wg.shape[2]
    bm = BM_MOE
    tf = TF_MOE
    nf = ff // tf
    assert nf >= 2, "the first and last hidden tile of a block carry different row copies"

    def w_idx(m, f, be, nu):
        f_eff = jnp.where(m % 2 == 0, f, nf - 1 - f)
        last = jnp.where((nu[0] - 1) % 2 == 0, nf - 1, 0)
        return be[m], jnp.where(m < nu[0], f_eff, last)

    return pl.pallas_call(
        functools.partial(_experts_kernel, n_tok=n_tok, chunk=CHUNK_MOE),
        out_shape=jax.ShapeDtypeStruct((TOP_K * n_tok + 2 * bm, d), F32),
        grid_spec=pltpu.PrefetchScalarGridSpec(
            num_scalar_prefetch=4, grid=(p_rows // bm, nf),
            in_specs=[pl.BlockSpec(memory_space=pl.ANY),
                      pl.BlockSpec((1, d, tf), lambda m, f, sr, ds, be, nu: (w_idx(m, f, be, nu)[0], 0, w_idx(m, f, be, nu)[1])),
                      pl.BlockSpec((1, d, tf), lambda m, f, sr, ds, be, nu: (w_idx(m, f, be, nu)[0], 0, w_idx(m, f, be, nu)[1])),
                      pl.BlockSpec((1, tf, d), lambda m, f, sr, ds, be, nu: (w_idx(m, f, be, nu)[0], w_idx(m, f, be, nu)[1], 0))],
            out_specs=pl.BlockSpec(memory_space=pl.ANY),
            scratch_shapes=[pltpu.VMEM((2, bm, d), F32), pltpu.VMEM((bm, d), BF16), pltpu.VMEM((bm, d), F32),
                            pltpu.VMEM((2, bm, d), F32), pltpu.SemaphoreType.DMA((2,)),
                            pltpu.SemaphoreType.DMA((2,))]),
        compiler_params=_cparams(("arbitrary", "arbitrary"), VMEM_LIMIT_BIG),
        name="moe_experts",
    )(src_rows, dst_rows, blk_expert, n_used, h, wg, wu, wd)


def _combine_kernel(x_ref, gate_ref, fg_ref, y0_ref, y1_ref, o_ref, *, final):
    gate = gate_ref[...]
    y = x_ref[...] + gate[:, 0:1] * y0_ref[...] + gate[:, 1:2] * y1_ref[...]
    if final:
        ms = jnp.mean(y * y, axis=-1, keepdims=True)
        y = (y * lax.rsqrt(ms + EPS)) * fg_ref[...]
    o_ref[...] = y


def moe_combine(x, gate, final_g, y, final):
    n, d = x.shape
    tb = min(TM_PROJ, n)
    return pl.pallas_call(
        functools.partial(_combine_kernel, final=final),
        out_shape=jax.ShapeDtypeStruct((n, d), F32),
        grid=(n // tb,),
        in_specs=[pl.BlockSpec((tb, d), lambda i: (i, 0)),
                  pl.BlockSpec((tb, LANES), lambda i: (i, 0)),
                  pl.BlockSpec((1, d), lambda i: (0, 0)),
                  pl.BlockSpec((tb, d), lambda i: (i, 0)),
                  pl.BlockSpec((tb, d), lambda i: (i + n // tb, 0))],
        out_specs=pl.BlockSpec((tb, d), lambda i: (i, 0)),
        compiler_params=_cparams(("arbitrary",)),
        name="moe_combine",
    )(x, gate, final_g, y, y)


def _final_norm_kernel(x_ref, g_ref, o_ref):
    x = x_ref[...]
    ms = jnp.mean(x * x, axis=-1, keepdims=True)
    o_ref[...] = (x * lax.rsqrt(ms + EPS)) * g_ref[...]


def final_norm(x, g):
    n, d = x.shape
    tm = min(TM_PROJ, n)
    return pl.pallas_call(
        _final_norm_kernel,
        out_shape=jax.ShapeDtypeStruct((n, d), F32),
        grid=(n // tm,),
        in_specs=[pl.BlockSpec((tm, d), lambda i: (i, 0)), pl.BlockSpec((1, d), lambda i: (0, 0))],
        out_specs=pl.BlockSpec((tm, d), lambda i: (i, 0)),
        compiler_params=_cparams(("arbitrary",)),
        name="final_norm",
    )(x, g)


def _moe_plan(top_idx, bm):
    n = top_idx.shape[0]
    a = n * TOP_K
    e_flat = top_idx.reshape(a)
    onehot = (e_flat[:, None] == jnp.arange(N_EXPERTS, dtype=jnp.int32)[None, :]).astype(jnp.int32)
    incl = jnp.cumsum(onehot, axis=0)
    rank = jnp.sum((incl - onehot) * onehot, axis=1)
    counts = incl[-1]
    padded = (counts + bm - 1) // bm * bm
    pad_end = jnp.cumsum(padded)
    pad_start = pad_end - padded
    dest = (jnp.sum(pad_start[None, :] * onehot, axis=1) + rank).astype(jnp.int32)
    p_rows = a + N_EXPERTS * bm
    n_blocks = p_rows // bm
    n_used = (pad_end[-1] // bm).astype(jnp.int32)
    blk_start = jnp.arange(n_blocks, dtype=jnp.int32) * bm
    blk_expert = jnp.sum((blk_start[:, None] >= pad_end[None, :]).astype(jnp.int32), axis=1)
    blk_expert = jnp.minimum(blk_expert, N_EXPERTS - 1)
    last_e = blk_expert[jnp.maximum(n_used - 1, 0)]
    blk_expert = jnp.where(jnp.arange(n_blocks) < n_used, blk_expert, last_e).astype(jnp.int32)
    pad_lo = jnp.concatenate([pad_start + counts, pad_end[-1:]]).astype(jnp.int32)
    pad_hi = jnp.concatenate([pad_end, jnp.full((1,), p_rows, jnp.int32)]).astype(jnp.int32)
    return dest, blk_expert, n_used.reshape(1), pad_lo, pad_hi, p_rows


def _head_block_mask(width):
    r = jnp.arange(width) // HEAD_DIM
    return (r[:, None] == r[None, :]).astype(BF16)


def kernel(x, attn_norm, w_in, hgrn_lb, hgrn_norm, mlstm_conv, mlstm_b_i, mlstm_b_f, mlstm_norm, fox_b_f, w_out, ffn_norm, dense_w_gate, dense_w_up, dense_w_down, router, moe_w_gate, moe_w_up, moe_w_down, final_norm_g):
    batch, seq, d = x.shape
    depth = w_in.shape[0]
    n = batch * seq
    xf = x.reshape(n, d)
    m_bf = _head_block_mask(HG_W)
    n_main_a = 4 * HG_W + 3 * ML_W
    gate_a = n_main_a
    mo_a = gate_a + 2 * ML_HEADS
    fx_a = mo_a + ML_W
    ff_a = fx_a + 3 * FX_W
    done = False
    for l in range(depth):
        wl = w_in[l]
        wq_fx = wl[:, fx_a:fx_a + FX_W] * (LOG2E * HEAD_DIM ** -0.5)
        w_main = jnp.concatenate([wl[:, :n_main_a], wl[:, mo_a:fx_a], wq_fx, wl[:, fx_a + FX_W:ff_a]],
                                 axis=1).astype(BF16)
        w_gate_t = jnp.concatenate([wl[:, gate_a:mo_a], wl[:, ff_a:]], axis=1).T.astype(BF16)
        main, g_row_raw = norm_inproj(xf, attn_norm[l].reshape(1, d), w_main, w_gate_t)
        bias = jnp.concatenate([mlstm_b_i[l], mlstm_b_f[l], fox_b_f[l]]).reshape(N_GATE_ROWS, 1).astype(F32)
        g_row, g_col, c_aug = gates(g_row_raw, bias, batch)
        o_hg = hgrn2(main, hgrn_lb.astype(F32), hgrn_norm[l].reshape(1, HG_W), m_bf, batch, l)
        o_ml = mlstm(main, g_row, g_col, mlstm_conv[l], mlstm_norm[l].reshape(1, ML_W), m_bf, batch)
        o_fx = fox(main, c_aug, batch)
        wo = w_out[l].astype(BF16)
        fg = ffn_norm[l].reshape(1, d)
        j = l // 2
        if l % 2 == 0:
            x_res, h = outproj(xf, o_hg, o_ml, o_fx, wo, fg)
            xf = dense_ffn(h, x_res, dense_w_gate[j].astype(BF16), dense_w_up[j].astype(BF16),
                           dense_w_down[j].astype(BF16))
        else:
            r = jnp.pad(router[j], ((0, 0), (0, LANES - N_EXPERTS)))
            r_hi = r.astype(BF16)
            r_lo = (r - r_hi.astype(F32)).astype(BF16)
            x_res, h, idx, gate = outproj(xf, o_hg, o_ml, o_fx, wo, fg, (r_hi, r_lo))
            dest, blk_expert, n_used, pad_lo, pad_hi, p_rows = _moe_plan(idx[:, :TOP_K], BM_MOE)
            src_rows, dst_rows = moe_invert(dest, pad_lo, pad_hi, p_rows, n, BM_MOE)
            y = moe_experts(src_rows, dst_rows, blk_expert, n_used, h, moe_w_gate[j].astype(BF16),
                            moe_w_up[j].astype(BF16), moe_w_down[j].astype(BF16))
            done = l == depth - 1
            xf = moe_combine(x_res, gate, final_norm_g.reshape(1, d), y, done)
    if not done:
        xf = final_norm(xf, final_norm_g.reshape(1, d))
    return xf.reshape(batch, seq, d)
```

```python
import functools

import jax
import jax.numpy as jnp
from jax import lax
from jax.experimental import pallas as pl
from jax.experimental.pallas import tpu as pltpu

F32 = jnp.float32
BF16 = jnp.bfloat16
EPS = 1e-6
NEG_INF = float("-inf")
LOG2E = 1.4426950408889634

HEAD_DIM = 64
HG_W = 256
ML_W = 256
FX_W = 512
ML_HEADS = 4
FX_HEADS = 8
CONV_K = 4
N_EXPERTS = 8
TOP_K = 2
N_GATE_ROWS = 16

LANES = 128
VMEM_LIMIT = 48 * 1024 * 1024
VMEM_LIMIT_BIG = 56 * 1024 * 1024

TM_PROJ = 512
TM_INPROJ = 1024
T_GATE = 512
T_HG = 128
SUB_HG = 16
T_ML = 128
TQ_FX = 1024
TM_FFN = 512
BM_MOE = 512
TF_MOE = 1792
CHUNK_MOE = 256
N_DMA_QUEUES = 2


def _cparams(sem, vmem=VMEM_LIMIT):
    return pltpu.CompilerParams(dimension_semantics=sem, vmem_limit_bytes=vmem)


def _split3(x):
    hi = x.astype(BF16)
    r = x - hi.astype(F32)
    mid = r.astype(BF16)
    lo = (r - mid.astype(F32)).astype(BF16)
    return hi, mid, lo


def _dot(a, b):
    return jnp.dot(a, b, preferred_element_type=F32)


def _dot_nt(a, b):
    return lax.dot_general(a, b, (((1,), (1,)), ((), ())), preferred_element_type=F32)


def _dot_tn(a, b):
    return lax.dot_general(a, b, (((0,), (0,)), ((), ())), preferred_element_type=F32)


def _dot3(parts, b):
    return _dot(parts[0], b) + _dot(parts[1], b) + _dot(parts[2], b)


def _log_sigmoid(z):
    return -(jnp.maximum(-z, 0.0) + jnp.log1p(jnp.exp(-jnp.abs(z))))


def _sigmoid(z):
    return 1.0 / (1.0 + jnp.exp(-z))


def _head_mean_sq(o, m_bf):
    o2 = o * o
    hi = o2.astype(BF16)
    lo = (o2 - hi.astype(F32)).astype(BF16)
    return (_dot(hi, m_bf) + _dot(lo, m_bf)) * (1.0 / HEAD_DIM)


def _norm_inproj_kernel(x_ref, g_ref, w_ref, wgt_ref, main_ref, grow_ref, *, tn):
    x = x_ref[...]
    ms = jnp.mean(x * x, axis=-1, keepdims=True)
    h = ((x * lax.rsqrt(ms + EPS)) * g_ref[...]).astype(BF16)
    for j in range(w_ref.shape[1] // tn):
        main_ref[:, j * tn:(j + 1) * tn] = _dot(h, w_ref[:, j * tn:(j + 1) * tn]).astype(BF16)
    grow_ref[...] = _dot_nt(wgt_ref[...], h)


def norm_inproj(x, g, w_main, w_gate_t):
    n, d = x.shape
    wm = w_main.shape[1]
    tm = min(TM_INPROJ, n)
    return pl.pallas_call(
        functools.partial(_norm_inproj_kernel, tn=512),
        out_shape=(jax.ShapeDtypeStruct((n, wm), BF16), jax.ShapeDtypeStruct((N_GATE_ROWS, n), F32)),
        grid=(n // tm,),
        in_specs=[pl.BlockSpec((tm, d), lambda i: (i, 0)),
                  pl.BlockSpec((1, d), lambda i: (0, 0)),
                  pl.BlockSpec((d, wm), lambda i: (0, 0)),
                  pl.BlockSpec((N_GATE_ROWS, d), lambda i: (0, 0))],
        out_specs=(pl.BlockSpec((tm, wm), lambda i: (i, 0)),
                   pl.BlockSpec((N_GATE_ROWS, tm), lambda i: (0, i))),
        compiler_params=_cparams(("arbitrary",)),
        name="norm_inproj",
    )(x, g, w_main, w_gate_t)


def _gates_kernel(g_ref, bias_ref, sel_ref, grow_ref, gcol_ref, caug_ref, carry_ref):
    t = g_ref.shape[1]

    @pl.when(pl.program_id(1) == 0)
    def _():
        carry_ref[...] = jnp.zeros_like(carry_ref)

    z = g_ref[...] + bias_ref[...]
    row = lax.broadcasted_iota(jnp.int32, z.shape, 0)
    is_input_gate = row < ML_HEADS
    val = jnp.where(is_input_gate, 0.0, _log_sigmoid(z))
    r_i = lax.broadcasted_iota(jnp.int32, (t, t), 0)
    c_i = lax.broadcasted_iota(jnp.int32, (t, t), 1)
    upper = jnp.where(r_i <= c_i, 1.0, 0.0).astype(BF16)
    tot = _dot3(_split3(val), upper) + carry_ref[:, 0:1]
    out = jnp.where(is_input_gate, z, tot)
    grow_ref[...] = out
    carry_ref[...] = jnp.broadcast_to(tot[:, t - 1:t], carry_ref.shape)
    eye = jnp.where(r_i == c_i, 1.0, 0.0).astype(BF16)
    p0, p1, p2 = _split3(out)
    gcol_ref[...] = _dot_nt(eye, p0) + _dot_nt(eye, p1) + _dot_nt(eye, p2)
    n0, n1, n2 = _split3(out * (-LOG2E))
    zrows = _dot(sel_ref[0], n0) + _dot(sel_ref[1], n1) + _dot(sel_ref[2], n2)
    caug_ref[...] = _dot_nt(eye, zrows.astype(BF16)).astype(BF16)


def _bias_lane_selectors():
    sel = [[[0.0] * N_GATE_ROWS for _ in range(FX_HEADS // 2 * LANES)] for _ in range(3)]
    for j in range(3):
        for p in range(FX_HEADS // 2):
            for a in range(2):
                sel[j][LANES * p + 3 * a + j][2 * ML_HEADS + 2 * p + a] = 1.0
    return jnp.asarray(sel, BF16)


def gates(g_row, bias, batch):
    r, n = g_row.shape
    s = n // batch
    t = min(T_GATE, s)
    nb = s // t
    sel = _bias_lane_selectors()
    wc = sel.shape[1]
    return pl.pallas_call(
        _gates_kernel,
        out_shape=(jax.ShapeDtypeStruct((r, n), F32), jax.ShapeDtypeStruct((n, r), F32),
                   jax.ShapeDtypeStruct((n, wc), BF16)),
        grid=(batch, nb),
        in_specs=[pl.BlockSpec((r, t), lambda b, j: (0, b * nb + j)),
                  pl.BlockSpec((r, 1), lambda b, j: (0, 0)),
                  pl.BlockSpec(sel.shape, lambda b, j: (0, 0, 0))],
        out_specs=(pl.BlockSpec((r, t), lambda b, j: (0, b * nb + j)),
                   pl.BlockSpec((t, r), lambda b, j: (b * nb + j, 0)),
                   pl.BlockSpec((t, wc), lambda b, j: (b * nb + j, 0))),
        scratch_shapes=[pltpu.VMEM((r, LANES), F32)],
        compiler_params=_cparams(("arbitrary", "arbitrary")),
        name="gates",
    )(g_row, bias, sel)


def _hgrn_kernel(q_ref, f_ref, i_ref, g_ref, lb_ref, gain_ref, m_ref, o_ref, st_ref, x_ref, y_ref, *, layer):
    t = q_ref.shape[0]
    sub = SUB_HG
    nsub = t // sub

    @pl.when(pl.program_id(1) == 0)
    def _():
        st_ref[...] = jnp.zeros_like(st_ref)

    lbp = lb_ref[...]
    rows = [lbp[r:r + 1, :] for r in range(lbp.shape[0])]
    mx = functools.reduce(jnp.maximum, rows)
    es = [jnp.exp(r - mx) for r in rows]
    tot = functools.reduce(lambda a, b: a + b, es)
    cs, run = [], None
    for e in es:
        run = e / tot if run is None else run + e / tot
        cs.append(run)
    lb = cs[layer] - cs[0]

    z = f_ref[...].astype(F32)
    a = jnp.log(lb)
    bb = jnp.log1p(-lb) + _log_sigmoid(z)
    log_f = jnp.maximum(a, bb) + jnp.log1p(jnp.exp(-jnp.abs(a - bb)))
    k = (1.0 - lb) * _sigmoid(-z)
    q = q_ref[...].astype(F32)
    v = i_ref[...].astype(F32)
    m_bf = m_ref[...]

    r_i = lax.broadcasted_iota(jnp.int32, (t, t), 0)
    c_i = lax.broadcasted_iota(jnp.int32, (t, t), 1)
    same_sub = (c_i // sub) == (r_i // sub)
    lower = jnp.where(jnp.logical_and(c_i <= r_i, same_sub), 1.0, 0.0).astype(BF16)
    f0, f1, f2 = _split3(log_f)
    b = _dot(lower, f0) + _dot(lower, f1) + _dot(lower, f2)
    b2 = b * LOG2E
    m_f32 = m_bf.astype(F32)
    t_in_sub = lax.broadcasted_iota(jnp.int32, (sub, q.shape[1]), 0)

    st = st_ref[...]
    o_inter = []
    for i in range(nsub):
        rows = slice(i * sub, (i + 1) * sub)
        bi, b2i, qi, ki = b[rows], b2[rows], q[rows], k[rows]
        for s in range(sub):
            diff = b2i - b2i[s:s + 1, :]
            if s > 0:
                diff = jnp.where(t_in_sub >= s, diff, NEG_INF)
            base = (i * sub + s) * sub
            x_ref[base:base + sub, :] = (qi * (ki[s:s + 1, :] * jnp.exp2(diff))).astype(BF16)
        o_inter.append(_dot_nt((qi * jnp.exp(bi)).astype(BF16), st.astype(BF16)))
        b_end = bi[sub - 1:sub, :]
        kd = ki * jnp.exp(b_end - bi)
        st = st * jnp.exp(b_end) + _dot_tn(v[rows].astype(BF16), kd.astype(BF16)) * m_f32
    st_ref[...] = st
    y_ref[...] = _dot(x_ref[...], m_bf)
    outs = []
    for i in range(nsub):
        vi = v[i * sub:(i + 1) * sub]
        acc = o_inter[i]
        for s in range(sub):
            base = (i * sub + s) * sub
            acc = acc + y_ref[base:base + sub, :] * vi[s:s + 1, :]
        outs.append(acc)
    o = jnp.concatenate(outs, axis=0)

    gt = g_ref[...].astype(F32)
    y = o * lax.rsqrt(_head_mean_sq(o, m_bf) + EPS) * gain_ref[...] * (gt * _sigmoid(gt))
    o_ref[...] = y.astype(o_ref.dtype)


def hgrn2(main, lb_all, gain, m_bf, batch, layer):
    n = main.shape[0]
    s = n // batch
    t = T_HG
    nc = s // t
    w = HG_W
    n_pairs = t * SUB_HG
    col = lambda cidx: pl.BlockSpec((t, w), lambda b, c: (b * nc + c, cidx))
    full = lambda shape: pl.BlockSpec(shape, lambda b, c: (0, 0))
    return pl.pallas_call(
        functools.partial(_hgrn_kernel, layer=layer),
        out_shape=jax.ShapeDtypeStruct((n, w), BF16),
        grid=(batch, nc),
        in_specs=[col(0), col(1), col(2), col(3), full(lb_all.shape), full((1, w)), full((w, w))],
        out_specs=pl.BlockSpec((t, w), lambda b, c: (b * nc + c, 0)),
        scratch_shapes=[pltpu.VMEM((w, w), F32), pltpu.VMEM((n_pairs, w), BF16), pltpu.VMEM((n_pairs, w), F32)],
        compiler_params=_cparams(("arbitrary", "arbitrary")),
        name="hgrn2",
    )(main, main, main, main, lb_all, gain, m_bf)


def _mlstm_kernel(q_ref, k_ref, v_ref, og_ref, grow_ref, gcol_ref, cw_ref, gain_ref, m_ref, o_ref,
                  ext_ref, ct_ref, n_ref, mm_ref):
    t = q_ref.shape[0]
    w = q_ref.shape[1]
    halo = 8

    @pl.when(pl.program_id(1) == 0)
    def _():
        ext_ref[0:halo, :] = jnp.zeros((halo, 2 * w), F32)
        ct_ref[...] = jnp.zeros_like(ct_ref)
        n_ref[...] = jnp.zeros_like(n_ref)
        mm_ref[...] = jnp.zeros_like(mm_ref)

    ext_ref[halo:halo + t, 0:w] = q_ref[...].astype(F32)
    ext_ref[halo:halo + t, w:2 * w] = k_ref[...].astype(F32)
    cw = cw_ref[...]
    y = None
    for j in range(CONV_K):
        term = ext_ref[halo - (CONV_K - 1) + j:halo - (CONV_K - 1) + j + t, :] * cw[j:j + 1, :]
        y = term if y is None else y + term
    tail = ext_ref[t:t + halo, :]
    ext_ref[0:halo, :] = tail
    qk = y * _sigmoid(y)
    q = qk[:, 0:w]
    k = qk[:, w:2 * w] * (HEAD_DIM ** -0.5)
    kb = k.astype(BF16)
    vb = v_ref[...]
    m_bf = m_ref[...]

    grow = grow_ref[...]
    gcol = gcol_ref[...]
    lane_head = lax.broadcasted_iota(jnp.int32, (1, w), 1) // HEAD_DIM
    r_i = lax.broadcasted_iota(jnp.int32, (t, t), 0)
    c_i = lax.broadcasted_iota(jnp.int32, (t, t), 1)
    causal = c_i <= r_i

    num_intra = jnp.zeros((t, w), F32)
    sint_l = jnp.zeros((t, w), F32)
    wsum_l = jnp.zeros((t, w), F32)
    mt_l = jnp.zeros((t, w), F32)
    wk_l = jnp.zeros((t, w), F32)
    decay_l = jnp.zeros((1, w), F32)
    for h in range(ML_HEADS):
        sel = lane_head == h
        qh = jnp.where(sel, q, 0.0).astype(BF16)
        s = _dot_nt(qh, kb)
        bc = gcol[:, ML_HEADS + h:ML_HEADS + h + 1]
        br = grow[ML_HEADS + h:ML_HEADS + h + 1, :]
        lir = grow[h:h + 1, :]
        lic = gcol[:, h:h + 1]
        dlog = jnp.where(causal, bc - br + lir, NEG_INF)
        mmh = mm_ref[h:h + 1, 0:1]
        inter = bc + mmh
        m_t = jnp.maximum(jnp.max(dlog, axis=1, keepdims=True), inter)
        wgt = s * jnp.exp(dlog - m_t)
        s_int = jnp.exp(inter - m_t)
        pv = _dot(wgt.astype(BF16), vb)
        num_intra = jnp.where(sel, pv, num_intra)
        sint_l = jnp.where(sel, s_int, sint_l)
        wsum_l = jnp.where(sel, jnp.sum(wgt, axis=1, keepdims=True), wsum_l)
        mt_l = jnp.where(sel, m_t, mt_l)
        b_end = br[:, t - 1:t]
        m_new = jnp.maximum(b_end + mmh, jnp.max(b_end - br + lir, axis=1, keepdims=True))
        wk_l = jnp.where(sel, jnp.exp(b_end - bc + lic - m_new), wk_l)
        decay_l = jnp.where(sel, jnp.exp(b_end + mmh - m_new), decay_l)
        mm_ref[h:h + 1, :] = jnp.broadcast_to(m_new - b_end, (1, mm_ref.shape[1]))

    ct = ct_ref[...]
    nrow = n_ref[0:1, :]
    q_c = _dot_nt(q.astype(BF16), ct.astype(BF16))
    qn = q * nrow
    qn_hi = qn.astype(BF16)
    qn_lo = (qn - qn_hi.astype(F32)).astype(BF16)
    qn_l = _dot(qn_hi, m_bf) + _dot(qn_lo, m_bf)
    num = num_intra + sint_l * q_c
    den = wsum_l + sint_l * qn_l
    hval = num / jnp.maximum(jnp.abs(den), jnp.exp(-mt_l))

    kw = k * wk_l
    upd = _dot_tn(vb, kw.astype(BF16))
    ct_ref[...] = decay_l * ct + upd * m_bf.astype(F32)
    n_ref[...] = jnp.broadcast_to(decay_l * nrow + jnp.sum(kw, axis=0, keepdims=True), n_ref.shape)

    og = og_ref[...].astype(F32)
    yv = hval * lax.rsqrt(_head_mean_sq(hval, m_bf) + EPS) * gain_ref[...] * _sigmoid(og)
    o_ref[...] = yv.astype(o_ref.dtype)


def mlstm(main, g_row, g_col, conv_w, gain, m_bf, batch):
    n = main.shape[0]
    s = n // batch
    t = min(T_ML, s)
    nc = s // t
    w = ML_W
    col = lambda cidx: pl.BlockSpec((t, w), lambda b, c: (b * nc + c, cidx))
    full = lambda shape: pl.BlockSpec(shape, lambda b, c: (0, 0))
    return pl.pallas_call(
        _mlstm_kernel,
        out_shape=jax.ShapeDtypeStruct((n, w), BF16),
        grid=(batch, nc),
        in_specs=[col(4), col(5), col(6), col(7),
                  pl.BlockSpec((N_GATE_ROWS, t), lambda b, c: (0, b * nc + c)),
                  pl.BlockSpec((t, N_GATE_ROWS), lambda b, c: (b * nc + c, 0)),
                  full((CONV_K, 2 * w)), full((1, w)), full((w, w))],
        out_specs=pl.BlockSpec((t, w), lambda b, c: (b * nc + c, 0)),
        scratch_shapes=[pltpu.VMEM((t + 8, 2 * w), F32), pltpu.VMEM((w, w), F32),
                        pltpu.VMEM((8, w), F32), pltpu.VMEM((8, LANES), F32)],
        compiler_params=_cparams(("arbitrary", "arbitrary")),
        name="mlstm",
    )(main, main, main, main, g_row, g_col, conv_w, gain, m_bf)


def _fox_kernel(q_ref, k_ref, v_ref, c_ref, o_ref, m_ref, acc_ref):
    tq = q_ref.shape[0]
    half = tq // 2
    qi = pl.program_id(2)
    lane = lax.broadcasted_iota(jnp.int32, (1, LANES), 1)
    first = lane < HEAD_DIM

    m_ref[...] = jnp.full(m_ref.shape, NEG_INF, F32)
    acc_ref[...] = jnp.zeros_like(acc_ref)

    def attend(r0, r1, kstart, klen, causal_shift):
        q2 = q_ref[r0:r1, :]
        v2 = v_ref[pl.ds(kstart, klen), :]
        k_aug = jnp.concatenate([k_ref[pl.ds(kstart, klen), :], c_ref[pl.ds(kstart, klen), :]], axis=1)
        for a in range(2):
            sel = first if a == 0 else jnp.logical_not(first)
            ones_lanes = jnp.logical_and(lane >= 3 * a, lane < 3 * a + 3)
            q_bias = jnp.broadcast_to(jnp.where(ones_lanes, 1.0, 0.0).astype(q2.dtype), q2.shape)
            q_aug = jnp.concatenate([jnp.where(sel, q2, jnp.zeros_like(q2)), q_bias], axis=1)
            s = _dot_nt(q_aug, k_aug)
            if causal_shift is not None:
                r_i = lax.broadcasted_iota(jnp.int32, s.shape, 0)
                c_i = lax.broadcasted_iota(jnp.int32, s.shape, 1)
                s = jnp.where(c_i <= r_i + causal_shift, s, NEG_INF)
            m_prev = m_ref[a, r0:r1, :]
            m_new = jnp.maximum(m_prev, jnp.max(s, axis=1, keepdims=True))
            p = jnp.concatenate([jnp.exp2(s[:, c * LANES:(c + 1) * LANES] - m_new).astype(v2.dtype)
                                 for c in range(klen // LANES)], axis=1)
            v_aug = jnp.where(sel, v2, jnp.ones_like(v2))
            acc_ref[a, r0:r1, :] = jnp.exp2(m_prev - m_new) * acc_ref[a, r0:r1, :] + _dot(p, v_aug)
            m_ref[a, r0:r1, :] = m_new

    def past_block(j, carry):
        attend(0, tq, pl.multiple_of(j * tq, tq), tq, None)
        return carry

    lax.fori_loop(0, qi, past_block, 0)
    d0 = pl.multiple_of(qi * tq, tq)
    attend(0, half, d0, half, 0)
    attend(half, tq, d0, tq, half)

    acc_a = acc_ref[0]
    acc_b = acc_ref[1]
    out = jnp.where(first, acc_a / pltpu.roll(acc_a, HEAD_DIM, 1), acc_b / pltpu.roll(acc_b, HEAD_DIM, 1))
    o_ref[...] = out.astype(o_ref.dtype)


def fox(main, c_aug, batch):
    n = main.shape[0]
    s = n // batch
    tq = min(TQ_FX, s)
    nq = s // tq
    pairs = FX_HEADS // 2
    qcol, kcol, vcol = 2048 // LANES, 2560 // LANES, 3072 // LANES
    seq_blk = lambda col0: pl.BlockSpec((s, LANES), lambda b, p, i: (b, col0 + p))
    return pl.pallas_call(
        _fox_kernel,
        out_shape=jax.ShapeDtypeStruct((n, FX_W), BF16),
        grid=(batch, pairs, nq),
        in_specs=[pl.BlockSpec((tq, LANES), lambda b, p, i: (b * nq + i, qcol + p)),
                  seq_blk(kcol), seq_blk(vcol), seq_blk(0)],
        out_specs=pl.BlockSpec((tq, LANES), lambda b, p, i: (b * nq + i, p)),
        scratch_shapes=[pltpu.VMEM((2, tq, LANES), F32), pltpu.VMEM((2, tq, LANES), F32)],
        compiler_params=_cparams(("arbitrary", "arbitrary", "arbitrary")),
        name="fox",
    )(main, main, main, c_aug)


def _outproj_body(x_ref, ohg_ref, oml_ref, ofx_ref, w_ref, g_ref):
    acc = x_ref[...]
    acc = acc + _dot(ohg_ref[...], w_ref[0:HG_W, :])
    acc = acc + _dot(oml_ref[...], w_ref[HG_W:HG_W + ML_W, :])
    acc = acc + _dot(ofx_ref[...], w_ref[HG_W + ML_W:, :])
    ms = jnp.mean(acc * acc, axis=-1, keepdims=True)
    h = (acc * lax.rsqrt(ms + EPS)) * g_ref[...]
    return acc, h


def _outproj_dense_kernel(x_ref, ohg_ref, oml_ref, ofx_ref, w_ref, g_ref, xo_ref, h_ref):
    acc, h = _outproj_body(x_ref, ohg_ref, oml_ref, ofx_ref, w_ref, g_ref)
    xo_ref[...] = acc
    h_ref[...] = h.astype(h_ref.dtype)


def _outproj_moe_kernel(x_ref, ohg_ref, oml_ref, ofx_ref, w_ref, g_ref, rhi_ref, rlo_ref,
                        xo_ref, h_ref, idx_ref, gate_ref):
    acc, h = _outproj_body(x_ref, ohg_ref, oml_ref, ofx_ref, w_ref, g_ref)
    xo_ref[...] = acc
    h_ref[...] = h
    h_hi = h.astype(BF16)
    h_lo = (h - h_hi.astype(F32)).astype(BF16)
    logits = _dot(h_hi, rhi_ref[...]) + _dot(h_hi, rlo_ref[...]) + _dot(h_lo, rhi_ref[...])
    lane_i = lax.broadcasted_iota(jnp.int32, logits.shape, 1)
    lane = lane_i.astype(F32)
    lg = jnp.where(lane_i < N_EXPERTS, logits, NEG_INF)
    m1 = jnp.max(lg, axis=1, keepdims=True)
    i1 = jnp.min(jnp.where(lg == m1, lane, float(LANES)), axis=1, keepdims=True)
    lg2 = jnp.where(lane == i1, NEG_INF, lg)
    m2 = jnp.max(lg2, axis=1, keepdims=True)
    i2 = jnp.min(jnp.where(lg2 == m2, lane, float(LANES)), axis=1, keepdims=True)
    e = jnp.exp(m2 - m1)
    g1 = 1.0 / (1.0 + e)
    g2 = e / (1.0 + e)
    idx_ref[...] = jnp.where(lane_i == 0, i1, jnp.where(lane_i == 1, i2, 0.0)).astype(jnp.int32)
    gate_ref[...] = jnp.where(lane_i == 0, g1, jnp.where(lane_i == 1, g2, 0.0))


def outproj(x, o_hg, o_ml, o_fx, w_out, g, router_parts=None):
    n, d = x.shape
    tm = min(TM_PROJ, n)
    row = lambda width: pl.BlockSpec((tm, width), lambda i: (i, 0))
    full = lambda shape: pl.BlockSpec(shape, lambda i: (0, 0))
    in_specs = [row(d), row(HG_W), row(ML_W), row(FX_W), full(w_out.shape), full((1, d))]
    args = [x, o_hg, o_ml, o_fx, w_out, g]
    if router_parts is None:
        kern = _outproj_dense_kernel
        out_shape = (jax.ShapeDtypeStruct((n, d), F32), jax.ShapeDtypeStruct((n, d), BF16))
        out_specs = (row(d), row(d))
    else:
        kern = _outproj_moe_kernel
        in_specs += [full(router_parts[0].shape), full(router_parts[1].shape)]
        args += list(router_parts)
        out_shape = (jax.ShapeDtypeStruct((n, d), F32), jax.ShapeDtypeStruct((n, d), F32),
                     jax.ShapeDtypeStruct((n, LANES), jnp.int32), jax.ShapeDtypeStruct((n, LANES), F32))
        out_specs = (row(d), row(d), row(LANES), row(LANES))
    return pl.pallas_call(
        kern, out_shape=out_shape, grid=(n // tm,), in_specs=in_specs, out_specs=out_specs,
        compiler_params=_cparams(("arbitrary",)), name="outproj",
    )(*args)


def _ffn_kernel(h_ref, x_ref, wg_ref, wu_ref, wd_ref, o_ref):
    @pl.when(pl.program_id(1) == 0)
    def _():
        o_ref[...] = x_ref[...]

    h = h_ref[...]
    gt = _dot(h, wg_ref[...])
    up = _dot(h, wu_ref[...])
    hid = (gt * _sigmoid(gt) * up).astype(BF16)
    o_ref[...] += _dot(hid, wd_ref[...])


def dense_ffn(h, x, wg, wu, wd):
    n, d = x.shape
    ff = wg.shape[1]
    tm = min(TM_FFN, n)
    tf = ff // 2 if (ff // 2) % LANES == 0 else ff
    return pl.pallas_call(
        _ffn_kernel,
        out_shape=jax.ShapeDtypeStruct((n, d), F32),
        grid=(n // tm, ff // tf),
        in_specs=[pl.BlockSpec((tm, d), lambda i, f: (i, 0)),
                  pl.BlockSpec((tm, d), lambda i, f: (i, 0)),
                  pl.BlockSpec((d, tf), lambda i, f: (0, f)),
                  pl.BlockSpec((d, tf), lambda i, f: (0, f)),
                  pl.BlockSpec((tf, d), lambda i, f: (f, 0))],
        out_specs=pl.BlockSpec((tm, d), lambda i, f: (i, 0)),
        compiler_params=_cparams(("arbitrary", "arbitrary")),
        name="dense_ffn",
    )(h, x, wg, wu, wd)


def _row_copy(src_ref, src_row, dst_ref, dst_row, sem):
    return pltpu.make_async_copy(src_ref.at[pl.ds(src_row, 1)], dst_ref.at[pl.ds(dst_row, 1)], sem)


def _invert_kernel(dest_ref, lo_ref, hi_ref, src_ref, dst_ref, *, n_tok, bm):
    def fill(i, carry):
        src_ref[i] = 0
        dst_ref[i] = TOP_K * n_tok + (i & (2 * bm - 1))
        return carry

    for r in range(lo_ref.shape[0]):
        lax.fori_loop(lo_ref[r], hi_ref[r], fill, 0)

    def put(tok, carry):
        for slot in range(TOP_K):
            row = dest_ref[TOP_K * tok + slot]
            src_ref[row] = tok
            dst_ref[row] = slot * n_tok + tok
        return carry

    lax.fori_loop(0, n_tok, put, 0, unroll=8)


def moe_invert(dest, pad_lo, pad_hi, p_rows, n_tok, bm):
    smem = pl.BlockSpec(memory_space=pltpu.SMEM)
    rows = jax.ShapeDtypeStruct((p_rows,), jnp.int32)
    return pl.pallas_call(
        functools.partial(_invert_kernel, n_tok=n_tok, bm=bm),
        out_shape=(rows, rows),
        in_specs=[smem, smem, smem],
        out_specs=(smem, smem),
        name="moe_invert",
    )(dest, pad_lo, pad_hi)


def _experts_kernel(src_ref, dst_ref, blk_e_ref, nused_ref, h_ref, wg_ref, wu_ref, wd_ref, y_ref,
                    xbuf_ref, xb_ref, acc_ref, obuf_ref, gsem, ssem, *, n_tok, chunk):
    del blk_e_ref
    bm = xb_ref.shape[0]
    tf = wg_ref.shape[2]
    nchunk = tf // chunk
    m = pl.program_id(0)
    f = pl.program_id(1)
    nb = pl.num_programs(0)
    nf = pl.num_programs(1)
    slot = m % 2
    other = 1 - slot
    n_used = nused_ref[0]
    live = m < n_used

    def gather_copy(block, s, i):
        return _row_copy(h_ref, src_ref[block * bm + i], xbuf_ref.at[s], i, gsem.at[s])

    def scatter_copy(block, s, i, to_spare):
        row = jnp.where(to_spare, TOP_K * n_tok + s * bm + i, dst_ref[block * bm + i])
        return _row_copy(obuf_ref.at[s], i, y_ref, row, ssem.at[s])

    def for_rows(fn):
        def body(i, carry):
            fn(i)
            return carry
        lax.fori_loop(0, bm, body, 0, unroll=8)

    def wait_gather(s):
        for_rows(lambda i: _row_copy(h_ref, 0, xbuf_ref.at[s], i, gsem.at[s]).wait())

    def wait_scatter(s):
        for_rows(lambda i: _row_copy(obuf_ref.at[s], i, y_ref, 0, ssem.at[s]).wait())

    def swiglu(first, issue):
        xb = xb_ref[...]
        n_groups = 3 * nchunk

        def issue_group(g):
            if issue is not None:
                for i in range(g * bm // n_groups, (g + 1) * bm // n_groups):
                    issue(i)

        for c in range(nchunk):
            cols = slice(c * chunk, (c + 1) * chunk)
            gt = _dot(xb, wg_ref[0, :, cols])
            issue_group(3 * c)
            up = _dot(xb, wu_ref[0, :, cols])
            issue_group(3 * c + 1)
            hid = (gt * _sigmoid(gt) * up).astype(BF16)
            part = _dot(hid, wd_ref[0, cols, :])
            if first and c == 0:
                acc_ref[...] = part
            else:
                acc_ref[...] += part
            issue_group(3 * c + 2)

    @pl.when(jnp.logical_and(f == 0, live))
    def _():
        @pl.when(m == 0)
        def _():
            obuf_ref[...] = jnp.zeros_like(obuf_ref)
            spare0 = pltpu.make_async_copy(obuf_ref.at[0], y_ref.at[pl.ds(TOP_K * n_tok, bm)], ssem.at[0])
            spare0.start()
            spare0.wait()
            for_rows(lambda i: gather_copy(0, 0, i).start())

        wait_gather(slot)
        xb_ref[...] = xbuf_ref[slot].astype(BF16)
        nxt = jnp.minimum(m + 1, nb - 1)
        swiglu(True, lambda i: gather_copy(nxt, other, i).start(priority=i % N_DMA_QUEUES))

    @pl.when(jnp.logical_and(jnp.logical_and(f > 0, f < nf - 1), live))
    def _():
        swiglu(False, None)

    @pl.when(jnp.logical_and(f == nf - 1, live))
    def _():
        prev = jnp.maximum(m - 1, 0)
        swiglu(False, lambda i: scatter_copy(prev, other, i, m == 0).start(priority=i % N_DMA_QUEUES))

        @pl.when(m >= 1)
        def _():
            wait_scatter(slot)

        obuf_ref[slot] = acc_ref[...]

        @pl.when(m == n_used - 1)
        def _():
            wait_gather(other)
            for_rows(lambda i: scatter_copy(m, slot, i, False).start())
            wait_scatter(slot)
            wait_scatter(other)


def moe_experts(src_rows, dst_rows, blk_expert, n_used, h, wg, wu, wd):
    n_tok, d = h.shape
    p_rows = src_rows.shape[0]
    ff = wg.shape[2]
    bm = BM_MOE
    tf = TF_MOE
    nf = ff // tf
    assert nf >= 2, "the first and last hidden tile of a block carry different row copies"

    def w_idx(m, f, be, nu):
        f_eff = jnp.where(m % 2 == 0, f, nf - 1 - f)
        last = jnp.where((nu[0] - 1) % 2 == 0, nf - 1, 0)
        return be[m], jnp.where(m < nu[0], f_eff, last)

    return pl.pallas_call(
        functools.partial(_experts_kernel, n_tok=n_tok, chunk=CHUNK_MOE),
        out_shape=jax.ShapeDtypeStruct((TOP_K * n_tok + 2 * bm, d), F32),
        grid_spec=pltpu.PrefetchScalarGridSpec(
            num_scalar_prefetch=4, grid=(p_rows // bm, nf),
            in_specs=[pl.BlockSpec(memory_space=pl.ANY),
                      pl.BlockSpec((1, d, tf), lambda m, f, sr, ds, be, nu: (w_idx(m, f, be, nu)[0], 0, w_idx(m, f, be, nu)[1])),
                      pl.BlockSpec((1, d, tf), lambda m, f, sr, ds, be, nu: (w_idx(m, f, be, nu)[0], 0, w_idx(m, f, be, nu)[1])),
                      pl.BlockSpec((1, tf, d), lambda m, f, sr, ds, be, nu: (w_idx(m, f, be, nu)[0], w_idx(m, f, be, nu)[1], 0))],
            out_specs=pl.BlockSpec(memory_space=pl.ANY),
            scratch_shapes=[pltpu.VMEM((2, bm, d), F32), pltpu.VMEM((bm, d), BF16), pltpu.VMEM((bm, d), F32),
                            pltpu.VMEM((2, bm, d), F32), pltpu.SemaphoreType.DMA((2,)),
                            pltpu.SemaphoreType.DMA((2,))]),
        compiler_params=_cparams(("arbitrary", "arbitrary"), VMEM_LIMIT_BIG),
        name="moe_experts",
    )(src_rows, dst_rows, blk_expert, n_used, h, wg, wu, wd)


def _combine_kernel(x_ref, gate_ref, fg_ref, y0_ref, y1_ref, o_ref, *, final):
    gate = gate_ref[...]
    y = x_ref[...] + gate[:, 0:1] * y0_ref[...] + gate[:, 1:2] * y1_ref[...]
    if final:
        ms = jnp.mean(y * y, axis=-1, keepdims=True)
        y = (y * lax.rsqrt(ms + EPS)) * fg_ref[...]
    o_ref[...] = y


def moe_combine(x, gate, final_g, y, final):
    n, d = x.shape
    tb = min(TM_PROJ, n)
    return pl.pallas_call(
        functools.partial(_combine_kernel, final=final),
        out_shape=jax.ShapeDtypeStruct((n, d), F32),
        grid=(n // tb,),
        in_specs=[pl.BlockSpec((tb, d), lambda i: (i, 0)),
                  pl.BlockSpec((tb, LANES), lambda i: (i, 0)),
                  pl.BlockSpec((1, d), lambda i: (0, 0)),
                  pl.BlockSpec((tb, d), lambda i: (i, 0)),
                  pl.BlockSpec((tb, d), lambda i: (i + n // tb, 0))],
        out_specs=pl.BlockSpec((tb, d), lambda i: (i, 0)),
        compiler_params=_cparams(("arbitrary",)),
        name="moe_combine",
    )(x, gate, final_g, y, y)


def _final_norm_kernel(x_ref, g_ref, o_ref):
    x = x_ref[...]
    ms = jnp.mean(x * x, axis=-1, keepdims=True)
    o_ref[...] = (x * lax.rsqrt(ms + EPS)) * g_ref[...]


def final_norm(x, g):
    n, d = x.shape
    tm = min(TM_PROJ, n)
    return pl.pallas_call(
        _final_norm_kernel,
        out_shape=jax.ShapeDtypeStruct((n, d), F32),
        grid=(n // tm,),
        in_specs=[pl.BlockSpec((tm, d), lambda i: (i, 0)), pl.BlockSpec((1, d), lambda i: (0, 0))],
        out_specs=pl.BlockSpec((tm, d), lambda i: (i, 0)),
        compiler_params=_cparams(("arbitrary",)),
        name="final_norm",
    )(x, g)


def _moe_plan(top_idx, bm):
    n = top_idx.shape[0]
    a = n * TOP_K
    e_flat = top_idx.reshape(a)
    onehot = (e_flat[:, None] == jnp.arange(N_EXPERTS, dtype=jnp.int32)[None, :]).astype(jnp.int32)
    incl = jnp.cumsum(onehot, axis=0)
    rank = jnp.sum((incl - onehot) * onehot, axis=1)
    counts = incl[-1]
    padded = (counts + bm - 1) // bm * bm
    pad_end = jnp.cumsum(padded)
    pad_start = pad_end - padded
    dest = (jnp.sum(pad_start[None, :] * onehot, axis=1) + rank).astype(jnp.int32)
    p_rows = a + N_EXPERTS * bm
    n_blocks = p_rows // bm
    n_used = (pad_end[-1] // bm).astype(jnp.int32)
    blk_start = jnp.arange(n_blocks, dtype=jnp.int32) * bm
    blk_expert = jnp.sum((blk_start[:, None] >= pad_end[None, :]).astype(jnp.int32), axis=1)
    blk_expert = jnp.minimum(blk_expert, N_EXPERTS - 1)
    last_e = blk_expert[jnp.maximum(n_used - 1, 0)]
    blk_expert = jnp.where(jnp.arange(n_blocks) < n_used, blk_expert, last_e).astype(jnp.int32)
    pad_lo = jnp.concatenate([pad_start + counts, pad_end[-1:]]).astype(jnp.int32)
    pad_hi = jnp.concatenate([pad_end, jnp.full((1,), p_rows, jnp.int32)]).astype(jnp.int32)
    return dest, blk_expert, n_used.reshape(1), pad_lo, pad_hi, p_rows


def _head_block_mask(width):
    r = jnp.arange(width) // HEAD_DIM
    return (r[:, None] == r[None, :]).astype(BF16)


def kernel(x, attn_norm, w_in, hgrn_lb, hgrn_norm, mlstm_conv, mlstm_b_i, mlstm_b_f, mlstm_norm, fox_b_f, w_out, ffn_norm, dense_w_gate, dense_w_up, dense_w_down, router, moe_w_gate, moe_w_up, moe_w_down, final_norm_g):
    batch, seq, d = x.shape
    depth = w_in.shape[0]
    n = batch * seq
    xf = x.reshape(n, d)
    m_bf = _head_block_mask(HG_W)
    n_main_a = 4 * HG_W + 3 * ML_W
    gate_a = n_main_a
    mo_a = gate_a + 2 * ML_HEADS
    fx_a = mo_a + ML_W
    ff_a = fx_a + 3 * FX_W
    done = False
    for l in range(depth):
        wl = w_in[l]
        wq_fx = wl[:, fx_a:fx_a + FX_W] * (LOG2E * HEAD_DIM ** -0.5)
        w_main = jnp.concatenate([wl[:, :n_main_a], wl[:, mo_a:fx_a], wq_fx, wl[:, fx_a + FX_W:ff_a]],
                                 axis=1).astype(BF16)
        w_gate_t = jnp.concatenate([wl[:, gate_a:mo_a], wl[:, ff_a:]], axis=1).T.astype(BF16)
        main, g_row_raw = norm_inproj(xf, attn_norm[l].reshape(1, d), w_main, w_gate_t)
        bias = jnp.concatenate([mlstm_b_i[l], mlstm_b_f[l], fox_b_f[l]]).reshape(N_GATE_ROWS, 1).astype(F32)
        g_row, g_col, c_aug = gates(g_row_raw, bias, batch)
        o_hg = hgrn2(main, hgrn_lb.astype(F32), hgrn_norm[l].reshape(1, HG_W), m_bf, batch, l)
        o_ml = mlstm(main, g_row, g_col, mlstm_conv[l], mlstm_norm[l].reshape(1, ML_W), m_bf, batch)
        o_fx = fox(main, c_aug, batch)
        wo = w_out[l].astype(BF16)
        fg = ffn_norm[l].reshape(1, d)
        j = l // 2
        if l % 2 == 0:
            x_res, h = outproj(xf, o_hg, o_ml, o_fx, wo, fg)
            xf = dense_ffn(h, x_res, dense_w_gate[j].astype(BF16), dense_w_up[j].astype(BF16),
                           dense_w_down[j].astype(BF16))
        else:
            r = jnp.pad(router[j], ((0, 0), (0, LANES - N_EXPERTS)))
            r_hi = r.astype(BF16)
            r_lo = (r - r_hi.astype(F32)).astype(BF16)
            x_res, h, idx, gate = outproj(xf, o_hg, o_ml, o_fx, wo, fg, (r_hi, r_lo))
            dest, blk_expert, n_used, pad_lo, pad_hi, p_rows = _moe_plan(idx[:, :TOP_K], BM_MOE)
            src_rows, dst_rows = moe_invert(dest, pad_lo, pad_hi, p_rows, n, BM_MOE)
            y = moe_experts(src_rows, dst_rows, blk_expert, n_used, h, moe_w_gate[j].astype(BF16),
                            moe_w_up[j].astype(BF16), moe_w_down[j].astype(BF16))
            done = l == depth - 1
            xf = moe_combine(x_res, gate, final_norm_g.reshape(1, d), y, done)
    if not done:
        xf = final_norm(xf, final_norm_g.reshape(1, d))
    return xf.reshape(batch, seq, d)
```

```python
import functools

import jax
import jax.numpy as jnp
import numpy as np
from jax import lax
from jax.experimental import pallas as pl
from jax.experimental.pallas import tpu as pltpu

F32 = jnp.float32
BF16 = jnp.bfloat16
EPS = 1e-6
NEG_INF = float("-inf")
LOG2E = 1.4426950408889634

HEAD_DIM = 64
HG_W = 256
ML_W = 256
FX_W = 512
ML_HEADS = 4
FX_HEADS = 8
CONV_K = 4
N_EXPERTS = 8
TOP_K = 2
N_GATE_ROWS = 16
ROUTE_ROWS = 8
ROUTER_ROWS = 16

LANES = 128
VMEM_LIMIT = 48 * 1024 * 1024
VMEM_LIMIT_BIG = 56 * 1024 * 1024

TM_PROJ = 512
TM_INPROJ = 1024
T_GATE = 512
T_HG = 128
SUB_HG = 16
T_ML = 128
TQ_FX = 1024
TM_FFN = 512
BM_MOE = 512
TF_MOE = 1792
CHUNK_MOE = 256
N_DMA_QUEUES = 2


def _cparams(sem, vmem=VMEM_LIMIT):
    return pltpu.CompilerParams(dimension_semantics=sem, vmem_limit_bytes=vmem)


def _split3(x):
    hi = x.astype(BF16)
    r = x - hi.astype(F32)
    mid = r.astype(BF16)
    lo = (r - mid.astype(F32)).astype(BF16)
    return hi, mid, lo


def _dot(a, b):
    return jnp.dot(a, b, preferred_element_type=F32)


def _dot_nt(a, b):
    return lax.dot_general(a, b, (((1,), (1,)), ((), ())), preferred_element_type=F32)


def _dot_tn(a, b):
    return lax.dot_general(a, b, (((0,), (0,)), ((), ())), preferred_element_type=F32)


def _dot3(parts, b):
    return _dot(parts[0], b) + _dot(parts[1], b) + _dot(parts[2], b)


def _log_sigmoid(z):
    return -(jnp.maximum(-z, 0.0) + jnp.log1p(jnp.exp(-jnp.abs(z))))


def _sigmoid(z):
    return 1.0 / (1.0 + jnp.exp(-z))


def _head_mean_sq(o, m_bf):
    o2 = o * o
    hi = o2.astype(BF16)
    lo = (o2 - hi.astype(F32)).astype(BF16)
    return (_dot(hi, m_bf) + _dot(lo, m_bf)) * (1.0 / HEAD_DIM)


def _norm_inproj_kernel(x_ref, g_ref, w_ref, wgt_ref, main_ref, grow_ref, *, tn):
    x = x_ref[...]
    ms = jnp.mean(x * x, axis=-1, keepdims=True)
    h = ((x * lax.rsqrt(ms + EPS)) * g_ref[...]).astype(BF16)
    for j in range(w_ref.shape[1] // tn):
        main_ref[:, j * tn:(j + 1) * tn] = _dot(h, w_ref[:, j * tn:(j + 1) * tn]).astype(BF16)
    grow_ref[...] = _dot_nt(wgt_ref[...], h)


def norm_inproj(x, g, w_main, w_gate_t):
    n, d = x.shape
    wm = w_main.shape[1]
    tm = min(TM_INPROJ, n)
    return pl.pallas_call(
        functools.partial(_norm_inproj_kernel, tn=512),
        out_shape=(jax.ShapeDtypeStruct((n, wm), BF16), jax.ShapeDtypeStruct((N_GATE_ROWS, n), F32)),
        grid=(n // tm,),
        in_specs=[pl.BlockSpec((tm, d), lambda i: (i, 0)),
                  pl.BlockSpec((1, d), lambda i: (0, 0)),
                  pl.BlockSpec((d, wm), lambda i: (0, 0)),
                  pl.BlockSpec((N_GATE_ROWS, d), lambda i: (0, 0))],
        out_specs=(pl.BlockSpec((tm, wm), lambda i: (i, 0)),
                   pl.BlockSpec((N_GATE_ROWS, tm), lambda i: (0, i))),
        compiler_params=_cparams(("arbitrary",)),
        name="norm_inproj",
    )(x, g, w_main, w_gate_t)


def _gates_kernel(g_ref, bias_ref, sel_ref, grow_ref, gcol_ref, caug_ref, carry_ref):
    t = g_ref.shape[1]

    @pl.when(pl.program_id(1) == 0)
    def _():
        carry_ref[...] = jnp.zeros_like(carry_ref)

    z = g_ref[...] + bias_ref[...]
    row = lax.broadcasted_iota(jnp.int32, z.shape, 0)
    is_input_gate = row < ML_HEADS
    val = jnp.where(is_input_gate, 0.0, _log_sigmoid(z))
    r_i = lax.broadcasted_iota(jnp.int32, (t, t), 0)
    c_i = lax.broadcasted_iota(jnp.int32, (t, t), 1)
    upper = jnp.where(r_i <= c_i, 1.0, 0.0).astype(BF16)
    tot = _dot3(_split3(val), upper) + carry_ref[:, 0:1]
    out = jnp.where(is_input_gate, z, tot)
    grow_ref[...] = out
    carry_ref[...] = jnp.broadcast_to(tot[:, t - 1:t], carry_ref.shape)
    eye = jnp.where(r_i == c_i, 1.0, 0.0).astype(BF16)
    p0, p1, p2 = _split3(out)
    gcol_ref[...] = _dot_nt(eye, p0) + _dot_nt(eye, p1) + _dot_nt(eye, p2)
    n0, n1, n2 = _split3(out * (-LOG2E))
    zrows = _dot(sel_ref[0], n0) + _dot(sel_ref[1], n1) + _dot(sel_ref[2], n2)
    caug_ref[...] = _dot_nt(eye, zrows.astype(BF16)).astype(BF16)


def _bias_lane_selectors():
    sel = [[[0.0] * N_GATE_ROWS for _ in range(FX_HEADS // 2 * LANES)] for _ in range(3)]
    for j in range(3):
        for p in range(FX_HEADS // 2):
            for a in range(2):
                sel[j][LANES * p + 3 * a + j][2 * ML_HEADS + 2 * p + a] = 1.0
    return jnp.asarray(sel, BF16)


def gates(g_row, bias, batch):
    r, n = g_row.shape
    s = n // batch
    t = min(T_GATE, s)
    nb = s // t
    sel = _bias_lane_selectors()
    wc = sel.shape[1]
    return pl.pallas_call(
        _gates_kernel,
        out_shape=(jax.ShapeDtypeStruct((r, n), F32), jax.ShapeDtypeStruct((n, r), F32),
                   jax.ShapeDtypeStruct((n, wc), BF16)),
        grid=(batch, nb),
        in_specs=[pl.BlockSpec((r, t), lambda b, j: (0, b * nb + j)),
                  pl.BlockSpec((r, 1), lambda b, j: (0, 0)),
                  pl.BlockSpec(sel.shape, lambda b, j: (0, 0, 0))],
        out_specs=(pl.BlockSpec((r, t), lambda b, j: (0, b * nb + j)),
                   pl.BlockSpec((t, r), lambda b, j: (b * nb + j, 0)),
                   pl.BlockSpec((t, wc), lambda b, j: (b * nb + j, 0))),
        scratch_shapes=[pltpu.VMEM((r, LANES), F32)],
        compiler_params=_cparams(("arbitrary", "arbitrary")),
        name="gates",
    )(g_row, bias, sel)


def _hgrn_kernel(q_ref, f_ref, i_ref, g_ref, lb_ref, gain_ref, m_ref, o_ref, st_ref, x_ref, y_ref, *, layer):
    t = q_ref.shape[0]
    sub = SUB_HG
    nsub = t // sub

    @pl.when(pl.program_id(1) == 0)
    def _():
        st_ref[...] = jnp.zeros_like(st_ref)

    lbp = lb_ref[...]
    rows = [lbp[r:r + 1, :] for r in range(lbp.shape[0])]
    mx = functools.reduce(jnp.maximum, rows)
    es = [jnp.exp(r - mx) for r in rows]
    tot = functools.reduce(lambda a, b: a + b, es)
    cs, run = [], None
    for e in es:
        run = e / tot if run is None else run + e / tot
        cs.append(run)
    lb = cs[layer] - cs[0]

    z = f_ref[...].astype(F32)
    a = jnp.log(lb)
    bb = jnp.log1p(-lb) + _log_sigmoid(z)
    log_f = jnp.maximum(a, bb) + jnp.log1p(jnp.exp(-jnp.abs(a - bb)))
    k = (1.0 - lb) * _sigmoid(-z)
    q = q_ref[...].astype(F32)
    v = i_ref[...].astype(F32)
    m_bf = m_ref[...]

    r_i = lax.broadcasted_iota(jnp.int32, (t, t), 0)
    c_i = lax.broadcasted_iota(jnp.int32, (t, t), 1)
    same_sub = (c_i // sub) == (r_i // sub)
    lower = jnp.where(jnp.logical_and(c_i <= r_i, same_sub), 1.0, 0.0).astype(BF16)
    f0, f1, f2 = _split3(log_f)
    b = _dot(lower, f0) + _dot(lower, f1) + _dot(lower, f2)
    b2 = b * LOG2E
    m_f32 = m_bf.astype(F32)
    t_in_sub = lax.broadcasted_iota(jnp.int32, (sub, q.shape[1]), 0)

    st = st_ref[...]
    o_inter = []
    for i in range(nsub):
        rows = slice(i * sub, (i + 1) * sub)
        bi, b2i, qi, ki = b[rows], b2[rows], q[rows], k[rows]
        for s in range(sub):
            diff = b2i - b2i[s:s + 1, :]
            if s > 0:
                diff = jnp.where(t_in_sub >= s, diff, NEG_INF)
            base = (i * sub + s) * sub
            x_ref[base:base + sub, :] = (qi * (ki[s:s + 1, :] * jnp.exp2(diff))).astype(BF16)
        o_inter.append(_dot_nt((qi * jnp.exp(bi)).astype(BF16), st.astype(BF16)))
        b_end = bi[sub - 1:sub, :]
        kd = ki * jnp.exp(b_end - bi)
        st = st * jnp.exp(b_end) + _dot_tn(v[rows].astype(BF16), kd.astype(BF16)) * m_f32
    st_ref[...] = st
    y_ref[...] = _dot(x_ref[...], m_bf)
    outs = []
    for i in range(nsub):
        vi = v[i * sub:(i + 1) * sub]
        acc = o_inter[i]
        for s in range(sub):
            base = (i * sub + s) * sub
            acc = acc + y_ref[base:base + sub, :] * vi[s:s + 1, :]
        outs.append(acc)
    o = jnp.concatenate(outs, axis=0)

    gt = g_ref[...].astype(F32)
    y = o * lax.rsqrt(_head_mean_sq(o, m_bf) + EPS) * gain_ref[...] * (gt * _sigmoid(gt))
    o_ref[...] = y.astype(o_ref.dtype)


def hgrn2(main, lb_all, gain, m_bf, batch, layer):
    n = main.shape[0]
    s = n // batch
    t = T_HG
    nc = s // t
    w = HG_W
    n_pairs = t * SUB_HG
    col = lambda cidx: pl.BlockSpec((t, w), lambda b, c: (b * nc + c, cidx))
    full = lambda shape: pl.BlockSpec(shape, lambda b, c: (0, 0))
    return pl.pallas_call(
        functools.partial(_hgrn_kernel, layer=layer),
        out_shape=jax.ShapeDtypeStruct((n, w), BF16),
        grid=(batch, nc),
        in_specs=[col(0), col(1), col(2), col(3), full(lb_all.shape), full((1, w)), full((w, w))],
        out_specs=pl.BlockSpec((t, w), lambda b, c: (b * nc + c, 0)),
        scratch_shapes=[pltpu.VMEM((w, w), F32), pltpu.VMEM((n_pairs, w), BF16), pltpu.VMEM((n_pairs, w), F32)],
        compiler_params=_cparams(("arbitrary", "arbitrary")),
        name="hgrn2",
    )(main, main, main, main, lb_all, gain, m_bf)


def _mlstm_kernel(q_ref, k_ref, v_ref, og_ref, grow_ref, gcol_ref, cw_ref, gain_ref, m_ref, o_ref,
                  ext_ref, ct_ref, n_ref, mm_ref):
    t = q_ref.shape[0]
    w = q_ref.shape[1]
    halo = 8

    @pl.when(pl.program_id(1) == 0)
    def _():
        ext_ref[0:halo, :] = jnp.zeros((halo, 2 * w), F32)
        ct_ref[...] = jnp.zeros_like(ct_ref)
        n_ref[...] = jnp.zeros_like(n_ref)
        mm_ref[...] = jnp.zeros_like(mm_ref)

    ext_ref[halo:halo + t, 0:w] = q_ref[...].astype(F32)
    ext_ref[halo:halo + t, w:2 * w] = k_ref[...].astype(F32)
    cw = cw_ref[...]
    y = None
    for j in range(CONV_K):
        term = ext_ref[halo - (CONV_K - 1) + j:halo - (CONV_K - 1) + j + t, :] * cw[j:j + 1, :]
        y = term if y is None else y + term
    tail = ext_ref[t:t + halo, :]
    ext_ref[0:halo, :] = tail
    qk = y * _sigmoid(y)
    q = qk[:, 0:w]
    k = qk[:, w:2 * w] * (HEAD_DIM ** -0.5)
    kb = k.astype(BF16)
    vb = v_ref[...]
    m_bf = m_ref[...]

    grow = grow_ref[...]
    gcol = gcol_ref[...]
    lane_head = lax.broadcasted_iota(jnp.int32, (1, w), 1) // HEAD_DIM
    r_i = lax.broadcasted_iota(jnp.int32, (t, t), 0)
    c_i = lax.broadcasted_iota(jnp.int32, (t, t), 1)
    causal = c_i <= r_i

    num_intra = jnp.zeros((t, w), F32)
    sint_l = jnp.zeros((t, w), F32)
    wsum_l = jnp.zeros((t, w), F32)
    mt_l = jnp.zeros((t, w), F32)
    wk_l = jnp.zeros((t, w), F32)
    decay_l = jnp.zeros((1, w), F32)
    for h in range(ML_HEADS):
        sel = lane_head == h
        qh = jnp.where(sel, q, 0.0).astype(BF16)
        s = _dot_nt(qh, kb)
        bc = gcol[:, ML_HEADS + h:ML_HEADS + h + 1]
        br = grow[ML_HEADS + h:ML_HEADS + h + 1, :]
        lir = grow[h:h + 1, :]
        lic = gcol[:, h:h + 1]
        dlog = jnp.where(causal, bc - br + lir, NEG_INF)
        mmh = mm_ref[h:h + 1, 0:1]
        inter = bc + mmh
        m_t = jnp.maximum(jnp.max(dlog, axis=1, keepdims=True), inter)
        wgt = s * jnp.exp(dlog - m_t)
        s_int = jnp.exp(inter - m_t)
        pv = _dot(wgt.astype(BF16), vb)
        num_intra = jnp.where(sel, pv, num_intra)
        sint_l = jnp.where(sel, s_int, sint_l)
        wsum_l = jnp.where(sel, jnp.sum(wgt, axis=1, keepdims=True), wsum_l)
        mt_l = jnp.where(sel, m_t, mt_l)
        b_end = br[:, t - 1:t]
        m_new = jnp.maximum(b_end + mmh, jnp.max(b_end - br + lir, axis=1, keepdims=True))
        wk_l = jnp.where(sel, jnp.exp(b_end - bc + lic - m_new), wk_l)
        decay_l = jnp.where(sel, jnp.exp(b_end + mmh - m_new), decay_l)
        mm_ref[h:h + 1, :] = jnp.broadcast_to(m_new - b_end, (1, mm_ref.shape[1]))

    ct = ct_ref[...]
    nrow = n_ref[0:1, :]
    q_c = _dot_nt(q.astype(BF16), ct.astype(BF16))
    qn = q * nrow
    qn_hi = qn.astype(BF16)
    qn_lo = (qn - qn_hi.astype(F32)).astype(BF16)
    qn_l = _dot(qn_hi, m_bf) + _dot(qn_lo, m_bf)
    num = num_intra + sint_l * q_c
    den = wsum_l + sint_l * qn_l
    hval = num / jnp.maximum(jnp.abs(den), jnp.exp(-mt_l))

    kw = k * wk_l
    upd = _dot_tn(vb, kw.astype(BF16))
    ct_ref[...] = decay_l * ct + upd * m_bf.astype(F32)
    n_ref[...] = jnp.broadcast_to(decay_l * nrow + jnp.sum(kw, axis=0, keepdims=True), n_ref.shape)

    og = og_ref[...].astype(F32)
    yv = hval * lax.rsqrt(_head_mean_sq(hval, m_bf) + EPS) * gain_ref[...] * _sigmoid(og)
    o_ref[...] = yv.astype(o_ref.dtype)


def mlstm(main, g_row, g_col, conv_w, gain, m_bf, batch):
    n = main.shape[0]
    s = n // batch
    t = min(T_ML, s)
    nc = s // t
    w = ML_W
    col = lambda cidx: pl.BlockSpec((t, w), lambda b, c: (b * nc + c, cidx))
    full = lambda shape: pl.BlockSpec(shape, lambda b, c: (0, 0))
    return pl.pallas_call(
        _mlstm_kernel,
        out_shape=jax.ShapeDtypeStruct((n, w), BF16),
        grid=(batch, nc),
        in_specs=[col(4), col(5), col(6), col(7),
                  pl.BlockSpec((N_GATE_ROWS, t), lambda b, c: (0, b * nc + c)),
                  pl.BlockSpec((t, N_GATE_ROWS), lambda b, c: (b * nc + c, 0)),
                  full((CONV_K, 2 * w)), full((1, w)), full((w, w))],
        out_specs=pl.BlockSpec((t, w), lambda b, c: (b * nc + c, 0)),
        scratch_shapes=[pltpu.VMEM((t + 8, 2 * w), F32), pltpu.VMEM((w, w), F32),
                        pltpu.VMEM((8, w), F32), pltpu.VMEM((8, LANES), F32)],
        compiler_params=_cparams(("arbitrary", "arbitrary")),
        name="mlstm",
    )(main, main, main, main, g_row, g_col, conv_w, gain, m_bf)


def _fox_kernel(q_ref, k_ref, v_ref, c_ref, o_ref, m_ref, acc_ref):
    tq = q_ref.shape[0]
    half = tq // 2
    qi = pl.program_id(2)
    lane = lax.broadcasted_iota(jnp.int32, (1, LANES), 1)
    first = lane < HEAD_DIM

    m_ref[...] = jnp.full(m_ref.shape, NEG_INF, F32)
    acc_ref[...] = jnp.zeros_like(acc_ref)

    def attend(r0, r1, kstart, klen, causal_shift):
        q2 = q_ref[r0:r1, :]
        v2 = v_ref[pl.ds(kstart, klen), :]
        k_aug = jnp.concatenate([k_ref[pl.ds(kstart, klen), :], c_ref[pl.ds(kstart, klen), :]], axis=1)
        for a in range(2):
            sel = first if a == 0 else jnp.logical_not(first)
            ones_lanes = jnp.logical_and(lane >= 3 * a, lane < 3 * a + 3)
            q_bias = jnp.broadcast_to(jnp.where(ones_lanes, 1.0, 0.0).astype(q2.dtype), q2.shape)
            q_aug = jnp.concatenate([jnp.where(sel, q2, jnp.zeros_like(q2)), q_bias], axis=1)
            s = _dot_nt(q_aug, k_aug)
            if causal_shift is not None:
                r_i = lax.broadcasted_iota(jnp.int32, s.shape, 0)
                c_i = lax.broadcasted_iota(jnp.int32, s.shape, 1)
                s = jnp.where(c_i <= r_i + causal_shift, s, NEG_INF)
            m_prev = m_ref[a, r0:r1, :]
            m_new = jnp.maximum(m_prev, jnp.max(s, axis=1, keepdims=True))
            p = jnp.concatenate([jnp.exp2(s[:, c * LANES:(c + 1) * LANES] - m_new).astype(v2.dtype)
                                 for c in range(klen // LANES)], axis=1)
            v_aug = jnp.where(sel, v2, jnp.ones_like(v2))
            acc_ref[a, r0:r1, :] = jnp.exp2(m_prev - m_new) * acc_ref[a, r0:r1, :] + _dot(p, v_aug)
            m_ref[a, r0:r1, :] = m_new

    def past_block(j, carry):
        attend(0, tq, pl.multiple_of(j * tq, tq), tq, None)
        return carry

    lax.fori_loop(0, qi, past_block, 0)
    d0 = pl.multiple_of(qi * tq, tq)
    attend(0, half, d0, half, 0)
    attend(half, tq, d0, tq, half)

    acc_a = acc_ref[0]
    acc_b = acc_ref[1]
    out = jnp.where(first, acc_a / pltpu.roll(acc_a, HEAD_DIM, 1), acc_b / pltpu.roll(acc_b, HEAD_DIM, 1))
    o_ref[...] = out.astype(o_ref.dtype)


def fox(main, c_aug, batch):
    n = main.shape[0]
    s = n // batch
    tq = min(TQ_FX, s)
    nq = s // tq
    pairs = FX_HEADS // 2
    qcol, kcol, vcol = 2048 // LANES, 2560 // LANES, 3072 // LANES
    seq_blk = lambda col0: pl.BlockSpec((s, LANES), lambda b, p, i: (b, col0 + p))
    return pl.pallas_call(
        _fox_kernel,
        out_shape=jax.ShapeDtypeStruct((n, FX_W), BF16),
        grid=(batch, pairs, nq),
        in_specs=[pl.BlockSpec((tq, LANES), lambda b, p, i: (b * nq + i, qcol + p)),
                  seq_blk(kcol), seq_blk(vcol), seq_blk(0)],
        out_specs=pl.BlockSpec((tq, LANES), lambda b, p, i: (b * nq + i, p)),
        scratch_shapes=[pltpu.VMEM((2, tq, LANES), F32), pltpu.VMEM((2, tq, LANES), F32)],
        compiler_params=_cparams(("arbitrary", "arbitrary", "arbitrary")),
        name="fox",
    )(main, main, main, c_aug)


def _outproj_body(x_ref, ohg_ref, oml_ref, ofx_ref, w_ref, g_ref):
    acc = x_ref[...]
    acc = acc + _dot(ohg_ref[...], w_ref[0:HG_W, :])
    acc = acc + _dot(oml_ref[...], w_ref[HG_W:HG_W + ML_W, :])
    acc = acc + _dot(ofx_ref[...], w_ref[HG_W + ML_W:, :])
    ms = jnp.mean(acc * acc, axis=-1, keepdims=True)
    h = (acc * lax.rsqrt(ms + EPS)) * g_ref[...]
    return acc, h


def _outproj_dense_kernel(x_ref, ohg_ref, oml_ref, ofx_ref, w_ref, g_ref, xo_ref, h_ref):
    acc, h = _outproj_body(x_ref, ohg_ref, oml_ref, ofx_ref, w_ref, g_ref)
    xo_ref[...] = acc
    h_ref[...] = h.astype(h_ref.dtype)


def _outproj_moe_kernel(x_ref, ohg_ref, oml_ref, ofx_ref, w_ref, g_ref, rhi_ref, rlo_ref,
                        xo_ref, h_ref, idx_ref, gate_ref):
    acc, h = _outproj_body(x_ref, ohg_ref, oml_ref, ofx_ref, w_ref, g_ref)
    xo_ref[...] = acc
    h_ref[...] = h
    h_hi = h.astype(BF16)
    h_lo = (h - h_hi.astype(F32)).astype(BF16)
    logits = _dot_nt(rhi_ref[...], h_hi) + _dot_nt(rlo_ref[...], h_hi) + _dot_nt(rhi_ref[...], h_lo)
    row_i = lax.broadcasted_iota(jnp.int32, logits.shape, 0)
    row = row_i.astype(F32)
    n_rows = float(logits.shape[0])
    lg = jnp.where(row_i < N_EXPERTS, logits, NEG_INF)
    m1 = jnp.max(lg, axis=0, keepdims=True)
    i1 = jnp.min(jnp.where(lg == m1, row, n_rows), axis=0, keepdims=True)
    lg2 = jnp.where(row == i1, NEG_INF, lg)
    m2 = jnp.max(lg2, axis=0, keepdims=True)
    i2 = jnp.min(jnp.where(lg2 == m2, row, n_rows), axis=0, keepdims=True)
    e = jnp.exp(m2 - m1)
    g1 = 1.0 / (1.0 + e)
    g2 = e / (1.0 + e)
    out_row = lax.broadcasted_iota(jnp.int32, idx_ref.shape, 0)
    idx_ref[...] = jnp.where(out_row == 0, i1, jnp.where(out_row == 1, i2, 0.0)).astype(jnp.int32)
    gate_ref[...] = jnp.where(out_row == 0, g1, jnp.where(out_row == 1, g2, 0.0))


def outproj(x, o_hg, o_ml, o_fx, w_out, g, router_parts=None):
    n, d = x.shape
    tm = min(TM_PROJ, n)
    row = lambda width: pl.BlockSpec((tm, width), lambda i: (i, 0))
    full = lambda shape: pl.BlockSpec(shape, lambda i: (0, 0))
    in_specs = [row(d), row(HG_W), row(ML_W), row(FX_W), full(w_out.shape), full((1, d))]
    args = [x, o_hg, o_ml, o_fx, w_out, g]
    if router_parts is None:
        kern = _outproj_dense_kernel
        out_shape = (jax.ShapeDtypeStruct((n, d), F32), jax.ShapeDtypeStruct((n, d), BF16))
        out_specs = (row(d), row(d))
    else:
        kern = _outproj_moe_kernel
        in_specs += [full(router_parts[0].shape), full(router_parts[1].shape)]
        args += list(router_parts)
        rout = pl.BlockSpec((ROUTE_ROWS, tm), lambda i: (0, i))
        out_shape = (jax.ShapeDtypeStruct((n, d), F32), jax.ShapeDtypeStruct((n, d), F32),
                     jax.ShapeDtypeStruct((ROUTE_ROWS, n), jnp.int32), jax.ShapeDtypeStruct((ROUTE_ROWS, n), F32))
        out_specs = (row(d), row(d), rout, rout)
    return pl.pallas_call(
        kern, out_shape=out_shape, grid=(n // tm,), in_specs=in_specs, out_specs=out_specs,
        compiler_params=_cparams(("arbitrary",)), name="outproj",
    )(*args)


def _ffn_kernel(h_ref, x_ref, wg_ref, wu_ref, wd_ref, o_ref, *, chunk):
    h = h_ref[...]
    o_ref[...] = x_ref[...]
    for c in range(wg_ref.shape[1] // chunk):
        cols = slice(c * chunk, (c + 1) * chunk)
        gt = _dot(h, wg_ref[:, cols])
        up = _dot(h, wu_ref[:, cols])
        hid = (gt * _sigmoid(gt) * up).astype(BF16)
        o_ref[...] += _dot(hid, wd_ref[cols, :])


def dense_ffn(h, x, wg, wu, wd):
    n, d = x.shape
    ff = wg.shape[1]
    tm = min(TM_FFN, n)
    chunk = CHUNK_MOE
    assert ff % chunk == 0
    full = lambda shape: pl.BlockSpec(shape, lambda i: (0, 0))
    return pl.pallas_call(
        functools.partial(_ffn_kernel, chunk=chunk),
        out_shape=jax.ShapeDtypeStruct((n, d), F32),
        grid=(n // tm,),
        in_specs=[pl.BlockSpec((tm, d), lambda i: (i, 0)),
                  pl.BlockSpec((tm, d), lambda i: (i, 0)),
                  full((d, ff)), full((d, ff)), full((ff, d))],
        out_specs=pl.BlockSpec((tm, d), lambda i: (i, 0)),
        compiler_params=_cparams(("arbitrary",), VMEM_LIMIT_BIG),
        name="dense_ffn",
    )(h, x, wg, wu, wd)


def _row_copy(src_ref, src_row, dst_ref, dst_row, sem):
    return pltpu.make_async_copy(src_ref.at[pl.ds(src_row, 1)], dst_ref.at[pl.ds(dst_row, 1)], sem)


def _invert_kernel(dest_ref, src0_ref, dst0_ref, src_ref, dst_ref, sem, *, n_tok):
    init_src = pltpu.make_async_copy(src0_ref, src_ref, sem.at[0])
    init_dst = pltpu.make_async_copy(dst0_ref, dst_ref, sem.at[1])
    init_src.start()
    init_dst.start()
    init_src.wait()
    init_dst.wait()

    def put(tok, carry):
        for slot in range(TOP_K):
            row = dest_ref[TOP_K * tok + slot]
            src_ref[row] = tok
            dst_ref[row] = slot * n_tok + tok
        return carry

    lax.fori_loop(0, n_tok, put, 0, unroll=8)


def moe_invert(dest, p_rows, n_tok, bm):
    spare = TOP_K * n_tok + np.arange(p_rows + bm) % (2 * bm)
    spare[p_rows:] = TOP_K * n_tok + bm + np.arange(bm)
    src0 = jnp.zeros((p_rows + bm,), jnp.int32)
    dst0 = jnp.asarray(spare, jnp.int32)
    rows = jax.ShapeDtypeStruct((p_rows + bm,), jnp.int32)
    smem = pl.BlockSpec(memory_space=pltpu.SMEM)
    hbm = pl.BlockSpec(memory_space=pl.ANY)
    return pl.pallas_call(
        functools.partial(_invert_kernel, n_tok=n_tok),
        out_shape=(rows, rows),
        in_specs=[smem, hbm, hbm],
        out_specs=(smem, smem),
        scratch_shapes=[pltpu.SemaphoreType.DMA((2,))],
        name="moe_invert",
    )(dest, src0, dst0)


def _experts_kernel(src_ref, dst_ref, blk_e_ref, nused_ref, h_ref, wg_ref, wu_ref, wd_ref, y_ref,
                    xbuf_ref, xb_ref, acc_ref, obuf_ref, gsem, ssem, *, n_tok, chunk):
    del blk_e_ref
    bm = xb_ref.shape[0]
    tf = wg_ref.shape[2]
    nchunk = tf // chunk
    m = pl.program_id(0)
    f = pl.program_id(1)
    nb = pl.num_programs(0)
    nf = pl.num_programs(1)
    slot = m % 2
    other = 1 - slot
    n_used = nused_ref[0]
    live = m < n_used

    def gather_copy(block, s, i):
        return _row_copy(h_ref, src_ref[block * bm + i], xbuf_ref.at[s], i, gsem.at[s])

    def scatter_copy(block, s, i):
        return _row_copy(obuf_ref.at[s], i, y_ref, dst_ref[block * bm + i], ssem.at[s])

    def for_rows(fn):
        def body(i, carry):
            fn(i)
            return carry
        lax.fori_loop(0, bm, body, 0, unroll=8)

    def wait_gather(s):
        for_rows(lambda i: _row_copy(h_ref, 0, xbuf_ref.at[s], i, gsem.at[s]).wait())

    def wait_scatter(s):
        for_rows(lambda i: _row_copy(obuf_ref.at[s], i, y_ref, 0, ssem.at[s]).wait())

    def swiglu(first, issue):
        xb = xb_ref[...]
        n_groups = 3 * nchunk

        def issue_group(g):
            if issue is not None:
                for i in range(g * bm // n_groups, (g + 1) * bm // n_groups):
                    issue(i)

        for c in range(nchunk):
            cols = slice(c * chunk, (c + 1) * chunk)
            gt = _dot(xb, wg_ref[0, :, cols])
            issue_group(3 * c)
            up = _dot(xb, wu_ref[0, :, cols])
            issue_group(3 * c + 1)
            hid = (gt * _sigmoid(gt) * up).astype(BF16)
            part = _dot(hid, wd_ref[0, cols, :])
            if first and c == 0:
                acc_ref[...] = part
            else:
                acc_ref[...] += part
            issue_group(3 * c + 2)

    @pl.when(jnp.logical_and(f == 0, live))
    def _():
        @pl.when(m == 0)
        def _():
            obuf_ref[...] = jnp.zeros_like(obuf_ref)
            spare0 = pltpu.make_async_copy(obuf_ref.at[0], y_ref.at[pl.ds(TOP_K * n_tok, bm)], ssem.at[0])
            spare0.start()
            spare0.wait()
            for_rows(lambda i: gather_copy(0, 0, i).start())

        wait_gather(slot)
        xb_ref[...] = xbuf_ref[slot].astype(BF16)
        nxt = jnp.minimum(m + 1, nb - 1)
        swiglu(True, lambda i: gather_copy(nxt, other, i).start(priority=i % N_DMA_QUEUES))

    @pl.when(jnp.logical_and(jnp.logical_and(f > 0, f < nf - 1), live))
    def _():
        swiglu(False, None)

    @pl.when(jnp.logical_and(f == nf - 1, live))
    def _():
        prev = jnp.where(m == 0, nb, m - 1)
        swiglu(False, lambda i: scatter_copy(prev, other, i).start(priority=i % N_DMA_QUEUES))

        @pl.when(m >= 1)
        def _():
            wait_scatter(slot)

        obuf_ref[slot] = acc_ref[...]

        @pl.when(m == n_used - 1)
        def _():
            wait_gather(other)
            for_rows(lambda i: scatter_copy(m, slot, i).start())
            wait_scatter(slot)
            wait_scatter(other)


def moe_experts(src_rows, dst_rows, blk_expert, n_used, h, wg, wu, wd):
    n_tok, d = h.shape
    bm = BM_MOE
    p_rows = src_rows.shape[0] - bm
    ff = wg.shape[2]
    tf = TF_MOE
    nf = ff // tf
    assert nf >= 2, "the first and last hidden tile of a block carry different row copies"

    def w_idx(m, f, be, nu):
        f_eff = jnp.where(m % 2 == 0, f, nf - 1 - f)
        last = jnp.where((nu[0] - 1) % 2 == 0, nf - 1, 0)
        return be[m], jnp.where(m < nu[0], f_eff, last)

    return pl.pallas_call(
        functools.partial(_experts_kernel, n_tok=n_tok, chunk=CHUNK_MOE),
        out_shape=jax.ShapeDtypeStruct((TOP_K * n_tok + 2 * bm, d), F32),
        grid_spec=pltpu.PrefetchScalarGridSpec(
            num_scalar_prefetch=4, grid=(p_rows // bm, nf),
            in_specs=[pl.BlockSpec(memory_space=pl.ANY),
                      pl.BlockSpec((1, d, tf), lambda m, f, sr, ds, be, nu: (w_idx(m, f, be, nu)[0], 0, w_idx(m, f, be, nu)[1])),
                      pl.BlockSpec((1, d, tf), lambda m, f, sr, ds, be, nu: (w_idx(m, f, be, nu)[0], 0, w_idx(m, f, be, nu)[1])),
                      pl.BlockSpec((1, tf, d), lambda m, f, sr, ds, be, nu: (w_idx(m, f, be, nu)[0], w_idx(m, f, be, nu)[1], 0))],
            out_specs=pl.BlockSpec(memory_space=pl.ANY),
            scratch_shapes=[pltpu.VMEM((2, bm, d), F32), pltpu.VMEM((bm, d), BF16), pltpu.VMEM((bm, d), F32),
                            pltpu.VMEM((2, bm, d), F32), pltpu.SemaphoreType.DMA((2,)),
                            pltpu.SemaphoreType.DMA((2,))]),
        compiler_params=_cparams(("arbitrary", "arbitrary"), VMEM_LIMIT_BIG),
        name="moe_experts",
    )(src_rows, dst_rows, blk_expert, n_used, h, wg, wu, wd)


def _combine_kernel(x_ref, gate_ref, fg_ref, y0_ref, y1_ref, o_ref, *, final):
    gate = gate_ref[...]
    y = x_ref[...] + gate[:, 0:1] * y0_ref[...] + gate[:, 1:2] * y1_ref[...]
    if final:
        ms = jnp.mean(y * y, axis=-1, keepdims=True)
        y = (y * lax.rsqrt(ms + EPS)) * fg_ref[...]
    o_ref[...] = y


def moe_combine(x, gate, final_g, y, final):
    n, d = x.shape
    tb = min(TM_PROJ, n)
    return pl.pallas_call(
        functools.partial(_combine_kernel, final=final),
        out_shape=jax.ShapeDtypeStruct((n, d), F32),
        grid=(n // tb,),
        in_specs=[pl.BlockSpec((tb, d), lambda i: (i, 0)),
                  pl.BlockSpec((tb, TOP_K), lambda i: (i, 0)),
                  pl.BlockSpec((1, d), lambda i: (0, 0)),
                  pl.BlockSpec((tb, d), lambda i: (i, 0)),
                  pl.BlockSpec((tb, d), lambda i: (i + n // tb, 0))],
        out_specs=pl.BlockSpec((tb, d), lambda i: (i, 0)),
        compiler_params=_cparams(("arbitrary",)),
        name="moe_combine",
    )(x, gate, final_g, y, y)


def _final_norm_kernel(x_ref, g_ref, o_ref):
    x = x_ref[...]
    ms = jnp.mean(x * x, axis=-1, keepdims=True)
    o_ref[...] = (x * lax.rsqrt(ms + EPS)) * g_ref[...]


def final_norm(x, g):
    n, d = x.shape
    tm = min(TM_PROJ, n)
    return pl.pallas_call(
        _final_norm_kernel,
        out_shape=jax.ShapeDtypeStruct((n, d), F32),
        grid=(n // tm,),
        in_specs=[pl.BlockSpec((tm, d), lambda i: (i, 0)), pl.BlockSpec((1, d), lambda i: (0, 0))],
        out_specs=pl.BlockSpec((tm, d), lambda i: (i, 0)),
        compiler_params=_cparams(("arbitrary",)),
        name="final_norm",
    )(x, g)


def _moe_plan(top_idx, bm):
    n = top_idx.shape[0]
    a = n * TOP_K
    e_flat = top_idx.reshape(a)
    onehot = (e_flat[:, None] == jnp.arange(N_EXPERTS, dtype=jnp.int32)[None, :]).astype(jnp.int32)
    incl = jnp.cumsum(onehot, axis=0)
    rank = jnp.sum((incl - onehot) * onehot, axis=1)
    counts = incl[-1]
    padded = (counts + bm - 1) // bm * bm
    pad_end = jnp.cumsum(padded)
    pad_start = pad_end - padded
    dest = (jnp.sum(pad_start[None, :] * onehot, axis=1) + rank).astype(jnp.int32)
    p_rows = a + N_EXPERTS * bm
    n_blocks = p_rows // bm
    n_used = (pad_end[-1] // bm).astype(jnp.int32)
    blk_start = jnp.arange(n_blocks, dtype=jnp.int32) * bm
    blk_expert = jnp.sum((blk_start[:, None] >= pad_end[None, :]).astype(jnp.int32), axis=1)
    blk_expert = jnp.minimum(blk_expert, N_EXPERTS - 1)
    last_e = blk_expert[jnp.maximum(n_used - 1, 0)]
    blk_expert = jnp.where(jnp.arange(n_blocks) < n_used, blk_expert, last_e).astype(jnp.int32)
    return dest, blk_expert, n_used.reshape(1), p_rows


def _head_block_mask(width):
    r = jnp.arange(width) // HEAD_DIM
    return (r[:, None] == r[None, :]).astype(BF16)


def kernel(x, attn_norm, w_in, hgrn_lb, hgrn_norm, mlstm_conv, mlstm_b_i, mlstm_b_f, mlstm_norm, fox_b_f, w_out, ffn_norm, dense_w_gate, dense_w_up, dense_w_down, router, moe_w_gate, moe_w_up, moe_w_down, final_norm_g):
    batch, seq, d = x.shape
    depth = w_in.shape[0]
    n = batch * seq
    xf = x.reshape(n, d)
    m_bf = _head_block_mask(HG_W)
    n_main_a = 4 * HG_W + 3 * ML_W
    gate_a = n_main_a
    mo_a = gate_a + 2 * ML_HEADS
    fx_a = mo_a + ML_W
    ff_a = fx_a + 3 * FX_W
    done = False
    for l in range(depth):
        wl = w_in[l]
        wq_fx = wl[:, fx_a:fx_a + FX_W] * (LOG2E * HEAD_DIM ** -0.5)
        w_main = jnp.concatenate([wl[:, :n_main_a], wl[:, mo_a:fx_a], wq_fx, wl[:, fx_a + FX_W:ff_a]],
                                 axis=1).astype(BF16)
        w_gate_t = jnp.concatenate([wl[:, gate_a:mo_a], wl[:, ff_a:]], axis=1).T.astype(BF16)
        main, g_row_raw = norm_inproj(xf, attn_norm[l].reshape(1, d), w_main, w_gate_t)
        bias = jnp.concatenate([mlstm_b_i[l], mlstm_b_f[l], fox_b_f[l]]).reshape(N_GATE_ROWS, 1).astype(F32)
        g_row, g_col, c_aug = gates(g_row_raw, bias, batch)
        o_hg = hgrn2(main, hgrn_lb.astype(F32), hgrn_norm[l].reshape(1, HG_W), m_bf, batch, l)
        o_ml = mlstm(main, g_row, g_col, mlstm_conv[l], mlstm_norm[l].reshape(1, ML_W), m_bf, batch)
        o_fx = fox(main, c_aug, batch)
        wo = w_out[l].astype(BF16)
        fg = ffn_norm[l].reshape(1, d)
        j = l // 2
        if l % 2 == 0:
            x_res, h = outproj(xf, o_hg, o_ml, o_fx, wo, fg)
            xf = dense_ffn(h, x_res, dense_w_gate[j].astype(BF16), dense_w_up[j].astype(BF16),
                           dense_w_down[j].astype(BF16))
        else:
            r = jnp.pad(router[j].T, ((0, ROUTER_ROWS - N_EXPERTS), (0, 0)))
            r_hi = r.astype(BF16)
            r_lo = (r - r_hi.astype(F32)).astype(BF16)
            x_res, h, idx_rows, gate_rows = outproj(xf, o_hg, o_ml, o_fx, wo, fg, (r_hi, r_lo))
            gate = gate_rows[:TOP_K].T
            dest, blk_expert, n_used, p_rows = _moe_plan(idx_rows[:TOP_K].T, BM_MOE)
            src_rows, dst_rows = moe_invert(dest, p_rows, n, BM_MOE)
            y = moe_experts(src_rows, dst_rows, blk_expert, n_used, h, moe_w_gate[j].astype(BF16),
                            moe_w_up[j].astype(BF16), moe_w_down[j].astype(BF16))
            done = l == depth - 1
            xf = moe_combine(x_res, gate, final_norm_g.reshape(1, d), y, done)
    if not done:
        xf = final_norm(xf, final_norm_g.reshape(1, d))
    return xf.reshape(batch, seq, d)
```

```python
import functools

import jax
import jax.numpy as jnp
import numpy as np
from jax import lax
from jax.experimental import pallas as pl
from jax.experimental.pallas import tpu as pltpu

F32 = jnp.float32
BF16 = jnp.bfloat16
EPS = 1e-6
NEG_INF = float("-inf")
LOG2E = 1.4426950408889634

HEAD_DIM = 64
HG_W = 256
ML_W = 256
FX_W = 512
ML_HEADS = 4
FX_HEADS = 8
CONV_K = 4
N_EXPERTS = 8
TOP_K = 2
N_GATE_ROWS = 16
ROUTE_ROWS = 8
ROUTER_ROWS = 16

LANES = 128
VMEM_LIMIT = 48 * 1024 * 1024
VMEM_LIMIT_BIG = 56 * 1024 * 1024

TM_PROJ = 512
TM_INPROJ = 1024
T_GATE = 512
T_HG = 128
SUB_HG = 16
T_ML = 128
TQ_FX = 1024
TM_FFN = 512
BM_MOE = 512
TF_MOE = 1792
CHUNK_MOE = 256
N_DMA_QUEUES = 2


def _cparams(sem, vmem=VMEM_LIMIT):
    return pltpu.CompilerParams(dimension_semantics=sem, vmem_limit_bytes=vmem)


def _split3(x):
    hi = x.astype(BF16)
    r = x - hi.astype(F32)
    mid = r.astype(BF16)
    lo = (r - mid.astype(F32)).astype(BF16)
    return hi, mid, lo


def _dot(a, b):
    return jnp.dot(a, b, preferred_element_type=F32)


def _dot_nt(a, b):
    return lax.dot_general(a, b, (((1,), (1,)), ((), ())), preferred_element_type=F32)


def _dot_tn(a, b):
    return lax.dot_general(a, b, (((0,), (0,)), ((), ())), preferred_element_type=F32)


def _dot3(parts, b):
    return _dot(parts[0], b) + _dot(parts[1], b) + _dot(parts[2], b)


def _log_sigmoid(z):
    return -(jnp.maximum(-z, 0.0) + jnp.log1p(jnp.exp(-jnp.abs(z))))


def _sigmoid(z):
    return 1.0 / (1.0 + jnp.exp(-z))


def _head_mean_sq(o, m_bf):
    o2 = o * o
    hi = o2.astype(BF16)
    lo = (o2 - hi.astype(F32)).astype(BF16)
    return (_dot(hi, m_bf) + _dot(lo, m_bf)) * (1.0 / HEAD_DIM)


def _norm_inproj_kernel(x_ref, g_ref, w_ref, wgt_ref, main_ref, grow_ref, *, tn):
    x = x_ref[...]
    ms = jnp.mean(x * x, axis=-1, keepdims=True)
    h = ((x * lax.rsqrt(ms + EPS)) * g_ref[...]).astype(BF16)
    for j in range(w_ref.shape[1] // tn):
        main_ref[:, j * tn:(j + 1) * tn] = _dot(h, w_ref[:, j * tn:(j + 1) * tn]).astype(BF16)
    grow_ref[...] = _dot_nt(wgt_ref[...], h)


def norm_inproj(x, g, w_main, w_gate_t):
    n, d = x.shape
    wm = w_main.shape[1]
    tm = min(TM_INPROJ, n)
    return pl.pallas_call(
        functools.partial(_norm_inproj_kernel, tn=512),
        out_shape=(jax.ShapeDtypeStruct((n, wm), BF16), jax.ShapeDtypeStruct((N_GATE_ROWS, n), F32)),
        grid=(n // tm,),
        in_specs=[pl.BlockSpec((tm, d), lambda i: (i, 0)),
                  pl.BlockSpec((1, d), lambda i: (0, 0)),
                  pl.BlockSpec((d, wm), lambda i: (0, 0)),
                  pl.BlockSpec((N_GATE_ROWS, d), lambda i: (0, 0))],
        out_specs=(pl.BlockSpec((tm, wm), lambda i: (i, 0)),
                   pl.BlockSpec((N_GATE_ROWS, tm), lambda i: (0, i))),
        compiler_params=_cparams(("arbitrary",)),
        name="norm_inproj",
    )(x, g, w_main, w_gate_t)


def _gates_kernel(g_ref, bias_ref, sel_ref, grow_ref, gcol_ref, caug_ref, carry_ref):
    t = g_ref.shape[1]

    @pl.when(pl.program_id(1) == 0)
    def _():
        carry_ref[...] = jnp.zeros_like(carry_ref)

    z = g_ref[...] + bias_ref[...]
    row = lax.broadcasted_iota(jnp.int32, z.shape, 0)
    is_input_gate = row < ML_HEADS
    val = jnp.where(is_input_gate, 0.0, _log_sigmoid(z))
    r_i = lax.broadcasted_iota(jnp.int32, (t, t), 0)
    c_i = lax.broadcasted_iota(jnp.int32, (t, t), 1)
    upper = jnp.where(r_i <= c_i, 1.0, 0.0).astype(BF16)
    tot = _dot3(_split3(val), upper) + carry_ref[:, 0:1]
    out = jnp.where(is_input_gate, z, tot)
    grow_ref[...] = out
    carry_ref[...] = jnp.broadcast_to(tot[:, t - 1:t], carry_ref.shape)
    eye = jnp.where(r_i == c_i, 1.0, 0.0).astype(BF16)
    p0, p1, p2 = _split3(out)
    gcol_ref[...] = _dot_nt(eye, p0) + _dot_nt(eye, p1) + _dot_nt(eye, p2)
    n0, n1, n2 = _split3(out * (-LOG2E))
    zrows = _dot(sel_ref[0], n0) + _dot(sel_ref[1], n1) + _dot(sel_ref[2], n2)
    caug_ref[...] = _dot_nt(eye, zrows.astype(BF16)).astype(BF16)


def _bias_lane_selectors():
    sel = [[[0.0] * N_GATE_ROWS for _ in range(FX_HEADS // 2 * LANES)] for _ in range(3)]
    for j in range(3):
        for p in range(FX_HEADS // 2):
            for a in range(2):
                sel[j][LANES * p + 3 * a + j][2 * ML_HEADS + 2 * p + a] = 1.0
    return jnp.asarray(sel, BF16)


def gates(g_row, bias, batch):
    r, n = g_row.shape
    s = n // batch
    t = min(T_GATE, s)
    nb = s // t
    sel = _bias_lane_selectors()
    wc = sel.shape[1]
    return pl.pallas_call(
        _gates_kernel,
        out_shape=(jax.ShapeDtypeStruct((r, n), F32), jax.ShapeDtypeStruct((n, r), F32),
                   jax.ShapeDtypeStruct((n, wc), BF16)),
        grid=(batch, nb),
        in_specs=[pl.BlockSpec((r, t), lambda b, j: (0, b * nb + j)),
                  pl.BlockSpec((r, 1), lambda b, j: (0, 0)),
                  pl.BlockSpec(sel.shape, lambda b, j: (0, 0, 0))],
        out_specs=(pl.BlockSpec((r, t), lambda b, j: (0, b * nb + j)),
                   pl.BlockSpec((t, r), lambda b, j: (b * nb + j, 0)),
                   pl.BlockSpec((t, wc), lambda b, j: (b * nb + j, 0))),
        scratch_shapes=[pltpu.VMEM((r, LANES), F32)],
        compiler_params=_cparams(("arbitrary", "arbitrary")),
        name="gates",
    )(g_row, bias, sel)


def _hgrn_kernel(q_ref, f_ref, i_ref, g_ref, lb_ref, gain_ref, m_ref, o_ref, st_ref, x_ref, y_ref, *, layer):
    @pl.when(pl.program_id(0) == 0)
    def _():
        st_ref[...] = jnp.zeros_like(st_ref)

    for b in range(q_ref.shape[0]):
        _hgrn_rows(q_ref.at[b], f_ref.at[b], i_ref.at[b], g_ref.at[b], lb_ref, gain_ref, m_ref, o_ref.at[b],
                   st_ref.at[b], x_ref.at[b], y_ref.at[b], layer=layer)


def _hgrn_rows(q_ref, f_ref, i_ref, g_ref, lb_ref, gain_ref, m_ref, o_ref, st_ref, x_ref, y_ref, *, layer):
    t = q_ref.shape[0]
    sub = SUB_HG
    nsub = t // sub

    lbp = lb_ref[...]
    rows = [lbp[r:r + 1, :] for r in range(lbp.shape[0])]
    mx = functools.reduce(jnp.maximum, rows)
    es = [jnp.exp(r - mx) for r in rows]
    tot = functools.reduce(lambda a, b: a + b, es)
    cs, run = [], None
    for e in es:
        run = e / tot if run is None else run + e / tot
        cs.append(run)
    lb = cs[layer] - cs[0]

    z = f_ref[...].astype(F32)
    a = jnp.log(lb)
    bb = jnp.log1p(-lb) + _log_sigmoid(z)
    log_f = jnp.maximum(a, bb) + jnp.log1p(jnp.exp(-jnp.abs(a - bb)))
    k = (1.0 - lb) * _sigmoid(-z)
    q = q_ref[...].astype(F32)
    v = i_ref[...].astype(F32)
    m_bf = m_ref[...]

    r_i = lax.broadcasted_iota(jnp.int32, (t, t), 0)
    c_i = lax.broadcasted_iota(jnp.int32, (t, t), 1)
    same_sub = (c_i // sub) == (r_i // sub)
    lower = jnp.where(jnp.logical_and(c_i <= r_i, same_sub), 1.0, 0.0).astype(BF16)
    f0, f1, f2 = _split3(log_f)
    b = _dot(lower, f0) + _dot(lower, f1) + _dot(lower, f2)
    b2 = b * LOG2E
    m_f32 = m_bf.astype(F32)
    t_in_sub = lax.broadcasted_iota(jnp.int32, (sub, q.shape[1]), 0)

    st = st_ref[...]
    o_inter = []
    for i in range(nsub):
        rows = slice(i * sub, (i + 1) * sub)
        bi, b2i, qi, ki = b[rows], b2[rows], q[rows], k[rows]
        for s in range(sub):
            diff = b2i - b2i[s:s + 1, :]
            if s > 0:
                diff = jnp.where(t_in_sub >= s, diff, NEG_INF)
            base = (i * sub + s) * sub
            x_ref[base:base + sub, :] = (qi * (ki[s:s + 1, :] * jnp.exp2(diff))).astype(BF16)
        o_inter.append(_dot_nt((qi * jnp.exp(bi)).astype(BF16), st.astype(BF16)))
        b_end = bi[sub - 1:sub, :]
        kd = ki * jnp.exp(b_end - bi)
        st = st * jnp.exp(b_end) + _dot_tn(v[rows].astype(BF16), kd.astype(BF16)) * m_f32
    st_ref[...] = st
    y_ref[...] = _dot(x_ref[...], m_bf)
    outs = []
    for i in range(nsub):
        vi = v[i * sub:(i + 1) * sub]
        acc = o_inter[i]
        for s in range(sub):
            base = (i * sub + s) * sub
            acc = acc + y_ref[base:base + sub, :] * vi[s:s + 1, :]
        outs.append(acc)
    o = jnp.concatenate(outs, axis=0)

    gt = g_ref[...].astype(F32)
    y = o * lax.rsqrt(_head_mean_sq(o, m_bf) + EPS) * gain_ref[...] * (gt * _sigmoid(gt))
    o_ref[...] = y.astype(o_ref.dtype)


def hgrn2(main, lb_all, gain, m_bf, batch, layer):
    n = main.shape[0]
    s = n // batch
    t = T_HG
    nc = s // t
    w = HG_W
    n_pairs = t * SUB_HG
    main3 = main.reshape(batch, s, main.shape[1])
    col = lambda cidx: pl.BlockSpec((batch, t, w), lambda c: (0, c, cidx))
    full = lambda shape: pl.BlockSpec(shape, lambda c: (0, 0))
    out = pl.pallas_call(
        functools.partial(_hgrn_kernel, layer=layer),
        out_shape=jax.ShapeDtypeStruct((batch, s, w), BF16),
        grid=(nc,),
        in_specs=[col(0), col(1), col(2), col(3), full(lb_all.shape), full((1, w)), full((w, w))],
        out_specs=pl.BlockSpec((batch, t, w), lambda c: (0, c, 0)),
        scratch_shapes=[pltpu.VMEM((batch, w, w), F32), pltpu.VMEM((batch, n_pairs, w), BF16),
                        pltpu.VMEM((batch, n_pairs, w), F32)],
        compiler_params=_cparams(("arbitrary",)),
        name="hgrn2",
    )(main3, main3, main3, main3, lb_all, gain, m_bf)
    return out.reshape(n, w)


def _mlstm_kernel(q_ref, k_ref, v_ref, og_ref, grow_ref, gcol_ref, cw_ref, gain_ref, m_ref, o_ref,
                  ext_ref, ct_ref, n_ref, mm_ref):
    t = q_ref.shape[0]
    w = q_ref.shape[1]
    halo = 8

    @pl.when(pl.program_id(1) == 0)
    def _():
        ext_ref[0:halo, :] = jnp.zeros((halo, 2 * w), F32)
        ct_ref[...] = jnp.zeros_like(ct_ref)
        n_ref[...] = jnp.zeros_like(n_ref)
        mm_ref[...] = jnp.zeros_like(mm_ref)

    ext_ref[halo:halo + t, 0:w] = q_ref[...].astype(F32)
    ext_ref[halo:halo + t, w:2 * w] = k_ref[...].astype(F32)
    cw = cw_ref[...]
    y = None
    for j in range(CONV_K):
        term = ext_ref[halo - (CONV_K - 1) + j:halo - (CONV_K - 1) + j + t, :] * cw[j:j + 1, :]
        y = term if y is None else y + term
    tail = ext_ref[t:t + halo, :]
    ext_ref[0:halo, :] = tail
    qk = y * _sigmoid(y)
    q = qk[:, 0:w]
    k = qk[:, w:2 * w] * (HEAD_DIM ** -0.5)
    kb = k.astype(BF16)
    vb = v_ref[...]
    m_bf = m_ref[...]

    grow = grow_ref[...]
    gcol = gcol_ref[...]
    lane_head = lax.broadcasted_iota(jnp.int32, (1, w), 1) // HEAD_DIM
    r_i = lax.broadcasted_iota(jnp.int32, (t, t), 0)
    c_i = lax.broadcasted_iota(jnp.int32, (t, t), 1)
    causal = c_i <= r_i

    num_intra = jnp.zeros((t, w), F32)
    sint_l = jnp.zeros((t, w), F32)
    wsum_l = jnp.zeros((t, w), F32)
    mt_l = jnp.zeros((t, w), F32)
    wk_l = jnp.zeros((t, w), F32)
    decay_l = jnp.zeros((1, w), F32)
    for h in range(ML_HEADS):
        sel = lane_head == h
        qh = jnp.where(sel, q, 0.0).astype(BF16)
        s = _dot_nt(qh, kb)
        bc = gcol[:, ML_HEADS + h:ML_HEADS + h + 1]
        br = grow[ML_HEADS + h:ML_HEADS + h + 1, :]
        lir = grow[h:h + 1, :]
        lic = gcol[:, h:h + 1]
        dlog = jnp.where(causal, bc - br + lir, NEG_INF)
        mmh = mm_ref[h:h + 1, 0:1]
        inter = bc + mmh
        m_t = jnp.maximum(jnp.max(dlog, axis=1, keepdims=True), inter)
        wgt = s * jnp.exp(dlog - m_t)
        s_int = jnp.exp(inter - m_t)
        pv = _dot(wgt.astype(BF16), vb)
        num_intra = jnp.where(sel, pv, num_intra)
        sint_l = jnp.where(sel, s_int, sint_l)
        wsum_l = jnp.where(sel, jnp.sum(wgt, axis=1, keepdims=True), wsum_l)
        mt_l = jnp.where(sel, m_t, mt_l)
        b_end = br[:, t - 1:t]
        m_new = jnp.maximum(b_end + mmh, jnp.max(b_end - br + lir, axis=1, keepdims=True))
        wk_l = jnp.where(sel, jnp.exp(b_end - bc + lic - m_new), wk_l)
        decay_l = jnp.where(sel, jnp.exp(b_end + mmh - m_new), decay_l)
        mm_ref[h:h + 1, :] = jnp.broadcast_to(m_new - b_end, (1, mm_ref.shape[1]))

    ct = ct_ref[...]
    nrow = n_ref[0:1, :]
    q_c = _dot_nt(q.astype(BF16), ct.astype(BF16))
    qn = q * nrow
    qn_hi = qn.astype(BF16)
    qn_lo = (qn - qn_hi.astype(F32)).astype(BF16)
    qn_l = _dot(qn_hi, m_bf) + _dot(qn_lo, m_bf)
    num = num_intra + sint_l * q_c
    den = wsum_l + sint_l * qn_l
    hval = num / jnp.maximum(jnp.abs(den), jnp.exp(-mt_l))

    kw = k * wk_l
    upd = _dot_tn(vb, kw.astype(BF16))
    ct_ref[...] = decay_l * ct + upd * m_bf.astype(F32)
    n_ref[...] = jnp.broadcast_to(decay_l * nrow + jnp.sum(kw, axis=0, keepdims=True), n_ref.shape)

    og = og_ref[...].astype(F32)
    yv = hval * lax.rsqrt(_head_mean_sq(hval, m_bf) + EPS) * gain_ref[...] * _sigmoid(og)
    o_ref[...] = yv.astype(o_ref.dtype)


def mlstm(main, g_row, g_col, conv_w, gain, m_bf, batch):
    n = main.shape[0]
    s = n // batch
    t = min(T_ML, s)
    nc = s // t
    w = ML_W
    col = lambda cidx: pl.BlockSpec((t, w), lambda b, c: (b * nc + c, cidx))
    full = lambda shape: pl.BlockSpec(shape, lambda b, c: (0, 0))
    return pl.pallas_call(
        _mlstm_kernel,
        out_shape=jax.ShapeDtypeStruct((n, w), BF16),
        grid=(batch, nc),
        in_specs=[col(4), col(5), col(6), col(7),
                  pl.BlockSpec((N_GATE_ROWS, t), lambda b, c: (0, b * nc + c)),
                  pl.BlockSpec((t, N_GATE_ROWS), lambda b, c: (b * nc + c, 0)),
                  full((CONV_K, 2 * w)), full((1, w)), full((w, w))],
        out_specs=pl.BlockSpec((t, w), lambda b, c: (b * nc + c, 0)),
        scratch_shapes=[pltpu.VMEM((t + 8, 2 * w), F32), pltpu.VMEM((w, w), F32),
                        pltpu.VMEM((8, w), F32), pltpu.VMEM((8, LANES), F32)],
        compiler_params=_cparams(("arbitrary", "arbitrary")),
        name="mlstm",
    )(main, main, main, main, g_row, g_col, conv_w, gain, m_bf)


def _fox_kernel(q_ref, k_ref, v_ref, c_ref, o_ref, m_ref, acc_ref):
    tq = q_ref.shape[0]
    half = tq // 2
    qi = pl.program_id(2)
    lane = lax.broadcasted_iota(jnp.int32, (1, LANES), 1)
    first = lane < HEAD_DIM

    m_ref[...] = jnp.full(m_ref.shape, NEG_INF, F32)
    acc_ref[...] = jnp.zeros_like(acc_ref)

    def attend(r0, r1, kstart, klen, causal_shift):
        q2 = q_ref[r0:r1, :]
        v2 = v_ref[pl.ds(kstart, klen), :]
        k_aug = jnp.concatenate([k_ref[pl.ds(kstart, klen), :], c_ref[pl.ds(kstart, klen), :]], axis=1)
        for a in range(2):
            sel = first if a == 0 else jnp.logical_not(first)
            ones_lanes = jnp.logical_and(lane >= 3 * a, lane < 3 * a + 3)
            q_bias = jnp.broadcast_to(jnp.where(ones_lanes, 1.0, 0.0).astype(q2.dtype), q2.shape)
            q_aug = jnp.concatenate([jnp.where(sel, q2, jnp.zeros_like(q2)), q_bias], axis=1)
            s = _dot_nt(q_aug, k_aug)
            if causal_shift is not None:
                r_i = lax.broadcasted_iota(jnp.int32, s.shape, 0)
                c_i = lax.broadcasted_iota(jnp.int32, s.shape, 1)
                s = jnp.where(c_i <= r_i + causal_shift, s, NEG_INF)
            m_prev = m_ref[a, r0:r1, :]
            m_new = jnp.maximum(m_prev, jnp.max(s, axis=1, keepdims=True))
            p = jnp.concatenate([jnp.exp2(s[:, c * LANES:(c + 1) * LANES] - m_new).astype(v2.dtype)
                                 for c in range(klen // LANES)], axis=1)
            v_aug = jnp.where(sel, v2, jnp.ones_like(v2))
            acc_ref[a, r0:r1, :] = jnp.exp2(m_prev - m_new) * acc_ref[a, r0:r1, :] + _dot(p, v_aug)
            m_ref[a, r0:r1, :] = m_new

    def past_pair(j, carry):
        attend(0, tq, pl.multiple_of(2 * j * tq, tq), tq, None)
        attend(0, tq, pl.multiple_of((2 * j + 1) * tq, tq), tq, None)
        return carry

    lax.fori_loop(0, qi // 2, past_pair, 0)

    @pl.when(qi % 2 == 1)
    def _():
        attend(0, tq, pl.multiple_of((qi - 1) * tq, tq), tq, None)

    d0 = pl.multiple_of(qi * tq, tq)
    attend(0, half, d0, half, 0)
    attend(half, tq, d0, tq, half)

    acc_a = acc_ref[0]
    acc_b = acc_ref[1]
    out = jnp.where(first, acc_a / pltpu.roll(acc_a, HEAD_DIM, 1), acc_b / pltpu.roll(acc_b, HEAD_DIM, 1))
    o_ref[...] = out.astype(o_ref.dtype)


def fox(main, c_aug, batch):
    n = main.shape[0]
    s = n // batch
    tq = min(TQ_FX, s)
    nq = s // tq
    pairs = FX_HEADS // 2
    qcol, kcol, vcol = 2048 // LANES, 2560 // LANES, 3072 // LANES
    seq_blk = lambda col0: pl.BlockSpec((s, LANES), lambda b, p, i: (b, col0 + p))
    return pl.pallas_call(
        _fox_kernel,
        out_shape=jax.ShapeDtypeStruct((n, FX_W), BF16),
        grid=(batch, pairs, nq),
        in_specs=[pl.BlockSpec((tq, LANES), lambda b, p, i: (b * nq + i, qcol + p)),
                  seq_blk(kcol), seq_blk(vcol), seq_blk(0)],
        out_specs=pl.BlockSpec((tq, LANES), lambda b, p, i: (b * nq + i, p)),
        scratch_shapes=[pltpu.VMEM((2, tq, LANES), F32), pltpu.VMEM((2, tq, LANES), F32)],
        compiler_params=_cparams(("arbitrary", "arbitrary", "arbitrary")),
        name="fox",
    )(main, main, main, c_aug)


def _outproj_body(x_ref, ohg_ref, oml_ref, ofx_ref, w_ref, g_ref):
    acc = x_ref[...]
    acc = acc + _dot(ohg_ref[...], w_ref[0:HG_W, :])
    acc = acc + _dot(oml_ref[...], w_ref[HG_W:HG_W + ML_W, :])
    acc = acc + _dot(ofx_ref[...], w_ref[HG_W + ML_W:, :])
    ms = jnp.mean(acc * acc, axis=-1, keepdims=True)
    h = (acc * lax.rsqrt(ms + EPS)) * g_ref[...]
    return acc, h


def _outproj_dense_kernel(x_ref, ohg_ref, oml_ref, ofx_ref, w_ref, g_ref, xo_ref, h_ref):
    acc, h = _outproj_body(x_ref, ohg_ref, oml_ref, ofx_ref, w_ref, g_ref)
    xo_ref[...] = acc
    h_ref[...] = h.astype(h_ref.dtype)


def _outproj_moe_kernel(x_ref, ohg_ref, oml_ref, ofx_ref, w_ref, g_ref, rhi_ref, rlo_ref,
                        xo_ref, h_ref, idx_ref, gate_ref):
    acc, h = _outproj_body(x_ref, ohg_ref, oml_ref, ofx_ref, w_ref, g_ref)
    xo_ref[...] = acc
    h_ref[...] = h
    h_hi = h.astype(BF16)
    h_lo = (h - h_hi.astype(F32)).astype(BF16)
    logits = _dot_nt(rhi_ref[...], h_hi) + _dot_nt(rlo_ref[...], h_hi) + _dot_nt(rhi_ref[...], h_lo)
    row_i = lax.broadcasted_iota(jnp.int32, logits.shape, 0)
    row = row_i.astype(F32)
    n_rows = float(logits.shape[0])
    lg = jnp.where(row_i < N_EXPERTS, logits, NEG_INF)
    m1 = jnp.max(lg, axis=0, keepdims=True)
    i1 = jnp.min(jnp.where(lg == m1, row, n_rows), axis=0, keepdims=True)
    lg2 = jnp.where(row == i1, NEG_INF, lg)
    m2 = jnp.max(lg2, axis=0, keepdims=True)
    i2 = jnp.min(jnp.where(lg2 == m2, row, n_rows), axis=0, keepdims=True)
    e = jnp.exp(m2 - m1)
    g1 = 1.0 / (1.0 + e)
    g2 = e / (1.0 + e)
    out_row = lax.broadcasted_iota(jnp.int32, idx_ref.shape, 0)
    idx_ref[...] = jnp.where(out_row == 0, i1, jnp.where(out_row == 1, i2, 0.0)).astype(jnp.int32)
    gate_ref[...] = jnp.where(out_row == 0, g1, jnp.where(out_row == 1, g2, 0.0))


def outproj(x, o_hg, o_ml, o_fx, w_out, g, router_parts=None):
    n, d = x.shape
    tm = min(TM_PROJ, n)
    row = lambda width: pl.BlockSpec((tm, width), lambda i: (i, 0))
    full = lambda shape: pl.BlockSpec(shape, lambda i: (0, 0))
    in_specs = [row(d), row(HG_W), row(ML_W), row(FX_W), full(w_out.shape), full((1, d))]
    args = [x, o_hg, o_ml, o_fx, w_out, g]
    if router_parts is None:
        kern = _outproj_dense_kernel
        out_shape = (jax.ShapeDtypeStruct((n, d), F32), jax.ShapeDtypeStruct((n, d), BF16))
        out_specs = (row(d), row(d))
    else:
        kern = _outproj_moe_kernel
        in_specs += [full(router_parts[0].shape), full(router_parts[1].shape)]
        args += list(router_parts)
        rout = pl.BlockSpec((ROUTE_ROWS, tm), lambda i: (0, i))
        out_shape = (jax.ShapeDtypeStruct((n, d), F32), jax.ShapeDtypeStruct((n, d), F32),
                     jax.ShapeDtypeStruct((ROUTE_ROWS, n), jnp.int32), jax.ShapeDtypeStruct((ROUTE_ROWS, n), F32))
        out_specs = (row(d), row(d), rout, rout)
    return pl.pallas_call(
        kern, out_shape=out_shape, grid=(n // tm,), in_specs=in_specs, out_specs=out_specs,
        compiler_params=_cparams(("arbitrary",)), name="outproj",
    )(*args)


def _ffn_kernel(h_ref, x_ref, wg_ref, wu_ref, wd_ref, o_ref, *, chunk):
    h = h_ref[...]
    o_ref[...] = x_ref[...]
    for c in range(wg_ref.shape[1] // chunk):
        cols = slice(c * chunk, (c + 1) * chunk)
        gt = _dot(h, wg_ref[:, cols])
        up = _dot(h, wu_ref[:, cols])
        hid = (gt * _sigmoid(gt) * up).astype(BF16)
        o_ref[...] += _dot(hid, wd_ref[cols, :])


def dense_ffn(h, x, wg, wu, wd):
    n, d = x.shape
    ff = wg.shape[1]
    tm = min(TM_FFN, n)
    chunk = CHUNK_MOE
    assert ff % chunk == 0
    full = lambda shape: pl.BlockSpec(shape, lambda i: (0, 0))
    return pl.pallas_call(
        functools.partial(_ffn_kernel, chunk=chunk),
        out_shape=jax.ShapeDtypeStruct((n, d), F32),
        grid=(n // tm,),
        in_specs=[pl.BlockSpec((tm, d), lambda i: (i, 0)),
                  pl.BlockSpec((tm, d), lambda i: (i, 0)),
                  full((d, ff)), full((d, ff)), full((ff, d))],
        out_specs=pl.BlockSpec((tm, d), lambda i: (i, 0)),
        compiler_params=_cparams(("arbitrary",), VMEM_LIMIT_BIG),
        name="dense_ffn",
    )(h, x, wg, wu, wd)


def _row_copy(src_ref, src_row, dst_ref, dst_row, sem):
    return pltpu.make_async_copy(src_ref.at[pl.ds(src_row, 1)], dst_ref.at[pl.ds(dst_row, 1)], sem)


def _invert_kernel(dest_ref, src0_ref, dst0_ref, src_ref, dst_ref, sem, *, n_tok):
    init_src = pltpu.make_async_copy(src0_ref, src_ref, sem.at[0])
    init_dst = pltpu.make_async_copy(dst0_ref, dst_ref, sem.at[1])
    init_src.start()
    init_dst.start()
    init_src.wait()
    init_dst.wait()

    def put(tok, carry):
        for slot in range(TOP_K):
            row = dest_ref[TOP_K * tok + slot]
            src_ref[row] = tok
            dst_ref[row] = slot * n_tok + tok
        return carry

    lax.fori_loop(0, n_tok, put, 0, unroll=8)


def moe_invert(dest, p_rows, n_tok, bm):
    spare = TOP_K * n_tok + np.arange(p_rows + bm) % (2 * bm)
    spare[p_rows:] = TOP_K * n_tok + bm + np.arange(bm)
    src0 = jnp.zeros((p_rows + bm,), jnp.int32)
    dst0 = jnp.asarray(spare, jnp.int32)
    rows = jax.ShapeDtypeStruct((p_rows + bm,), jnp.int32)
    smem = pl.BlockSpec(memory_space=pltpu.SMEM)
    hbm = pl.BlockSpec(memory_space=pl.ANY)
    return pl.pallas_call(
        functools.partial(_invert_kernel, n_tok=n_tok),
        out_shape=(rows, rows),
        in_specs=[smem, hbm, hbm],
        out_specs=(smem, smem),
        scratch_shapes=[pltpu.SemaphoreType.DMA((2,))],
        name="moe_invert",
    )(dest, src0, dst0)


def _experts_kernel(src_ref, dst_ref, blk_e_ref, nused_ref, h_ref, wg_ref, wu_ref, wd_ref, y_ref,
                    xbuf_ref, xb_ref, acc_ref, obuf_ref, gsem, ssem, *, n_tok, chunk):
    del blk_e_ref
    bm = xb_ref.shape[0]
    tf = wg_ref.shape[2]
    nchunk = tf // chunk
    m = pl.program_id(0)
    f = pl.program_id(1)
    nb = pl.num_programs(0)
    nf = pl.num_programs(1)
    slot = m % 2
    other = 1 - slot
    n_used = nused_ref[0]
    live = m < n_used

    def gather_copy(block, s, i):
        return _row_copy(h_ref, src_ref[block * bm + i], xbuf_ref.at[s], i, gsem.at[s])

    def scatter_copy(block, s, i):
        return _row_copy(obuf_ref.at[s], i, y_ref, dst_ref[block * bm + i], ssem.at[s])

    def for_rows(fn):
        def body(i, carry):
            fn(i)
            return carry
        lax.fori_loop(0, bm, body, 0, unroll=8)

    def wait_gather(s):
        for_rows(lambda i: _row_copy(h_ref, 0, xbuf_ref.at[s], i, gsem.at[s]).wait())

    def wait_scatter(s):
        for_rows(lambda i: _row_copy(obuf_ref.at[s], i, y_ref, 0, ssem.at[s]).wait())

    def swiglu(first, issue):
        xb = xb_ref[...]
        n_groups = 3 * nchunk

        def issue_group(g):
            if issue is not None:
                for i in range(g * bm // n_groups, (g + 1) * bm // n_groups):
                    issue(i)

        for c in range(nchunk):
            cols = slice(c * chunk, (c + 1) * chunk)
            gt = _dot(xb, wg_ref[0, :, cols])
            issue_group(3 * c)
            up = _dot(xb, wu_ref[0, :, cols])
            issue_group(3 * c + 1)
            hid = (gt * _sigmoid(gt) * up).astype(BF16)
            part = _dot(hid, wd_ref[0, cols, :])
            if first and c == 0:
                acc_ref[...] = part
            else:
                acc_ref[...] += part
            issue_group(3 * c + 2)

    @pl.when(jnp.logical_and(f == 0, live))
    def _():
        @pl.when(m == 0)
        def _():
            obuf_ref[...] = jnp.zeros_like(obuf_ref)
            spare0 = pltpu.make_async_copy(obuf_ref.at[0], y_ref.at[pl.ds(TOP_K * n_tok, bm)], ssem.at[0])
            spare0.start()
            spare0.wait()
            for_rows(lambda i: gather_copy(0, 0, i).start())

        wait_gather(slot)
        xb_ref[...] = xbuf_ref[slot].astype(BF16)
        nxt = jnp.minimum(m + 1, nb - 1)
        swiglu(True, lambda i: gather_copy(nxt, other, i).start(priority=i % N_DMA_QUEUES))

    @pl.when(jnp.logical_and(jnp.logical_and(f > 0, f < nf - 1), live))
    def _():
        swiglu(False, None)

    @pl.when(jnp.logical_and(f == nf - 1, live))
    def _():
        prev = jnp.where(m == 0, nb, m - 1)
        swiglu(False, lambda i: scatter_copy(prev, other, i).start(priority=i % N_DMA_QUEUES))

        @pl.when(m >= 1)
        def _():
            wait_scatter(slot)

        obuf_ref[slot] = acc_ref[...]

        @pl.when(m == n_used - 1)
        def _():
            wait_gather(other)
            for_rows(lambda i: scatter_copy(m, slot, i).start())
            wait_scatter(slot)
            wait_scatter(other)


def moe_experts(src_rows, dst_rows, blk_expert, n_used, h, wg, wu, wd):
    n_tok, d = h.shape
    bm = BM_MOE
    p_rows = src_rows.shape[0] - bm
    ff = wg.shape[2]
    tf = TF_MOE
    nf = ff // tf
    assert nf >= 2, "the first and last hidden tile of a block carry different row copies"

    def w_idx(m, f, be, nu):
        f_eff = jnp.where(m % 2 == 0, f, nf - 1 - f)
        last = jnp.where((nu[0] - 1) % 2 == 0, nf - 1, 0)
        return be[m], jnp.where(m < nu[0], f_eff, last)

    return pl.pallas_call(
        functools.partial(_experts_kernel, n_tok=n_tok, chunk=CHUNK_MOE),
        out_shape=jax.ShapeDtypeStruct((TOP_K * n_tok + 2 * bm, d), F32),
        grid_spec=pltpu.PrefetchScalarGridSpec(
            num_scalar_prefetch=4, grid=(p_rows // bm, nf),
            in_specs=[pl.BlockSpec(memory_space=pl.ANY),
                      pl.BlockSpec((1, d, tf), lambda m, f, sr, ds, be, nu: (w_idx(m, f, be, nu)[0], 0, w_idx(m, f, be, nu)[1])),
                      pl.BlockSpec((1, d, tf), lambda m, f, sr, ds, be, nu: (w_idx(m, f, be, nu)[0], 0, w_idx(m, f, be, nu)[1])),
                      pl.BlockSpec((1, tf, d), lambda m, f, sr, ds, be, nu: (w_idx(m, f, be, nu)[0], w_idx(m, f, be, nu)[1], 0))],
            out_specs=pl.BlockSpec(memory_space=pl.ANY),
            scratch_shapes=[pltpu.VMEM((2, bm, d), F32), pltpu.VMEM((bm, d), BF16), pltpu.VMEM((bm, d), F32),
                            pltpu.VMEM((2, bm, d), F32), pltpu.SemaphoreType.DMA((2,)),
                            pltpu.SemaphoreType.DMA((2,))]),
        compiler_params=_cparams(("arbitrary", "arbitrary"), VMEM_LIMIT_BIG),
        name="moe_experts",
    )(src_rows, dst_rows, blk_expert, n_used, h, wg, wu, wd)


def _combine_kernel(x_ref, gate_ref, fg_ref, y0_ref, y1_ref, o_ref, *, final):
    gate = gate_ref[...]
    y = x_ref[...] + gate[:, 0:1] * y0_ref[...] + gate[:, 1:2] * y1_ref[...]
    if final:
        ms = jnp.mean(y * y, axis=-1, keepdims=True)
        y = (y * lax.rsqrt(ms + EPS)) * fg_ref[...]
    o_ref[...] = y


def moe_combine(x, gate, final_g, y, final):
    n, d = x.shape
    tb = min(TM_PROJ, n)
    return pl.pallas_call(
        functools.partial(_combine_kernel, final=final),
        out_shape=jax.ShapeDtypeStruct((n, d), F32),
        grid=(n // tb,),
        in_specs=[pl.BlockSpec((tb, d), lambda i: (i, 0)),
                  pl.BlockSpec((tb, TOP_K), lambda i: (i, 0)),
                  pl.BlockSpec((1, d), lambda i: (0, 0)),
                  pl.BlockSpec((tb, d), lambda i: (i, 0)),
                  pl.BlockSpec((tb, d), lambda i: (i + n // tb, 0))],
        out_specs=pl.BlockSpec((tb, d), lambda i: (i, 0)),
        compiler_params=_cparams(("arbitrary",)),
        name="moe_combine",
    )(x, gate, final_g, y, y)


def _final_norm_kernel(x_ref, g_ref, o_ref):
    x = x_ref[...]
    ms = jnp.mean(x * x, axis=-1, keepdims=True)
    o_ref[...] = (x * lax.rsqrt(ms + EPS)) * g_ref[...]


def final_norm(x, g):
    n, d = x.shape
    tm = min(TM_PROJ, n)
    return pl.pallas_call(
        _final_norm_kernel,
        out_shape=jax.ShapeDtypeStruct((n, d), F32),
        grid=(n // tm,),
        in_specs=[pl.BlockSpec((tm, d), lambda i: (i, 0)), pl.BlockSpec((1, d), lambda i: (0, 0))],
        out_specs=pl.BlockSpec((tm, d), lambda i: (i, 0)),
        compiler_params=_cparams(("arbitrary",)),
        name="final_norm",
    )(x, g)


def _moe_plan(top_idx, bm):
    n = top_idx.shape[0]
    a = n * TOP_K
    e_flat = top_idx.reshape(a)
    onehot = (e_flat[:, None] == jnp.arange(N_EXPERTS, dtype=jnp.int32)[None, :]).astype(jnp.int32)
    incl = jnp.cumsum(onehot, axis=0)
    rank = jnp.sum((incl - onehot) * onehot, axis=1)
    counts = incl[-1]
    padded = (counts + bm - 1) // bm * bm
    pad_end = jnp.cumsum(padded)
    pad_start = pad_end - padded
    dest = (jnp.sum(pad_start[None, :] * onehot, axis=1) + rank).astype(jnp.int32)
    p_rows = a + N_EXPERTS * bm
    n_blocks = p_rows // bm
    n_used = (pad_end[-1] // bm).astype(jnp.int32)
    blk_start = jnp.arange(n_blocks, dtype=jnp.int32) * bm
    blk_expert = jnp.sum((blk_start[:, None] >= pad_end[None, :]).astype(jnp.int32), axis=1)
    blk_expert = jnp.minimum(blk_expert, N_EXPERTS - 1)
    last_e = blk_expert[jnp.maximum(n_used - 1, 0)]
    blk_expert = jnp.where(jnp.arange(n_blocks) < n_used, blk_expert, last_e).astype(jnp.int32)
    return dest, blk_expert, n_used.reshape(1), p_rows


def _head_block_mask(width):
    r = jnp.arange(width) // HEAD_DIM
    return (r[:, None] == r[None, :]).astype(BF16)


def kernel(x, attn_norm, w_in, hgrn_lb, hgrn_norm, mlstm_conv, mlstm_b_i, mlstm_b_f, mlstm_norm, fox_b_f, w_out, ffn_norm, dense_w_gate, dense_w_up, dense_w_down, router, moe_w_gate, moe_w_up, moe_w_down, final_norm_g):
    batch, seq, d = x.shape
    depth = w_in.shape[0]
    n = batch * seq
    xf = x.reshape(n, d)
    m_bf = _head_block_mask(HG_W)
    n_main_a = 4 * HG_W + 3 * ML_W
    gate_a = n_main_a
    mo_a = gate_a + 2 * ML_HEADS
    fx_a = mo_a + ML_W
    ff_a = fx_a + 3 * FX_W
    done = False
    for l in range(depth):
        wl = w_in[l]
        wq_fx = wl[:, fx_a:fx_a + FX_W] * (LOG2E * HEAD_DIM ** -0.5)
        w_main = jnp.concatenate([wl[:, :n_main_a], wl[:, mo_a:fx_a], wq_fx, wl[:, fx_a + FX_W:ff_a]],
                                 axis=1).astype(BF16)
        w_gate_t = jnp.concatenate([wl[:, gate_a:mo_a], wl[:, ff_a:]], axis=1).T.astype(BF16)
        main, g_row_raw = norm_inproj(xf, attn_norm[l].reshape(1, d), w_main, w_gate_t)
        bias = jnp.concatenate([mlstm_b_i[l], mlstm_b_f[l], fox_b_f[l]]).reshape(N_GATE_ROWS, 1).astype(F32)
        g_row, g_col, c_aug = gates(g_row_raw, bias, batch)
        o_hg = hgrn2(main, hgrn_lb.astype(F32), hgrn_norm[l].reshape(1, HG_W), m_bf, batch, l)
        o_ml = mlstm(main, g_row, g_col, mlstm_conv[l], mlstm_norm[l].reshape(1, ML_W), m_bf, batch)
        o_fx = fox(main, c_aug, batch)
        wo = w_out[l].astype(BF16)
        fg = ffn_norm[l].reshape(1, d)
        j = l // 2
        if l % 2 == 0:
            x_res, h = outproj(xf, o_hg, o_ml, o_fx, wo, fg)
            xf = dense_ffn(h, x_res, dense_w_gate[j].astype(BF16), dense_w_up[j].astype(BF16),
                           dense_w_down[j].astype(BF16))
        else:
            r = jnp.pad(router[j].T, ((0, ROUTER_ROWS - N_EXPERTS), (0, 0)))
            r_hi = r.astype(BF16)
            r_lo = (r - r_hi.astype(F32)).astype(BF16)
            x_res, h, idx_rows, gate_rows = outproj(xf, o_hg, o_ml, o_fx, wo, fg, (r_hi, r_lo))
            gate = gate_rows[:TOP_K].T
            dest, blk_expert, n_used, p_rows = _moe_plan(idx_rows[:TOP_K].T, BM_MOE)
            src_rows, dst_rows = moe_invert(dest, p_rows, n, BM_MOE)
            y = moe_experts(src_rows, dst_rows, blk_expert, n_used, h, moe_w_gate[j].astype(BF16),
                            moe_w_up[j].astype(BF16), moe_w_down[j].astype(BF16))
            done = l == depth - 1
            xf = moe_combine(x_res, gate, final_norm_g.reshape(1, d), y, done)
    if not done:
        xf = final_norm(xf, final_norm_g.reshape(1, d))
    return xf.reshape(batch, seq, d)
```

```python
import functools

import jax
import jax.numpy as jnp
import numpy as np
from jax import lax
from jax.experimental import pallas as pl
from jax.experimental.pallas import tpu as pltpu

F32 = jnp.float32
BF16 = jnp.bfloat16
EPS = 1e-6
NEG_INF = float("-inf")
LOG2E = 1.4426950408889634

HEAD_DIM = 64
HG_W = 256
ML_W = 256
FX_W = 512
ML_HEADS = 4
FX_HEADS = 8
CONV_K = 4
N_EXPERTS = 8
TOP_K = 2
N_GATE_ROWS = 16
ROUTE_ROWS = 8
ROUTER_ROWS = 16

LANES = 128
VMEM_LIMIT = 48 * 1024 * 1024
VMEM_LIMIT_BIG = 56 * 1024 * 1024

TM_PROJ = 512
TM_INPROJ = 1024
T_GATE = 512
T_HG = 128
SUB_HG = 16
T_ML = 128
TQ_FX = 1024
TM_FFN = 1024
BM_MOE = 512
TF_MOE = 1792
CHUNK_MOE = 256
N_DMA_QUEUES = 2


def _cparams(sem, vmem=VMEM_LIMIT):
    return pltpu.CompilerParams(dimension_semantics=sem, vmem_limit_bytes=vmem)


def _split3(x):
    hi = x.astype(BF16)
    r = x - hi.astype(F32)
    mid = r.astype(BF16)
    lo = (r - mid.astype(F32)).astype(BF16)
    return hi, mid, lo


def _dot(a, b):
    return jnp.dot(a, b, preferred_element_type=F32)


def _dot_nt(a, b):
    return lax.dot_general(a, b, (((1,), (1,)), ((), ())), preferred_element_type=F32)


def _dot_tn(a, b):
    return lax.dot_general(a, b, (((0,), (0,)), ((), ())), preferred_element_type=F32)


def _log_sigmoid(z):
    return -(jnp.maximum(-z, 0.0) + jnp.log1p(jnp.exp(-jnp.abs(z))))


def _sigmoid(z):
    return 1.0 / (1.0 + jnp.exp(-z))


def _head_mean_sq(o, m_bf):
    o2 = o * o
    hi = o2.astype(BF16)
    lo = (o2 - hi.astype(F32)).astype(BF16)
    return (_dot(hi, m_bf) + _dot(lo, m_bf)) * (1.0 / HEAD_DIM)


def _norm_inproj_kernel(x_ref, g_ref, w_ref, wgt_ref, main_ref, grow_ref, *, tn):
    x = x_ref[...]
    ms = jnp.mean(x * x, axis=-1, keepdims=True)
    h = ((x * lax.rsqrt(ms + EPS)) * g_ref[...]).astype(BF16)
    for j in range(w_ref.shape[1] // tn):
        main_ref[:, j * tn:(j + 1) * tn] = _dot(h, w_ref[:, j * tn:(j + 1) * tn]).astype(BF16)
    grow_ref[...] = _dot_nt(wgt_ref[...], h)


def norm_inproj(x, g, w_main, w_gate_t):
    n, d = x.shape
    wm = w_main.shape[1]
    tm = min(TM_INPROJ, n)
    return pl.pallas_call(
        functools.partial(_norm_inproj_kernel, tn=512),
        out_shape=(jax.ShapeDtypeStruct((n, wm), BF16), jax.ShapeDtypeStruct((N_GATE_ROWS, n), F32)),
        grid=(n // tm,),
        in_specs=[pl.BlockSpec((tm, d), lambda i: (i, 0)),
                  pl.BlockSpec((1, d), lambda i: (0, 0)),
                  pl.BlockSpec((d, wm), lambda i: (0, 0)),
                  pl.BlockSpec((N_GATE_ROWS, d), lambda i: (0, 0))],
        out_specs=(pl.BlockSpec((tm, wm), lambda i: (i, 0)),
                   pl.BlockSpec((N_GATE_ROWS, tm), lambda i: (0, i))),
        compiler_params=_cparams(("arbitrary",)),
        name="norm_inproj",
    )(x, g, w_main, w_gate_t)


def _gates_kernel(g_ref, bias_ref, sel_ref, grow_ref, gcol_ref, caug_ref, carry_ref):
    t = g_ref.shape[1]

    @pl.when(pl.program_id(1) == 0)
    def _():
        carry_ref[...] = jnp.zeros_like(carry_ref)

    z = g_ref[...] + bias_ref[...]
    row = lax.broadcasted_iota(jnp.int32, z.shape, 0)
    is_input_gate = row < ML_HEADS
    val = jnp.where(is_input_gate, 0.0, _log_sigmoid(z))
    r_i = lax.broadcasted_iota(jnp.int32, (t, t), 0)
    c_i = lax.broadcasted_iota(jnp.int32, (t, t), 1)
    upper = jnp.where(r_i <= c_i, 1.0, 0.0).astype(BF16)
    nr = z.shape[0]
    cum = _dot(jnp.concatenate(_split3(val), axis=0), upper)
    tot = cum[0:nr] + cum[nr:2 * nr] + cum[2 * nr:3 * nr] + carry_ref[:, 0:1]
    out = jnp.where(is_input_gate, z, tot)
    grow_ref[...] = out
    carry_ref[...] = jnp.broadcast_to(tot[:, t - 1:t], carry_ref.shape)
    eye = jnp.where(r_i == c_i, 1.0, 0.0).astype(BF16)
    p0, p1, p2 = _split3(out)
    gcol_ref[...] = _dot_nt(eye, p0) + _dot_nt(eye, p1) + _dot_nt(eye, p2)
    n0, n1, n2 = _split3(out * (-LOG2E))
    zrows = _dot(sel_ref[0], n0) + _dot(sel_ref[1], n1) + _dot(sel_ref[2], n2)
    caug_ref[...] = _dot_nt(eye, zrows.astype(BF16)).astype(BF16)


def _bias_lane_selectors():
    sel = np.zeros((3, FX_HEADS // 2 * LANES, N_GATE_ROWS), np.float32)
    for j in range(3):
        for p in range(FX_HEADS // 2):
            for a in range(2):
                sel[j, LANES * p + 3 * a + j, 2 * ML_HEADS + 2 * p + a] = 1.0
    return jnp.asarray(sel, BF16)


def gates(g_row, bias, batch):
    r, n = g_row.shape
    s = n // batch
    t = min(T_GATE, s)
    nb = s // t
    sel = _bias_lane_selectors()
    wc = sel.shape[1]
    return pl.pallas_call(
        _gates_kernel,
        out_shape=(jax.ShapeDtypeStruct((r, n), F32), jax.ShapeDtypeStruct((n, r), F32),
                   jax.ShapeDtypeStruct((n, wc), BF16)),
        grid=(batch, nb),
        in_specs=[pl.BlockSpec((r, t), lambda b, j: (0, b * nb + j)),
                  pl.BlockSpec((r, 1), lambda b, j: (0, 0)),
                  pl.BlockSpec(sel.shape, lambda b, j: (0, 0, 0))],
        out_specs=(pl.BlockSpec((r, t), lambda b, j: (0, b * nb + j)),
                   pl.BlockSpec((t, r), lambda b, j: (b * nb + j, 0)),
                   pl.BlockSpec((t, wc), lambda b, j: (b * nb + j, 0))),
        scratch_shapes=[pltpu.VMEM((r, LANES), F32)],
        compiler_params=_cparams(("arbitrary", "arbitrary")),
        name="gates",
    )(g_row, bias, sel)


def _hgrn_kernel(q_ref, f_ref, i_ref, g_ref, lb_ref, gain_ref, m_ref, o_ref, st_ref, x_ref, y_ref, *, layer):
    @pl.when(pl.program_id(0) == 0)
    def _():
        st_ref[...] = jnp.zeros_like(st_ref)

    for b in range(q_ref.shape[0]):
        _hgrn_rows(q_ref.at[b], f_ref.at[b], i_ref.at[b], g_ref.at[b], lb_ref, gain_ref, m_ref, o_ref.at[b],
                   st_ref.at[b], x_ref.at[b], y_ref.at[b], layer=layer)


def _hgrn_rows(q_ref, f_ref, i_ref, g_ref, lb_ref, gain_ref, m_ref, o_ref, st_ref, x_ref, y_ref, *, layer):
    t = q_ref.shape[0]
    sub = SUB_HG
    nsub = t // sub

    lbp = lb_ref[...]
    rows = [lbp[r:r + 1, :] for r in range(lbp.shape[0])]
    mx = functools.reduce(jnp.maximum, rows)
    es = [jnp.exp(r - mx) for r in rows]
    tot = functools.reduce(lambda a, b: a + b, es)
    cs, run = [], None
    for e in es:
        run = e / tot if run is None else run + e / tot
        cs.append(run)
    lb = cs[layer] - cs[0]

    z = f_ref[...].astype(F32)
    a = jnp.log(lb)
    bb = jnp.log1p(-lb) + _log_sigmoid(z)
    log_f = jnp.maximum(a, bb) + jnp.log1p(jnp.exp(-jnp.abs(a - bb)))
    k = (1.0 - lb) * _sigmoid(-z)
    q = q_ref[...].astype(F32)
    v = i_ref[...].astype(F32)
    m_bf = m_ref[...]

    r_i = lax.broadcasted_iota(jnp.int32, (t, t), 0)
    c_i = lax.broadcasted_iota(jnp.int32, (t, t), 1)
    same_sub = (c_i // sub) == (r_i // sub)
    lower = jnp.where(jnp.logical_and(c_i <= r_i, same_sub), 1.0, 0.0).astype(BF16)
    f0, f1, f2 = _split3(log_f)
    b = _dot(lower, f0) + _dot(lower, f1) + _dot(lower, f2)
    b2 = b * LOG2E
    m_f32 = m_bf.astype(F32)
    t_in_sub = lax.broadcasted_iota(jnp.int32, (sub, q.shape[1]), 0)

    st = st_ref[...]
    o_inter = []
    for i in range(nsub):
        rows = slice(i * sub, (i + 1) * sub)
        bi, b2i, qi, ki = b[rows], b2[rows], q[rows], k[rows]
        for s in range(sub):
            diff = b2i - b2i[s:s + 1, :]
            if s > 0:
                diff = jnp.where(t_in_sub >= s, diff, NEG_INF)
            base = (i * sub + s) * sub
            x_ref[base:base + sub, :] = (qi * (ki[s:s + 1, :] * jnp.exp2(diff))).astype(BF16)
        o_inter.append(_dot_nt((qi * jnp.exp(bi)).astype(BF16), st.astype(BF16)))
        b_end = bi[sub - 1:sub, :]
        kd = ki * jnp.exp(b_end - bi)
        st = st * jnp.exp(b_end) + _dot_tn(v[rows].astype(BF16), kd.astype(BF16)) * m_f32
    st_ref[...] = st
    y_ref[...] = _dot(x_ref[...], m_bf)
    outs = []
    for i in range(nsub):
        vi = v[i * sub:(i + 1) * sub]
        acc = o_inter[i]
        for s in range(sub):
            base = (i * sub + s) * sub
            acc = acc + y_ref[base:base + sub, :] * vi[s:s + 1, :]
        outs.append(acc)
    o = jnp.concatenate(outs, axis=0)

    gt = g_ref[...].astype(F32)
    y = o * lax.rsqrt(_head_mean_sq(o, m_bf) + EPS) * gain_ref[...] * (gt * _sigmoid(gt))
    o_ref[...] = y.astype(o_ref.dtype)


def hgrn2(main, lb_all, gain, m_bf, batch, layer):
    n = main.shape[0]
    s = n // batch
    t = T_HG
    nc = s // t
    w = HG_W
    n_pairs = t * SUB_HG
    main3 = main.reshape(batch, s, main.shape[1])
    col = lambda cidx: pl.BlockSpec((batch, t, w), lambda c: (0, c, cidx))
    full = lambda shape: pl.BlockSpec(shape, lambda c: (0, 0))
    out = pl.pallas_call(
        functools.partial(_hgrn_kernel, layer=layer),
        out_shape=jax.ShapeDtypeStruct((batch, s, w), BF16),
        grid=(nc,),
        in_specs=[col(0), col(1), col(2), col(3), full(lb_all.shape), full((1, w)), full((w, w))],
        out_specs=pl.BlockSpec((batch, t, w), lambda c: (0, c, 0)),
        scratch_shapes=[pltpu.VMEM((batch, w, w), F32), pltpu.VMEM((batch, n_pairs, w), BF16),
                        pltpu.VMEM((batch, n_pairs, w), F32)],
        compiler_params=_cparams(("arbitrary",)),
        name="hgrn2",
    )(main3, main3, main3, main3, lb_all, gain, m_bf)
    return out.reshape(n, w)


def _mlstm_kernel(q_ref, k_ref, v_ref, og_ref, grow_ref, gcol_ref, cw_ref, gain_ref, m_ref, o_ref,
                  ext_ref, ct_ref, n_ref, mm_ref):
    t = q_ref.shape[0]
    w = q_ref.shape[1]
    halo = 8

    @pl.when(pl.program_id(1) == 0)
    def _():
        ext_ref[0:halo, :] = jnp.zeros((halo, 2 * w), F32)
        ct_ref[...] = jnp.zeros_like(ct_ref)
        n_ref[...] = jnp.zeros_like(n_ref)
        mm_ref[...] = jnp.zeros_like(mm_ref)

    ext_ref[halo:halo + t, 0:w] = q_ref[...].astype(F32)
    ext_ref[halo:halo + t, w:2 * w] = k_ref[...].astype(F32)
    cw = cw_ref[...]
    y = None
    for j in range(CONV_K):
        term = ext_ref[halo - (CONV_K - 1) + j:halo - (CONV_K - 1) + j + t, :] * cw[j:j + 1, :]
        y = term if y is None else y + term
    tail = ext_ref[t:t + halo, :]
    ext_ref[0:halo, :] = tail
    qk = y * _sigmoid(y)
    q = qk[:, 0:w]
    k = qk[:, w:2 * w] * (HEAD_DIM ** -0.5)
    kb = k.astype(BF16)
    vb = v_ref[...]
    m_bf = m_ref[...]

    grow = grow_ref[...]
    gcol = gcol_ref[...]
    lane_head = lax.broadcasted_iota(jnp.int32, (1, w), 1) // HEAD_DIM
    r_i = lax.broadcasted_iota(jnp.int32, (t, t), 0)
    c_i = lax.broadcasted_iota(jnp.int32, (t, t), 1)
    causal = c_i <= r_i

    num_intra = jnp.zeros((t, w), F32)
    sint_l = jnp.zeros((t, w), F32)
    wsum_l = jnp.zeros((t, w), F32)
    mt_l = jnp.zeros((t, w), F32)
    wk_l = jnp.zeros((t, w), F32)
    decay_l = jnp.zeros((1, w), F32)
    for h in range(ML_HEADS):
        sel = lane_head == h
        qh = jnp.where(sel, q, 0.0).astype(BF16)
        s = _dot_nt(qh, kb)
        bc = gcol[:, ML_HEADS + h:ML_HEADS + h + 1]
        br = grow[ML_HEADS + h:ML_HEADS + h + 1, :]
        lir = grow[h:h + 1, :]
        lic = gcol[:, h:h + 1]
        dlog = jnp.where(causal, bc - br + lir, NEG_INF)
        mmh = mm_ref[h:h + 1, 0:1]
        inter = bc + mmh
        m_t = jnp.maximum(jnp.max(dlog, axis=1, keepdims=True), inter)
        wgt = s * jnp.exp(dlog - m_t)
        s_int = jnp.exp(inter - m_t)
        pv = _dot(wgt.astype(BF16), vb)
        num_intra = jnp.where(sel, pv, num_intra)
        sint_l = jnp.where(sel, s_int, sint_l)
        wsum_l = jnp.where(sel, jnp.sum(wgt, axis=1, keepdims=True), wsum_l)
        mt_l = jnp.where(sel, m_t, mt_l)
        b_end = br[:, t - 1:t]
        m_new = jnp.maximum(b_end + mmh, jnp.max(b_end - br + lir, axis=1, keepdims=True))
        wk_l = jnp.where(sel, jnp.exp(b_end - bc + lic - m_new), wk_l)
        decay_l = jnp.where(sel, jnp.exp(b_end + mmh - m_new), decay_l)
        mm_ref[h:h + 1, :] = jnp.broadcast_to(m_new - b_end, (1, mm_ref.shape[1]))

    ct = ct_ref[...]
    nrow = n_ref[0:1, :]
    q_c = _dot_nt(q.astype(BF16), ct.astype(BF16))
    qn = q * nrow
    qn_hi = qn.astype(BF16)
    qn_lo = (qn - qn_hi.astype(F32)).astype(BF16)
    qn_l = _dot(qn_hi, m_bf) + _dot(qn_lo, m_bf)
    num = num_intra + sint_l * q_c
    den = wsum_l + sint_l * qn_l
    hval = num / jnp.maximum(jnp.abs(den), jnp.exp(-mt_l))

    kw = k * wk_l
    upd = _dot_tn(vb, kw.astype(BF16))
    ct_ref[...] = decay_l * ct + upd * m_bf.astype(F32)
    n_ref[...] = jnp.broadcast_to(decay_l * nrow + jnp.sum(kw, axis=0, keepdims=True), n_ref.shape)

    og = og_ref[...].astype(F32)
    yv = hval * lax.rsqrt(_head_mean_sq(hval, m_bf) + EPS) * gain_ref[...] * _sigmoid(og)
    o_ref[...] = yv.astype(o_ref.dtype)


def mlstm(main, g_row, g_col, conv_w, gain, m_bf, batch):
    n = main.shape[0]
    s = n // batch
    t = min(T_ML, s)
    nc = s // t
    w = ML_W
    col = lambda cidx: pl.BlockSpec((t, w), lambda b, c: (b * nc + c, cidx))
    full = lambda shape: pl.BlockSpec(shape, lambda b, c: (0, 0))
    return pl.pallas_call(
        _mlstm_kernel,
        out_shape=jax.ShapeDtypeStruct((n, w), BF16),
        grid=(batch, nc),
        in_specs=[col(4), col(5), col(6), col(7),
                  pl.BlockSpec((N_GATE_ROWS, t), lambda b, c: (0, b * nc + c)),
                  pl.BlockSpec((t, N_GATE_ROWS), lambda b, c: (b * nc + c, 0)),
                  full((CONV_K, 2 * w)), full((1, w)), full((w, w))],
        out_specs=pl.BlockSpec((t, w), lambda b, c: (b * nc + c, 0)),
        scratch_shapes=[pltpu.VMEM((t + 8, 2 * w), F32), pltpu.VMEM((w, w), F32),
                        pltpu.VMEM((8, w), F32), pltpu.VMEM((8, LANES), F32)],
        compiler_params=_cparams(("arbitrary", "arbitrary")),
        name="mlstm",
    )(main, main, main, main, g_row, g_col, conv_w, gain, m_bf)


def _fox_kernel(q_ref, k_ref, v_ref, c_ref, o_ref, m_ref, acc_ref):
    tq = q_ref.shape[0]
    half = tq // 2
    qi = pl.program_id(2)
    lane = lax.broadcasted_iota(jnp.int32, (1, LANES), 1)
    first = lane < HEAD_DIM

    m_ref[...] = jnp.full(m_ref.shape, NEG_INF, F32)
    acc_ref[...] = jnp.zeros_like(acc_ref)

    def attend(r0, r1, kstart, klen, causal_shift):
        q2 = q_ref[r0:r1, :]
        v2 = v_ref[pl.ds(kstart, klen), :]
        k_aug = jnp.concatenate([k_ref[pl.ds(kstart, klen), :], c_ref[pl.ds(kstart, klen), :]], axis=1)
        for a in range(2):
            sel = first if a == 0 else jnp.logical_not(first)
            ones_lanes = jnp.logical_and(lane >= 3 * a, lane < 3 * a + 3)
            q_bias = jnp.broadcast_to(jnp.where(ones_lanes, 1.0, 0.0).astype(q2.dtype), q2.shape)
            q_aug = jnp.concatenate([jnp.where(sel, q2, jnp.zeros_like(q2)), q_bias], axis=1)
            s = _dot_nt(q_aug, k_aug)
            if causal_shift is not None:
                r_i = lax.broadcasted_iota(jnp.int32, s.shape, 0)
                c_i = lax.broadcasted_iota(jnp.int32, s.shape, 1)
                s = jnp.where(c_i <= r_i + causal_shift, s, NEG_INF)
            m_prev = m_ref[a, r0:r1, :]
            m_new = jnp.maximum(m_prev, jnp.max(s, axis=1, keepdims=True))
            p = jnp.concatenate([jnp.exp2(s[:, c * LANES:(c + 1) * LANES] - m_new).astype(v2.dtype)
                                 for c in range(klen // LANES)], axis=1)
            v_aug = jnp.where(sel, v2, jnp.ones_like(v2))
            acc_ref[a, r0:r1, :] = jnp.exp2(m_prev - m_new) * acc_ref[a, r0:r1, :] + _dot(p, v_aug)
            m_ref[a, r0:r1, :] = m_new

    def past_pair(j, carry):
        attend(0, tq, pl.multiple_of(2 * j * tq, tq), tq, None)
        attend(0, tq, pl.multiple_of((2 * j + 1) * tq, tq), tq, None)
        return carry

    lax.fori_loop(0, qi // 2, past_pair, 0)

    @pl.when(qi % 2 == 1)
    def _():
        attend(0, tq, pl.multiple_of((qi - 1) * tq, tq), tq, None)

    d0 = pl.multiple_of(qi * tq, tq)
    attend(0, half, d0, half, 0)
    attend(half, tq, d0, tq, half)

    acc_a = acc_ref[0]
    acc_b = acc_ref[1]
    out = jnp.where(first, acc_a / pltpu.roll(acc_a, HEAD_DIM, 1), acc_b / pltpu.roll(acc_b, HEAD_DIM, 1))
    o_ref[...] = out.astype(o_ref.dtype)


def fox(main, c_aug, batch):
    n = main.shape[0]
    s = n // batch
    tq = min(TQ_FX, s)
    nq = s // tq
    pairs = FX_HEADS // 2
    qcol, kcol, vcol = 2048 // LANES, 2560 // LANES, 3072 // LANES
    seq_blk = lambda col0: pl.BlockSpec((s, LANES), lambda b, p, i: (b, col0 + p))
    return pl.pallas_call(
        _fox_kernel,
        out_shape=jax.ShapeDtypeStruct((n, FX_W), BF16),
        grid=(batch, pairs, nq),
        in_specs=[pl.BlockSpec((tq, LANES), lambda b, p, i: (b * nq + i, qcol + p)),
                  seq_blk(kcol), seq_blk(vcol), seq_blk(0)],
        out_specs=pl.BlockSpec((tq, LANES), lambda b, p, i: (b * nq + i, p)),
        scratch_shapes=[pltpu.VMEM((2, tq, LANES), F32), pltpu.VMEM((2, tq, LANES), F32)],
        compiler_params=_cparams(("arbitrary", "arbitrary", "arbitrary")),
        name="fox",
    )(main, main, main, c_aug)


def _outproj_body(x_ref, ohg_ref, oml_ref, ofx_ref, w_ref, g_ref):
    acc = x_ref[...]
    acc = acc + _dot(ohg_ref[...], w_ref[0:HG_W, :])
    acc = acc + _dot(oml_ref[...], w_ref[HG_W:HG_W + ML_W, :])
    acc = acc + _dot(ofx_ref[...], w_ref[HG_W + ML_W:, :])
    ms = jnp.mean(acc * acc, axis=-1, keepdims=True)
    h = (acc * lax.rsqrt(ms + EPS)) * g_ref[...]
    return acc, h


def _outproj_moe_kernel(x_ref, ohg_ref, oml_ref, ofx_ref, w_ref, g_ref, rhi_ref, rlo_ref,
                        xo_ref, h_ref, idx_ref, gate_ref):
    acc, h = _outproj_body(x_ref, ohg_ref, oml_ref, ofx_ref, w_ref, g_ref)
    xo_ref[...] = acc
    h_ref[...] = h
    h_hi = h.astype(BF16)
    h_lo = (h - h_hi.astype(F32)).astype(BF16)
    logits = _dot_nt(rhi_ref[...], h_hi) + _dot_nt(rlo_ref[...], h_hi) + _dot_nt(rhi_ref[...], h_lo)
    row_i = lax.broadcasted_iota(jnp.int32, logits.shape, 0)
    row = row_i.astype(F32)
    n_rows = float(logits.shape[0])
    lg = jnp.where(row_i < N_EXPERTS, logits, NEG_INF)
    m1 = jnp.max(lg, axis=0, keepdims=True)
    i1 = jnp.min(jnp.where(lg == m1, row, n_rows), axis=0, keepdims=True)
    lg2 = jnp.where(row == i1, NEG_INF, lg)
    m2 = jnp.max(lg2, axis=0, keepdims=True)
    i2 = jnp.min(jnp.where(lg2 == m2, row, n_rows), axis=0, keepdims=True)
    e = jnp.exp(m2 - m1)
    g1 = 1.0 / (1.0 + e)
    g2 = e / (1.0 + e)
    out_row = lax.broadcasted_iota(jnp.int32, idx_ref.shape, 0)
    idx_ref[...] = jnp.where(out_row == 0, i1, jnp.where(out_row == 1, i2, 0.0)).astype(jnp.int32)
    gate_ref[...] = jnp.where(out_row == 0, g1, jnp.where(out_row == 1, g2, 0.0))


def outproj_route(x, o_hg, o_ml, o_fx, w_out, g, r_hi, r_lo):
    n, d = x.shape
    tm = min(TM_PROJ, n)
    row = lambda width: pl.BlockSpec((tm, width), lambda i: (i, 0))
    full = lambda shape: pl.BlockSpec(shape, lambda i: (0, 0))
    rout = pl.BlockSpec((ROUTE_ROWS, tm), lambda i: (0, i))
    return pl.pallas_call(
        _outproj_moe_kernel,
        out_shape=(jax.ShapeDtypeStruct((n, d), F32), jax.ShapeDtypeStruct((n, d), F32),
                   jax.ShapeDtypeStruct((ROUTE_ROWS, n), jnp.int32), jax.ShapeDtypeStruct((ROUTE_ROWS, n), F32)),
        grid=(n // tm,),
        in_specs=[row(d), row(HG_W), row(ML_W), row(FX_W), full(w_out.shape), full((1, d)),
                  full(r_hi.shape), full(r_lo.shape)],
        out_specs=(row(d), row(d), rout, rout),
        compiler_params=_cparams(("arbitrary",)), name="outproj_route",
    )(x, o_hg, o_ml, o_fx, w_out, g, r_hi, r_lo)


def _outproj_ffn_kernel(x_ref, ohg_ref, oml_ref, ofx_ref, w_ref, g_ref, wg_ref, wu_ref, wd_ref, o_ref, *, chunk):
    acc, h = _outproj_body(x_ref, ohg_ref, oml_ref, ofx_ref, w_ref, g_ref)
    hb = h.astype(BF16)
    o_ref[...] = acc
    for c in range(wg_ref.shape[1] // chunk):
        cols = slice(c * chunk, (c + 1) * chunk)
        gt = _dot(hb, wg_ref[:, cols])
        up = _dot(hb, wu_ref[:, cols])
        hid = (gt * _sigmoid(gt) * up).astype(BF16)
        o_ref[...] += _dot(hid, wd_ref[cols, :])


def outproj_ffn(x, o_hg, o_ml, o_fx, w_out, g, wg, wu, wd):
    n, d = x.shape
    ff = wg.shape[1]
    tm = min(TM_FFN, n)
    chunk = CHUNK_MOE
    assert ff % chunk == 0
    row = lambda width: pl.BlockSpec((tm, width), lambda i: (i, 0))
    full = lambda shape: pl.BlockSpec(shape, lambda i: (0, 0), pipeline_mode=pl.Buffered(1))
    return pl.pallas_call(
        functools.partial(_outproj_ffn_kernel, chunk=chunk),
        out_shape=jax.ShapeDtypeStruct((n, d), F32),
        grid=(n // tm,),
        in_specs=[row(d), row(HG_W), row(ML_W), row(FX_W), full(w_out.shape), full((1, d)),
                  full((d, ff)), full((d, ff)), full((ff, d))],
        out_specs=row(d),
        compiler_params=_cparams(("arbitrary",), VMEM_LIMIT_BIG),
        name="outproj_ffn",
    )(x, o_hg, o_ml, o_fx, w_out, g, wg, wu, wd)


def _row_copy(src_ref, src_row, dst_ref, dst_row, sem):
    return pltpu.make_async_copy(src_ref.at[pl.ds(src_row, 1)], dst_ref.at[pl.ds(dst_row, 1)], sem)


def _invert_kernel(dest_ref, src0_ref, dst0_ref, src_ref, dst_ref, sem, *, n_tok):
    init_src = pltpu.make_async_copy(src0_ref, src_ref, sem.at[0])
    init_dst = pltpu.make_async_copy(dst0_ref, dst_ref, sem.at[1])
    init_src.start()
    init_dst.start()
    init_src.wait()
    init_dst.wait()

    def put(tok, carry):
        for slot in range(TOP_K):
            row = dest_ref[TOP_K * tok + slot]
            src_ref[row] = tok
            dst_ref[row] = slot * n_tok + tok
        return carry

    lax.fori_loop(0, n_tok, put, 0, unroll=8)


def moe_invert(dest, p_rows, n_tok, bm):
    spare = TOP_K * n_tok + np.arange(p_rows + bm) % (2 * bm)
    spare[p_rows:] = TOP_K * n_tok + bm + np.arange(bm)
    src0 = jnp.zeros((p_rows + bm,), jnp.int32)
    dst0 = jnp.asarray(spare, jnp.int32)
    rows = jax.ShapeDtypeStruct((p_rows + bm,), jnp.int32)
    smem = pl.BlockSpec(memory_space=pltpu.SMEM)
    hbm = pl.BlockSpec(memory_space=pl.ANY)
    return pl.pallas_call(
        functools.partial(_invert_kernel, n_tok=n_tok),
        out_shape=(rows, rows),
        in_specs=[smem, hbm, hbm],
        out_specs=(smem, smem),
        scratch_shapes=[pltpu.SemaphoreType.DMA((2,))],
        name="moe_invert",
    )(dest, src0, dst0)


def _experts_kernel(src_ref, dst_ref, blk_e_ref, nused_ref, h_ref, wg_ref, wu_ref, wd_ref, y_ref,
                    xbuf_ref, xb_ref, acc_ref, obuf_ref, gsem, ssem, *, n_tok, chunk):
    del blk_e_ref
    bm = xb_ref.shape[0]
    tf = wg_ref.shape[2]
    nchunk = tf // chunk
    m = pl.program_id(0)
    f = pl.program_id(1)
    nb = pl.num_programs(0)
    nf = pl.num_programs(1)
    slot = m % 2
    other = 1 - slot
    n_used = nused_ref[0]
    live = m < n_used

    def gather_copy(block, s, i):
        return _row_copy(h_ref, src_ref[block * bm + i], xbuf_ref.at[s], i, gsem.at[s])

    def scatter_copy(block, s, i):
        return _row_copy(obuf_ref.at[s], i, y_ref, dst_ref[block * bm + i], ssem.at[s])

    def for_rows(fn):
        def body(i, carry):
            fn(i)
            return carry
        lax.fori_loop(0, bm, body, 0, unroll=8)

    def wait_gather(s):
        for_rows(lambda i: _row_copy(h_ref, 0, xbuf_ref.at[s], i, gsem.at[s]).wait())

    def wait_scatter(s):
        for_rows(lambda i: _row_copy(obuf_ref.at[s], i, y_ref, 0, ssem.at[s]).wait())

    def swiglu(first, issue):
        xb = xb_ref[...]
        n_groups = 3 * nchunk

        def issue_group(g):
            if issue is not None:
                for i in range(g * bm // n_groups, (g + 1) * bm // n_groups):
                    issue(i)

        for c in range(nchunk):
            cols = slice(c * chunk, (c + 1) * chunk)
            gt = _dot(xb, wg_ref[0, :, cols])
            issue_group(3 * c)
            up = _dot(xb, wu_ref[0, :, cols])
            issue_group(3 * c + 1)
            hid = (gt * _sigmoid(gt) * up).astype(BF16)
            part = _dot(hid, wd_ref[0, cols, :])
            if first and c == 0:
                acc_ref[...] = part
            else:
                acc_ref[...] += part
            issue_group(3 * c + 2)

    @pl.when(jnp.logical_and(f == 0, live))
    def _():
        @pl.when(m == 0)
        def _():
            obuf_ref[...] = jnp.zeros_like(obuf_ref)
            spare0 = pltpu.make_async_copy(obuf_ref.at[0], y_ref.at[pl.ds(TOP_K * n_tok, bm)], ssem.at[0])
            spare0.start()
            spare0.wait()
            for_rows(lambda i: gather_copy(0, 0, i).start())

        wait_gather(slot)
        xb_ref[...] = xbuf_ref[slot].astype(BF16)
        nxt = jnp.minimum(m + 1, nb - 1)
        swiglu(True, lambda i: gather_copy(nxt, other, i).start(priority=i % N_DMA_QUEUES))

    @pl.when(jnp.logical_and(jnp.logical_and(f > 0, f < nf - 1), live))
    def _():
        swiglu(False, None)

    @pl.when(jnp.logical_and(f == nf - 1, live))
    def _():
        prev = jnp.where(m == 0, nb, m - 1)
        swiglu(False, lambda i: scatter_copy(prev, other, i).start(priority=i % N_DMA_QUEUES))

        @pl.when(m >= 1)
        def _():
            wait_scatter(slot)

        obuf_ref[slot] = acc_ref[...]

        @pl.when(m == n_used - 1)
        def _():
            wait_gather(other)
            for_rows(lambda i: scatter_copy(m, slot, i).start())
            wait_scatter(slot)
            wait_scatter(other)


def moe_experts(src_rows, dst_rows, blk_expert, n_used, h, wg, wu, wd):
    n_tok, d = h.shape
    bm = BM_MOE
    p_rows = src_rows.shape[0] - bm
    ff = wg.shape[2]
    tf = TF_MOE
    nf = ff // tf
    assert nf >= 2, "the first and last hidden tile of a block carry different row copies"

    def w_idx(m, f, be, nu):
        f_eff = jnp.where(m % 2 == 0, f, nf - 1 - f)
        last = jnp.where((nu[0] - 1) % 2 == 0, nf - 1, 0)
        return be[m], jnp.where(m < nu[0], f_eff, last)

    return pl.pallas_call(
        functools.partial(_experts_kernel, n_tok=n_tok, chunk=CHUNK_MOE),
        out_shape=jax.ShapeDtypeStruct((TOP_K * n_tok + 2 * bm, d), F32),
        grid_spec=pltpu.PrefetchScalarGridSpec(
            num_scalar_prefetch=4, grid=(p_rows // bm, nf),
            in_specs=[pl.BlockSpec(memory_space=pl.ANY),
                      pl.BlockSpec((1, d, tf), lambda m, f, sr, ds, be, nu: (w_idx(m, f, be, nu)[0], 0, w_idx(m, f, be, nu)[1])),
                      pl.BlockSpec((1, d, tf), lambda m, f, sr, ds, be, nu: (w_idx(m, f, be, nu)[0], 0, w_idx(m, f, be, nu)[1])),
                      pl.BlockSpec((1, tf, d), lambda m, f, sr, ds, be, nu: (w_idx(m, f, be, nu)[0], w_idx(m, f, be, nu)[1], 0))],
            out_specs=pl.BlockSpec(memory_space=pl.ANY),
            scratch_shapes=[pltpu.VMEM((2, bm, d), F32), pltpu.VMEM((bm, d), BF16), pltpu.VMEM((bm, d), F32),
                            pltpu.VMEM((2, bm, d), F32), pltpu.SemaphoreType.DMA((2,)),
                            pltpu.SemaphoreType.DMA((2,))]),
        compiler_params=_cparams(("arbitrary", "arbitrary"), VMEM_LIMIT_BIG),
        name="moe_experts",
    )(src_rows, dst_rows, blk_expert, n_used, h, wg, wu, wd)


def _combine_kernel(x_ref, gate_ref, fg_ref, y0_ref, y1_ref, o_ref, *, final):
    gate = gate_ref[...]
    y = x_ref[...] + gate[:, 0:1] * y0_ref[...] + gate[:, 1:2] * y1_ref[...]
    if final:
        ms = jnp.mean(y * y, axis=-1, keepdims=True)
        y = (y * lax.rsqrt(ms + EPS)) * fg_ref[...]
    o_ref[...] = y


def moe_combine(x, gate, final_g, y, final):
    n, d = x.shape
    tb = min(TM_PROJ, n)
    return pl.pallas_call(
        functools.partial(_combine_kernel, final=final),
        out_shape=jax.ShapeDtypeStruct((n, d), F32),
        grid=(n // tb,),
        in_specs=[pl.BlockSpec((tb, d), lambda i: (i, 0)),
                  pl.BlockSpec((tb, TOP_K), lambda i: (i, 0)),
                  pl.BlockSpec((1, d), lambda i: (0, 0)),
                  pl.BlockSpec((tb, d), lambda i: (i, 0)),
                  pl.BlockSpec((tb, d), lambda i: (i + n // tb, 0))],
        out_specs=pl.BlockSpec((tb, d), lambda i: (i, 0)),
        compiler_params=_cparams(("arbitrary",)),
        name="moe_combine",
    )(x, gate, final_g, y, y)


def _final_norm_kernel(x_ref, g_ref, o_ref):
    x = x_ref[...]
    ms = jnp.mean(x * x, axis=-1, keepdims=True)
    o_ref[...] = (x * lax.rsqrt(ms + EPS)) * g_ref[...]


def final_norm(x, g):
    n, d = x.shape
    tm = min(TM_PROJ, n)
    return pl.pallas_call(
        _final_norm_kernel,
        out_shape=jax.ShapeDtypeStruct((n, d), F32),
        grid=(n // tm,),
        in_specs=[pl.BlockSpec((tm, d), lambda i: (i, 0)), pl.BlockSpec((1, d), lambda i: (0, 0))],
        out_specs=pl.BlockSpec((tm, d), lambda i: (i, 0)),
        compiler_params=_cparams(("arbitrary",)),
        name="final_norm",
    )(x, g)


def _moe_plan(top_idx, bm):
    n = top_idx.shape[0]
    a = n * TOP_K
    e_flat = top_idx.reshape(a)
    onehot = (e_flat[:, None] == jnp.arange(N_EXPERTS, dtype=jnp.int32)[None, :]).astype(jnp.int32)
    incl = jnp.cumsum(onehot, axis=0)
    rank = jnp.sum((incl - onehot) * onehot, axis=1)
    counts = incl[-1]
    padded = (counts + bm - 1) // bm * bm
    pad_end = jnp.cumsum(padded)
    pad_start = pad_end - padded
    dest = (jnp.sum(pad_start[None, :] * onehot, axis=1) + rank).astype(jnp.int32)
    p_rows = a + N_EXPERTS * bm
    n_blocks = p_rows // bm
    n_used = (pad_end[-1] // bm).astype(jnp.int32)
    blk_start = jnp.arange(n_blocks, dtype=jnp.int32) * bm
    blk_expert = jnp.sum((blk_start[:, None] >= pad_end[None, :]).astype(jnp.int32), axis=1)
    blk_expert = jnp.minimum(blk_expert, N_EXPERTS - 1)
    last_e = blk_expert[jnp.maximum(n_used - 1, 0)]
    blk_expert = jnp.where(jnp.arange(n_blocks) < n_used, blk_expert, last_e).astype(jnp.int32)
    return dest, blk_expert, n_used.reshape(1), p_rows


def _head_block_mask(width):
    r = jnp.arange(width) // HEAD_DIM
    return (r[:, None] == r[None, :]).astype(BF16)


def kernel(x, attn_norm, w_in, hgrn_lb, hgrn_norm, mlstm_conv, mlstm_b_i, mlstm_b_f, mlstm_norm, fox_b_f, w_out, ffn_norm, dense_w_gate, dense_w_up, dense_w_down, router, moe_w_gate, moe_w_up, moe_w_down, final_norm_g):
    batch, seq, d = x.shape
    depth = w_in.shape[0]
    n = batch * seq
    xf = x.reshape(n, d)
    m_bf = _head_block_mask(HG_W)
    n_main_a = 4 * HG_W + 3 * ML_W
    gate_a = n_main_a
    mo_a = gate_a + 2 * ML_HEADS
    fx_a = mo_a + ML_W
    ff_a = fx_a + 3 * FX_W
    done = False
    for l in range(depth):
        wl = w_in[l]
        wq_fx = wl[:, fx_a:fx_a + FX_W] * (LOG2E * HEAD_DIM ** -0.5)
        w_main = jnp.concatenate([wl[:, :n_main_a], wl[:, mo_a:fx_a], wq_fx, wl[:, fx_a + FX_W:ff_a]],
                                 axis=1).astype(BF16)
        w_gate_t = jnp.concatenate([wl[:, gate_a:mo_a], wl[:, ff_a:]], axis=1).T.astype(BF16)
        main, g_row_raw = norm_inproj(xf, attn_norm[l].reshape(1, d), w_main, w_gate_t)
        bias = jnp.concatenate([mlstm_b_i[l], mlstm_b_f[l], fox_b_f[l]]).reshape(N_GATE_ROWS, 1).astype(F32)
        g_row, g_col, c_aug = gates(g_row_raw, bias, batch)
        o_hg = hgrn2(main, hgrn_lb.astype(F32), hgrn_norm[l].reshape(1, HG_W), m_bf, batch, l)
        o_ml = mlstm(main, g_row, g_col, mlstm_conv[l], mlstm_norm[l].reshape(1, ML_W), m_bf, batch)
        o_fx = fox(main, c_aug, batch)
        wo = w_out[l].astype(BF16)
        fg = ffn_norm[l].reshape(1, d)
        j = l // 2
        if l % 2 == 0:
            xf = outproj_ffn(xf, o_hg, o_ml, o_fx, wo, fg, dense_w_gate[j].astype(BF16),
                             dense_w_up[j].astype(BF16), dense_w_down[j].astype(BF16))
        else:
            r = jnp.pad(router[j].T, ((0, ROUTER_ROWS - N_EXPERTS), (0, 0)))
            r_hi = r.astype(BF16)
            r_lo = (r - r_hi.astype(F32)).astype(BF16)
            x_res, h, idx_rows, gate_rows = outproj_route(xf, o_hg, o_ml, o_fx, wo, fg, r_hi, r_lo)
            gate = gate_rows[:TOP_K].T
            dest, blk_expert, n_used, p_rows = _moe_plan(idx_rows[:TOP_K].T, BM_MOE)
            src_rows, dst_rows = moe_invert(dest, p_rows, n, BM_MOE)
            y = moe_experts(src_rows, dst_rows, blk_expert, n_used, h, moe_w_gate[j].astype(BF16),
                            moe_w_up[j].astype(BF16), moe_w_down[j].astype(BF16))
            done = l == depth - 1
            xf = moe_combine(x_res, gate, final_norm_g.reshape(1, d), y, done)
    if not done:
        xf = final_norm(xf, final_norm_g.reshape(1, d))
    return xf.reshape(batch, seq, d)
```
